```python
import math
import jax
import jax.numpy as jnp
from jax import lax
import numpy as np

D_MODEL = 1024
BATCH = 8
SEQ = 2048
DEPTH = 1
DEC_BATCH = 32
DEC_SEQ = 1
PAST_LEN = 16384
PAGE_SIZE = 128

MIX_WIDTH = D_MODEL
ATTN_WIDTH = MIX_WIDTH // 2
CONV_CH = MIX_WIDTH - ATTN_WIDTH
DIFF_HEADS = 4
V_DIM = ATTN_WIDTH // DIFF_HEADS
QK_DIM = V_DIM // 2
QK_COLS = DIFF_HEADS * 2 * QK_DIM
IN_COLS = 2 * QK_COLS + ATTN_WIDTH + 2 * CONV_CH
CONV_K = 31
ROPE_THETA = 10000.0
Q_BLOCK = 128
N_EXPERTS = 32
TOP_K = 4
D_FF = D_MODEL
SWIGLU_LIMIT = 7.0
SWIGLU_ALPHA = 1.702
EPS = 1e-6
NEG = -1e30

kernel_name = "hymba_conformer_diffattn_moe_step"


def _rmsnorm(x, w):
    xf = x.astype(jnp.float32)
    y = xf * lax.rsqrt(jnp.mean(xf * xf, axis=-1, keepdims=True) + EPS)
    return (y * w.astype(jnp.float32)).astype(x.dtype)


def _layernorm(x, g, b):
    xf = x.astype(jnp.float32)
    mu = jnp.mean(xf, axis=-1, keepdims=True)
    xc = xf - mu
    y = xc * lax.rsqrt(jnp.mean(xc * xc, axis=-1, keepdims=True) + EPS)
    return (y * g.astype(jnp.float32) + b.astype(jnp.float32)).astype(x.dtype)


def _rope(x, pos):
    half = QK_DIM // 2
    inv = jnp.power(ROPE_THETA, -jnp.arange(half, dtype=jnp.float32) / half)
    ang = pos[:, None] * inv[None, :]
    cos = jnp.cos(ang)[None, :, None, None, :]
    sin = jnp.sin(ang)[None, :, None, None, :]
    xf = x.astype(jnp.float32)
    x1, x2 = xf[..., :half], xf[..., half:]
    return jnp.concatenate([x1 * cos - x2 * sin, x2 * cos + x1 * sin], axis=-1).astype(x.dtype)


def _project(x, pos, norm1_w, w_in, q_norm_w, k_norm_w):
    B, S, _ = x.shape
    h = _rmsnorm(x, norm1_w)
    p = h @ w_in
    q = p[..., :QK_COLS].reshape(B, S, DIFF_HEADS, 2, QK_DIM)
    k = p[..., QK_COLS:2 * QK_COLS].reshape(B, S, DIFF_HEADS, 2, QK_DIM)
    v = p[..., 2 * QK_COLS:2 * QK_COLS + ATTN_WIDTH].reshape(B, S, DIFF_HEADS, V_DIM)
    ua = p[..., IN_COLS - 2 * CONV_CH:IN_COLS - CONV_CH]
    ub = p[..., IN_COLS - CONV_CH:]
    u = ua * jax.nn.sigmoid(ub)
    q = _rope(_rmsnorm(q, q_norm_w), pos)
    k = _rope(_rmsnorm(k, k_norm_w), pos)
    return q, k, v, u


def _lambda(lq1, lk1, lq2, lk2, lam_init):
    f = jnp.float32
    return (jnp.exp(jnp.sum(lq1.astype(f) * lk1.astype(f)))
            - jnp.exp(jnp.sum(lq2.astype(f) * lk2.astype(f))) + lam_init)


def _scores(q, k):
    s = jnp.einsum('bqhcd,bkhcd->bhcqk', q, k, preferred_element_type=jnp.float32)
    return s * (QK_DIM ** -0.5)


def _diff_weights(s, lam):
    p = jax.nn.softmax(s, axis=-1)
    return p[:, :, 0] - lam * p[:, :, 1]


def _diff_attn_prompt(q, k, v, lam):
    B, S = q.shape[0], q.shape[1]
    nb = S // Q_BLOCK
    qb = jnp.moveaxis(q.reshape(B, nb, Q_BLOCK, DIFF_HEADS, 2, QK_DIM), 1, 0)
    vf = v.astype(jnp.float32)
    kpos = jnp.arange(S)

    def block(args):
        qblk, i = args
        qpos = i * Q_BLOCK + jnp.arange(Q_BLOCK)
        s = jnp.where(kpos[None, :] <= qpos[:, None], _scores(qblk, k), NEG)
        a = _diff_weights(s, lam)
        return jnp.einsum('bhqk,bkhd->bqhd', a, vf)

    o = lax.map(block, (qb, jnp.arange(nb)))
    return jnp.moveaxis(o, 0, 1).reshape(B, S, DIFF_HEADS, V_DIM)


def _diff_attn_sample(q, k_new, v_new, k_past, v_past, lam):
    sq = q.shape[1]
    n_past = k_past.shape[1]
    causal = jnp.arange(sq)[None, :] <= jnp.arange(sq)[:, None]
    s = jnp.concatenate([_scores(q, k_past), jnp.where(causal, _scores(q, k_new), NEG)], axis=-1)
    a = _diff_weights(s, lam)
    o_past = jnp.einsum('bhqk,bkhd->bqhd', a[..., :n_past], v_past.astype(jnp.float32))
    o_new = jnp.einsum('bhqk,bkhd->bqhd', a[..., n_past:], v_new.astype(jnp.float32))
    return o_past + o_new


def _diff_head_out(o, subln_w, lam_init, dtype):
    B, S = o.shape[0], o.shape[1]
    o = _rmsnorm(o, subln_w) * (1.0 - lam_init)
    return o.reshape(B, S, ATTN_WIDTH).astype(dtype)


def _conv_mixer(u_ext, conv_w, conv_b, ln_g, ln_b):
    y = lax.conv_general_dilated(u_ext, conv_w[:, None, :].astype(u_ext.dtype),
                                 window_strides=(1,), padding='VALID',
                                 dimension_numbers=('NWC', 'WIO', 'NWC'),
                                 feature_group_count=CONV_CH)
    y = y + conv_b.astype(y.dtype)
    return jax.nn.silu(_layernorm(y, ln_g, ln_b))


def _moe(h, router_w, router_b, w_gate_up, b_gate_up, w_down, b_down):
    T, D = h.shape
    logits = (h @ router_w).astype(jnp.float32) + router_b.astype(jnp.float32)
    top_vals, top_idx = lax.top_k(logits, TOP_K)
    gates = jax.nn.softmax(top_vals, axis=-1)
    n_assign = T * TOP_K
    blk = max(8, min(256, n_assign // N_EXPERTS))
    n_blocks = -(-n_assign // blk) + N_EXPERTS
    flat_e = top_idx.reshape(-1).astype(jnp.int32)
    order = jnp.argsort(flat_e)
    sorted_e = flat_e[order]
    tok = (order // TOP_K).astype(jnp.int32)
    counts = jnp.bincount(flat_e, length=N_EXPERTS).astype(jnp.int32)
    padded = (counts + blk - 1) // blk * blk
    pad_end = jnp.cumsum(padded)
    pad_start = pad_end - padded
    start = jnp.cumsum(counts) - counts
    dest = pad_start[sorted_e] + jnp.arange(n_assign, dtype=jnp.int32) - start[sorted_e]
    row_tok = jnp.zeros((n_blocks * blk,), jnp.int32).at[dest].set(tok)
    xs = h[row_tok].reshape(n_blocks, blk, D)
    blk_e = jnp.minimum(jnp.searchsorted(pad_end, jnp.arange(n_blocks, dtype=jnp.int32) * blk,
                                         side='right'), N_EXPERTS - 1)

    def expert_block(args):
        xb, e = args
        gu = xb @ w_gate_up[e] + b_gate_up[e]
        g = jnp.minimum(gu[:, :D_FF], SWIGLU_LIMIT)
        u = jnp.clip(gu[:, D_FF:], -SWIGLU_LIMIT, SWIGLU_LIMIT)
        act = (u + 1) * g * jax.nn.sigmoid(SWIGLU_ALPHA * g)
        return act @ w_down[e] + b_down[e]

    ys = lax.map(expert_block, (xs, blk_e)).reshape(n_blocks * blk, D)
    contrib = ys[dest].astype(jnp.float32) * gates.reshape(-1)[order][:, None]
    return jax.ops.segment_sum(contrib, tok, num_segments=T).astype(h.dtype)


def _layer_tail(x, attn, conv, w_out, norm2_w, router_w, router_b, w_gate_up, b_gate_up, w_down, b_down):
    B, S, D = x.shape
    x = x + jnp.concatenate([attn, conv], axis=-1) @ w_out
    h = _rmsnorm(x, norm2_w).reshape(B * S, D)
    return x + _moe(h, router_w, router_b, w_gate_up, b_gate_up, w_down, b_down).reshape(B, S, D)


def setup_inputs(seed: int = 0) -> dict:
    key = jax.random.key(seed)
    it = iter(list(jax.random.split(key, 32)))
    f32 = jnp.float32

    def nrm(shape, scale):
        return scale * jax.random.normal(next(it), shape, f32)

    n_pages = PAST_LEN // PAGE_SIZE
    n_used = DEC_BATCH * n_pages
    n_phys = n_used + max(1, n_used // 4)
    x_prompt = nrm((BATCH, SEQ, D_MODEL), 1.0)
    x_sample = nrm((DEC_BATCH, DEC_SEQ, D_MODEL), 1.0)
    cache_k = nrm((DEPTH, n_phys, PAGE_SIZE, DIFF_HEADS, 2, QK_DIM), 1.0)
    cache_v = nrm((DEPTH, n_phys, PAGE_SIZE, DIFF_HEADS, V_DIM), 1.0)
    state_conv = nrm((DEPTH, DEC_BATCH, CONV_K - 1, CONV_CH), 0.5)
    page_table = jax.random.permutation(next(it), n_phys)[:n_used].reshape(DEC_BATCH, n_pages).astype(jnp.int32)
    return {
        "x_prompt": x_prompt,
        "x_sample": x_sample,
        "cache_k": cache_k,
        "cache_v": cache_v,
        "state_conv": state_conv,
        "page_table": page_table,
        "norm1_w": 1.0 + nrm((DEPTH, D_MODEL), 0.02),
        "w_in": nrm((DEPTH, D_MODEL, IN_COLS), D_MODEL ** -0.5),
        "q_norm_w": 1.0 + nrm((DEPTH, QK_DIM), 0.02),
        "k_norm_w": 1.0 + nrm((DEPTH, QK_DIM), 0.02),
        "lambda_q1": nrm((DEPTH, QK_DIM), 0.1),
        "lambda_k1": nrm((DEPTH, QK_DIM), 0.1),
        "lambda_q2": nrm((DEPTH, QK_DIM), 0.1),
        "lambda_k2": nrm((DEPTH, QK_DIM), 0.1),
        "subln_w": 1.0 + nrm((DEPTH, V_DIM), 0.02),
        "conv_w": nrm((DEPTH, CONV_K, CONV_CH), CONV_K ** -0.5),
        "conv_b": nrm((DEPTH, CONV_CH), 0.01),
        "conv_ln_g": 1.0 + nrm((DEPTH, CONV_CH), 0.02),
        "conv_ln_b": nrm((DEPTH, CONV_CH), 0.01),
        "w_out": nrm((DEPTH, MIX_WIDTH, D_MODEL), MIX_WIDTH ** -0.5),
        "norm2_w": 1.0 + nrm((DEPTH, D_MODEL), 0.02),
        "router_w": nrm((DEPTH, D_MODEL, N_EXPERTS), D_MODEL ** -0.5),
        "router_b": nrm((DEPTH, N_EXPERTS), 0.01),
        "w_gate_up": nrm((DEPTH, N_EXPERTS, D_MODEL, 2 * D_FF), D_MODEL ** -0.5),
        "b_gate_up": nrm((DEPTH, N_EXPERTS, 2 * D_FF), 0.01),
        "w_down": nrm((DEPTH, N_EXPERTS, D_FF, D_MODEL), D_FF ** -0.5),
        "b_down": nrm((DEPTH, N_EXPERTS, D_MODEL), 0.01),
    }


def reference(x_prompt, x_sample, cache_k, cache_v, state_conv, page_table, norm1_w, w_in,
              q_norm_w, k_norm_w, lambda_q1, lambda_k1, lambda_q2, lambda_k2, subln_w,
              conv_w, conv_b, conv_ln_g, conv_ln_b, w_out, norm2_w, router_w, router_b,
              w_gate_up, b_gate_up, w_down, b_down):
    bs, ss = x_sample.shape[0], x_sample.shape[1]
    n_past = page_table.shape[1] * cache_k.shape[2]
    pos_p = jnp.arange(x_prompt.shape[1], dtype=jnp.float32)
    pos_s = n_past + jnp.arange(ss, dtype=jnp.float32)
    xp, xs = x_prompt, x_sample
    kp_l, vp_l, cp_l, ks_l, vs_l, cs_l = [], [], [], [], [], []
    for l in range(DEPTH):
        lam_init = 0.8 - 0.6 * math.exp(-0.3 * l)
        lam = _lambda(lambda_q1[l], lambda_k1[l], lambda_q2[l], lambda_k2[l], lam_init)
        tail = (w_out[l], norm2_w[l], router_w[l], router_b[l], w_gate_up[l], b_gate_up[l], w_down[l], b_down[l])
        conv_p = (conv_w[l], conv_b[l], conv_ln_g[l], conv_ln_b[l])

        q, k, v, u = _project(xp, pos_p, norm1_w[l], w_in[l], q_norm_w[l], k_norm_w[l])
        attn = _diff_head_out(_diff_attn_prompt(q, k, v, lam), subln_w[l], lam_init, xp.dtype)
        u_ext = jnp.pad(u, ((0, 0), (CONV_K - 1, 0), (0, 0)))
        conv = _conv_mixer(u_ext, *conv_p)
        xp = _layer_tail(xp, attn, conv, *tail)
        kp_l.append(k)
        vp_l.append(v)
        cp_l.append(u_ext[:, -(CONV_K - 1):])

        q, k, v, u = _project(xs, pos_s, norm1_w[l], w_in[l], q_norm_w[l], k_norm_w[l])
        k_past = cache_k[l][page_table].reshape(bs, n_past, DIFF_HEADS, 2, QK_DIM)
        v_past = cache_v[l][page_table].reshape(bs, n_past, DIFF_HEADS, V_DIM)
        attn = _diff_head_out(_diff_attn_sample(q, k, v, k_past, v_past, lam), subln_w[l], lam_init, xs.dtype)
        u_ext = jnp.concatenate([state_conv[l].astype(u.dtype), u], axis=1)
        conv = _conv_mixer(u_ext, *conv_p)
        xs = _layer_tail(xs, attn, conv, *tail)
        ks_l.append(k)
        vs_l.append(v)
        cs_l.append(u_ext[:, -(CONV_K - 1):])

    new_k_prompt = jnp.stack(kp_l)
    new_v_prompt = jnp.stack(vp_l)
    new_conv_prompt = jnp.stack(cp_l)
    new_k_sample = jnp.stack(ks_l)
    new_v_sample = jnp.stack(vs_l)
    new_conv_sample = jnp.stack(cs_l)
    return (xp, xs, new_k_prompt, new_v_prompt, new_conv_prompt, new_k_sample, new_v_sample, new_conv_sample)
```

```python
import functools
import math

import jax
import jax.numpy as jnp
from jax import lax
from jax.experimental import pallas as pl
from jax.experimental.pallas import tpu as pltpu

F32 = jnp.float32
BF16 = jnp.bfloat16
HIGHEST = lax.Precision.HIGHEST

EPS = 1e-6
ROPE_THETA = 10000.0
SWIGLU_LIMIT = 7.0
SWIGLU_ALPHA = 1.702
TOP_K = 4
NEG = -1e30
QK_GROUP = 64

V7X_LANES = 128
V7X_VMEM_BYTES = 64 * 1024 * 1024
VMEM_LIMIT = 56 * 1024 * 1024
BF16_ROWS = 16

MOE_TILE = 256
CHUNK = BF16_ROWS
MOE_BLOCK = 512
CHUNKS_PER_BLOCK = MOE_BLOCK // CHUNK


def _cparams(sem, vmem=VMEM_LIMIT):
    return pltpu.CompilerParams(dimension_semantics=sem, vmem_limit_bytes=vmem)


def _dot(a, b, exact=False):
    if exact:
        return jnp.dot(a.astype(F32), b.astype(F32), precision=HIGHEST,
                       preferred_element_type=F32)
    return jnp.dot(a.astype(BF16), b.astype(BF16), preferred_element_type=F32)


def _dot_nt(a, b, exact=False):
    dn = (((1,), (1,)), ((), ()))
    if exact:
        return lax.dot_general(a.astype(F32), b.astype(F32), dn, precision=HIGHEST,
                               preferred_element_type=F32)
    return lax.dot_general(a.astype(BF16), b.astype(BF16), dn, preferred_element_type=F32)


def _bf16_round(x):
    return x.astype(BF16).astype(F32)


def _rope_norm(p, gsum, w, cos, sin, first_half, exact_norm):
    ss = _dot(p * p, gsum, exact_norm)
    n = p * lax.rsqrt(ss * (1.0 / QK_GROUP) + EPS) * w
    outs = []
    for j in range(p.shape[1] // V7X_LANES):
        nj = n[:, j * V7X_LANES:(j + 1) * V7X_LANES]
        rot = jnp.where(first_half, pltpu.roll(nj, V7X_LANES - 32, 1), pltpu.roll(nj, 32, 1))
        outs.append(nj * cos + rot * sin)
    return jnp.concatenate(outs, axis=1)


def _proj_kernel(x_ref, n1_ref, w_ref, qw_ref, kw_ref, cos_ref, sin_ref, gsum_ref,
                 q_ref, k_ref, v_ref, u_ref, *rest, qc, vc, cc, scale, exact_norm):
    x = x_ref[...]
    h = x * lax.rsqrt(jnp.mean(x * x, axis=-1, keepdims=True) + EPS) * n1_ref[...]
    hm = h.astype(BF16)
    cos = cos_ref[...]
    sin = sin_ref[...]
    lane = lax.broadcasted_iota(jnp.int32, cos.shape, 1)
    first_half = (lane % QK_GROUP) < QK_GROUP // 2
    gsum = gsum_ref[...]

    q = _rope_norm(_dot(hm, w_ref[:, 0:qc]), gsum, qw_ref[...], cos, sin, first_half, exact_norm)
    q_ref[...] = (q * scale).astype(q_ref.dtype)
    k = _rope_norm(_dot(hm, w_ref[:, qc:2 * qc]), gsum, kw_ref[...], cos, sin, first_half,
                   exact_norm)
    k_ref[...] = k
    v = _dot(hm, w_ref[:, 2 * qc:2 * qc + vc])
    v_ref[...] = v
    o = 2 * qc + vc
    ua = _dot(hm, w_ref[:, o:o + cc])
    ub = _dot(hm, w_ref[:, o + cc:o + 2 * cc])
    u_ref[...] = ua * jax.nn.sigmoid(ub)
    if rest:
        kb_ref, vb_ref = rest
        kb_ref[...] = k.astype(BF16)
        vb_ref[...] = v.astype(BF16)


def _rope_tables(pos):
    half = 32
    inv = jnp.power(ROPE_THETA, -jnp.arange(half, dtype=F32) / half)
    ang = pos[:, None] * inv[None, :]
    cos = jnp.tile(jnp.cos(ang), (1, 4))
    s = jnp.sin(ang)
    sin = jnp.tile(jnp.concatenate([-s, s], axis=1), (1, 2))
    return cos, sin


def _proj(x2d, pos_rows, n_pos_blocks, tm, norm1_w, w_in, q_norm_w, k_norm_w, *, qc, vc, cc,
          exact_norm, bf16_kv):
    T, D = x2d.shape
    cos, sin = _rope_tables(pos_rows)
    gi = jnp.arange(qc) // QK_GROUP
    gsum = (gi[:, None] == gi[None, :]).astype(F32 if exact_norm else BF16)
    qw = jnp.tile(q_norm_w, qc // QK_GROUP)[None, :]
    kw = jnp.tile(k_norm_w, qc // QK_GROUP)[None, :]
    w = w_in.astype(BF16)
    row = lambda i: (i, 0)
    full = lambda i: (0, 0)
    out_shape = [jax.ShapeDtypeStruct((T, qc), BF16),
                 jax.ShapeDtypeStruct((T, qc), F32),
                 jax.ShapeDtypeStruct((T, vc), F32),
                 jax.ShapeDtypeStruct((T, cc), F32)]
    out_specs = [pl.BlockSpec((tm, qc), row), pl.BlockSpec((tm, qc), row),
                 pl.BlockSpec((tm, vc), row), pl.BlockSpec((tm, cc), row)]
    if bf16_kv:
        out_shape += [jax.ShapeDtypeStruct((T, qc), BF16), jax.ShapeDtypeStruct((T, vc), BF16)]
        out_specs += [pl.BlockSpec((tm, qc), row), pl.BlockSpec((tm, vc), row)]
    return pl.pallas_call(
        functools.partial(_proj_kernel, qc=qc, vc=vc, cc=cc, scale=QK_GROUP ** -0.5,
                          exact_norm=exact_norm),
        grid=(T // tm,),
        in_specs=[pl.BlockSpec((tm, D), row),
                  pl.BlockSpec((1, D), full),
                  pl.BlockSpec(w.shape, full),
                  pl.BlockSpec((1, qc), full),
                  pl.BlockSpec((1, qc), full),
                  pl.BlockSpec((tm, V7X_LANES), lambda i: (i % n_pos_blocks, 0)),
                  pl.BlockSpec((tm, V7X_LANES), lambda i: (i % n_pos_blocks, 0)),
                  pl.BlockSpec((qc, qc), full)],
        out_specs=out_specs,
        out_shape=out_shape,
        compiler_params=_cparams(("arbitrary",)),
        name="proj" if bf16_kv else "proj_step",
    )(x2d, norm1_w[None, :], w, qw, kw, cos, sin, gsum)


def _lambda_value(lv, lam_init):
    a = jnp.sum(lv[0:1] * lv[1:2], axis=-1, keepdims=True)
    b = jnp.sum(lv[2:3] * lv[3:4], axis=-1, keepdims=True)
    return jnp.exp(a) - jnp.exp(b) + lam_init


def _subln(o, w, lam_init):
    y = o * lax.rsqrt(jnp.mean(o * o, axis=-1, keepdims=True) + EPS)
    return y * w * (1.0 - lam_init)


def _attn_kernel(lam_ref, sw_ref, q_ref, k_ref, v_ref, o_ref, *, tq, lam_init):
    i = pl.program_id(2)
    lam = _lambda_value(lam_ref[...], lam_init)
    q = q_ref[...]
    lane = lax.broadcasted_iota(jnp.int32, q.shape, 1)
    zero = jnp.zeros_like(q)
    qs = (jnp.where(lane < QK_GROUP, q, zero), jnp.where(lane >= QK_GROUP, q, zero))

    def chunk(j, carry, masked):
        start = pl.multiple_of(j * tq, tq)
        kc = k_ref[pl.ds(start, tq), :]
        vc = v_ref[pl.ds(start, tq), :]
        out = []
        for c in range(2):
            m, l, acc = carry[c]
            s = _dot_nt(qs[c], kc)
            if masked:
                row = lax.broadcasted_iota(jnp.int32, s.shape, 0)
                col = lax.broadcasted_iota(jnp.int32, s.shape, 1)
                s = jnp.where(col <= row, s, NEG)
            m_new = jnp.maximum(m, jnp.max(s, axis=-1, keepdims=True))
            p = jnp.exp(s - m_new)
            alpha = jnp.exp(m - m_new)
            l = alpha * l + jnp.sum(p, axis=-1, keepdims=True)
            acc = alpha * acc + _dot(p, vc)
            out.append((m_new, l, acc))
        return tuple(out)

    init = tuple((jnp.full((tq, 1), NEG, F32), jnp.zeros((tq, 1), F32),
                  jnp.zeros((tq, V7X_LANES), F32)) for _ in range(2))
    carry = lax.fori_loop(0, i, lambda j, c: chunk(j, c, False), init)
    (_, l0, a0), (_, l1, a1) = chunk(i, carry, True)
    o = a0 / l0 - lam * (a1 / l1)
    o_ref[...] = _subln(o, sw_ref[...], lam_init).astype(o_ref.dtype)


def _attn_prompt(q, kb, vb, lamv, subln_w, *, batch, seq, heads, lam_init, tq=256):
    nq = seq // tq
    return pl.pallas_call(
        functools.partial(_attn_kernel, tq=tq, lam_init=lam_init),
        grid=(batch, heads, nq),
        in_specs=[pl.BlockSpec(lamv.shape, lambda b, h, i: (0, 0)),
                  pl.BlockSpec((1, V7X_LANES), lambda b, h, i: (0, 0)),
                  pl.BlockSpec((tq, V7X_LANES), lambda b, h, i: (b * nq + i, h)),
                  pl.BlockSpec((seq, V7X_LANES), lambda b, h, i: (b, h)),
                  pl.BlockSpec((seq, V7X_LANES), lambda b, h, i: (b, h))],
        out_specs=pl.BlockSpec((tq, V7X_LANES), lambda b, h, i: (b * nq + i, h)),
        out_shape=jax.ShapeDtypeStruct((batch * seq, heads * V7X_LANES), BF16),
        compiler_params=_cparams(("arbitrary", "arbitrary", "arbitrary")),
        name="attn",
    )(lamv, subln_w[None, :], q, kb, vb)


CONV_HALO = 32


def _ln_swish(y, b_ref, g_ref, be_ref):
    y = y + b_ref[...]
    mu = jnp.mean(y, axis=-1, keepdims=True)
    yc = y - mu
    z = yc * lax.rsqrt(jnp.mean(yc * yc, axis=-1, keepdims=True) + EPS) * g_ref[...] + be_ref[...]
    return z * jax.nn.sigmoid(z)


def _conv_kernel(u_ref, w_ref, b_ref, g_ref, be_ref, o_ref, buf_ref, *, tc, taps):
    @pl.when(pl.program_id(1) == 0)
    def _():
        buf_ref[0:CONV_HALO, :] = jnp.zeros((CONV_HALO, buf_ref.shape[1]), F32)

    buf_ref[CONV_HALO:CONV_HALO + tc, :] = _bf16_round(u_ref[...])
    acc = jnp.zeros((tc, buf_ref.shape[1]), F32)
    for k in range(taps):
        acc = acc + (buf_ref[pl.ds(CONV_HALO - (taps - 1) + k, tc), :]
                     * _bf16_round(w_ref[k:k + 1, :]))
    o_ref[...] = _ln_swish(acc, b_ref, g_ref, be_ref).astype(o_ref.dtype)
    buf_ref[0:CONV_HALO, :] = buf_ref[tc:tc + CONV_HALO, :]


def _conv_prompt(u2d, conv_w, conv_b, ln_g, ln_b, *, batch, seq, tc=512):
    taps, C = conv_w.shape
    nt = seq // tc
    vec = lambda b, i: (0, 0)
    return pl.pallas_call(
        functools.partial(_conv_kernel, tc=tc, taps=taps),
        grid=(batch, nt),
        in_specs=[pl.BlockSpec((tc, C), lambda b, i: (b * nt + i, 0)),
                  pl.BlockSpec((taps, C), vec), pl.BlockSpec((1, C), vec),
                  pl.BlockSpec((1, C), vec), pl.BlockSpec((1, C), vec)],
        out_specs=pl.BlockSpec((tc, C), lambda b, i: (b * nt + i, 0)),
        out_shape=jax.ShapeDtypeStruct((batch * seq, C), BF16),
        scratch_shapes=[pltpu.VMEM((tc + CONV_HALO, C), F32)],
        compiler_params=_cparams(("arbitrary", "arbitrary")),
        name="conv",
    )(u2d, conv_w, conv_b[None, :], ln_g[None, :], ln_b[None, :])


def _conv_step_kernel(st_ref, u_ref, w_ref, b_ref, g_ref, be_ref, o_ref, *, taps):
    acc = _bf16_round(u_ref[...]) * _bf16_round(w_ref[taps - 1:taps, :])
    for k in range(taps - 1):
        acc = acc + _bf16_round(st_ref[:, k, :]) * _bf16_round(w_ref[k:k + 1, :])
    o_ref[...] = _ln_swish(acc, b_ref, g_ref, be_ref)


def _conv_step(state, u, conv_w, conv_b, ln_g, ln_b):
    taps, C = conv_w.shape
    return pl.pallas_call(
        functools.partial(_conv_step_kernel, taps=taps),
        out_shape=jax.ShapeDtypeStruct(u.shape, F32),
        compiler_params=_cparams(None),
        name="conv_step",
    )(state, u, conv_w, conv_b[None, :], ln_g[None, :], ln_b[None, :])


def _tail_kernel(a_ref, c_ref, x_ref, wo_ref, n2_ref, rw_ref, rb_ref,
                 x1_ref, h_ref, posg_ref, cnt_ref):
    tm = x_ref.shape[0]
    half = a_ref.shape[1]
    x1 = (x_ref[...] + _dot(a_ref[...], wo_ref[0:half, :])
          + _dot(c_ref[...], wo_ref[half:, :]))
    x1_ref[...] = x1
    h = (x1 * lax.rsqrt(jnp.mean(x1 * x1, axis=-1, keepdims=True) + EPS)
         * n2_ref[...]).astype(BF16)
    h_ref[...] = h
    logits = _dot_nt(rw_ref[...], h) + rb_ref[...]
    ne = logits.shape[0]
    eidx = lax.broadcasted_iota(jnp.int32, logits.shape, 0)

    sels, vals = [], []
    l = logits
    for _ in range(TOP_K):
        m = jnp.max(l, axis=0, keepdims=True)
        first = jnp.min(jnp.where(l == m, eidx, ne), axis=0, keepdims=True)
        sel = eidx == first
        l = jnp.where(sel, -jnp.inf, l)
        sels.append(sel)
        vals.append(m)
    ex = [jnp.exp(v - vals[0]) for v in vals]
    den = ex[0] + ex[1] + ex[2] + ex[3]
    gates = [e / den for e in ex]

    msel = jnp.zeros(logits.shape, F32)
    for sel in sels:
        msel = msel + jnp.where(sel, 1.0, 0.0)
    r0 = lax.broadcasted_iota(jnp.int32, (tm, tm), 0)
    r1 = lax.broadcasted_iota(jnp.int32, (tm, tm), 1)
    upper = jnp.where(r0 < r1, 1.0, 0.0).astype(BF16)
    rank = jnp.dot(msel.astype(BF16), upper, preferred_element_type=F32)
    cnt = jnp.sum(msel, axis=1, keepdims=True)
    pcnt = jnp.ceil(cnt * (1.0 / CHUNK)) * CHUNK
    e0 = lax.broadcasted_iota(jnp.int32, (ne, ne), 0)
    e1 = lax.broadcasted_iota(jnp.int32, (ne, ne), 1)
    lower = jnp.where(e1 < e0, 1.0, 0.0)
    off = jnp.dot(lower.astype(BF16), jnp.broadcast_to(pcnt, (ne, V7X_LANES)).astype(BF16),
                  preferred_element_type=F32)[:, 0:1]
    pos = off + rank
    rows = [jnp.sum(jnp.where(sel, pos, 0.0), axis=0, keepdims=True) for sel in sels]
    posg_ref[0] = jnp.concatenate(rows + gates, axis=0)
    cnt_ref[0] = jnp.broadcast_to(cnt, (ne, V7X_LANES)).astype(jnp.int32)


def _tail(attn, conv, x2d, w_out, norm2_w, router_w, router_b, *, tm):
    T, D = x2d.shape
    half = attn.shape[1]
    ne = router_w.shape[1]
    nt = T // tm
    wo = w_out.astype(BF16)
    rw = router_w.T.astype(BF16)
    row = lambda i: (i, 0)
    full = lambda i: (0, 0)
    return pl.pallas_call(
        _tail_kernel,
        grid=(nt,),
        in_specs=[pl.BlockSpec((tm, half), row), pl.BlockSpec((tm, half), row),
                  pl.BlockSpec((tm, D), row), pl.BlockSpec((D, D), full),
                  pl.BlockSpec((1, D), full), pl.BlockSpec((ne, D), full),
                  pl.BlockSpec((ne, 1), full)],
        out_specs=[pl.BlockSpec((tm, D), row), pl.BlockSpec((tm, D), row),
                   pl.BlockSpec((1, 2 * TOP_K, tm), lambda i: (i, 0, 0)),
                   pl.BlockSpec((1, ne, V7X_LANES), lambda i: (i, 0, 0))],
        out_shape=[jax.ShapeDtypeStruct((T, D), F32), jax.ShapeDtypeStruct((T, D), BF16),
                   jax.ShapeDtypeStruct((nt, 2 * TOP_K, tm), F32),
                   jax.ShapeDtypeStruct((nt, ne, V7X_LANES), jnp.int32)],
        compiler_params=_cparams(("arbitrary",)),
        name="tail",
    )(attn, conv, x2d, wo, norm2_w[None, :], rw, router_b[:, None])


def _slots(tm, ne):
    worst = TOP_K * tm + ne * (CHUNK - 1)
    return -(-worst // V7X_LANES) * V7X_LANES


def _moe_tables(cnt, nb):
    ne = cnt.shape[1]
    nch = (cnt + (CHUNK - 1)) // CHUNK
    tot = jnp.sum(nch, axis=0)
    nblk = (tot + (CHUNKS_PER_BLOCK - 1)) // CHUNKS_PER_BLOCK
    bend = jnp.cumsum(nblk)
    gstart = (bend - nblk) * CHUNKS_PER_BLOCK
    rs = gstart[None, :] + jnp.cumsum(nch, axis=0) - nch
    tail_start = gstart + tot
    tail_n = nblk * CHUNKS_PER_BLOCK - tot
    nused = bend[-1:]
    blk = jnp.minimum(jnp.arange(nb, dtype=jnp.int32), nused - 1)
    blk_e = jnp.minimum(jnp.searchsorted(bend, blk, side='right'), ne - 1).astype(jnp.int32)
    i32 = lambda a: a.astype(jnp.int32)
    return i32(nch), i32(rs), i32(tail_start), i32(tail_n), i32(nused), blk_e


def _chunk_rows(c):
    return pl.ds(pl.multiple_of(c * CHUNK, CHUNK), CHUNK)


def _for_each_run_chunk(nch_ref, rs_ref, tile, ne, fn):
    def per_expert(e, c0):
        n = nch_ref[tile, e]
        g0 = rs_ref[tile, e]

        def body(j, carry):
            fn(c0 + j, g0 + j)
            return carry
        lax.fori_loop(0, n, body, 0)
        return c0 + n
    return lax.fori_loop(0, ne, per_expert, 0)


def _one_hot_rows(pos, nrows):
    r = lax.broadcasted_iota(jnp.int32, (nrows, pos.shape[1]), 0)
    p = jnp.zeros(r.shape, F32)
    for k in range(TOP_K):
        p = p + jnp.where(r == pos[k:k + 1], 1.0, 0.0)
    return p.astype(BF16)


def _dispatch_kernel(nch_ref, rs_ref, ts_ref, tn_ref, h_ref, posg_ref, xs_hbm,
                     buf, zbuf, sem, zsem, *, ne):
    i = pl.program_id(0)
    pos = posg_ref[0][0:TOP_K].astype(jnp.int32)
    buf[...] = jnp.dot(_one_hot_rows(pos, buf.shape[0]), h_ref[...],
                       preferred_element_type=F32).astype(BF16)

    def run_copy(c, g):
        return pltpu.make_async_copy(buf.at[_chunk_rows(c)], xs_hbm.at[_chunk_rows(g)], sem)

    def zero_copy(g):
        return pltpu.make_async_copy(zbuf, xs_hbm.at[_chunk_rows(g)], zsem)

    @pl.when(i == 0)
    def _():
        zbuf[...] = jnp.zeros(zbuf.shape, BF16)
        for phase in ("start", "wait"):
            def per_expert(e, carry):
                def body(j, c):
                    cp = zero_copy(ts_ref[e] + j)
                    cp.start() if phase == "start" else cp.wait()
                    return c
                return lax.fori_loop(0, tn_ref[e], body, carry)
            lax.fori_loop(0, ne, per_expert, 0)

    total = _for_each_run_chunk(nch_ref, rs_ref, i, ne, lambda c, g: run_copy(c, g).start())

    def wait_one(j, carry):
        run_copy(0, 0).wait()
        return carry
    lax.fori_loop(0, total, wait_one, 0)


def _dispatch(h, posg, nch, rs, tail_start, tail_n, *, nb):
    T, D = h.shape
    nt, _, tm = posg.shape
    ne = nch.shape[1]
    return pl.pallas_call(
        functools.partial(_dispatch_kernel, ne=ne),
        grid_spec=pltpu.PrefetchScalarGridSpec(
            num_scalar_prefetch=4, grid=(nt,),
            in_specs=[pl.BlockSpec((tm, D), lambda i, *_: (i, 0)),
                      pl.BlockSpec((1, 2 * TOP_K, tm), lambda i, *_: (i, 0, 0))],
            out_specs=pl.BlockSpec(memory_space=pl.ANY),
            scratch_shapes=[pltpu.VMEM((_slots(tm, ne), D), BF16),
                            pltpu.VMEM((CHUNK, D), BF16),
                            pltpu.SemaphoreType.DMA(()), pltpu.SemaphoreType.DMA(())]),
        out_shape=jax.ShapeDtypeStruct((nb * MOE_BLOCK, D), BF16),
        compiler_params=_cparams(("arbitrary",)),
        name="dispatch",
    )(nch, rs, tail_start, tail_n, h, posg)


def _experts_kernel(be_ref, nu_ref, xs_ref, wgu_ref, bgu_ref, wd_ref, bd_ref, ys_ref,
                    wgu_s, wd_s):
    b = pl.program_id(0)

    @pl.when(b < nu_ref[0])
    def _():
        @pl.when((b == 0) | (be_ref[b] != be_ref[jnp.maximum(b - 1, 0)]))
        def _():
            wgu_s[...] = wgu_ref[0].astype(BF16)
            wd_s[...] = wd_ref[0].astype(BF16)

        ff = wd_s.shape[0]
        gu = jnp.dot(xs_ref[...], wgu_s[...], preferred_element_type=F32) + bgu_ref[0]
        g = jnp.minimum(gu[:, :ff], SWIGLU_LIMIT)
        u = jnp.clip(gu[:, ff:], -SWIGLU_LIMIT, SWIGLU_LIMIT)
        act = (u + 1.0) * g * jax.nn.sigmoid(SWIGLU_ALPHA * g)
        ys = jnp.dot(act.astype(BF16), wd_s[...], preferred_element_type=F32) + bd_ref[0]
        ys_ref[...] = ys.astype(ys_ref.dtype)


def _experts(xs, blk_e, nused, w_gate_up, b_gate_up, w_down, b_down):
    rows, D = xs.shape
    nb = rows // MOE_BLOCK
    ne, _, ff2 = w_gate_up.shape
    ff = w_down.shape[1]
    blk = lambda b, be, nu: (jnp.minimum(b, nu[0] - 1), 0)
    exp3 = lambda b, be, nu: (be[b], 0, 0)
    return pl.pallas_call(
        _experts_kernel,
        grid_spec=pltpu.PrefetchScalarGridSpec(
            num_scalar_prefetch=2, grid=(nb,),
            in_specs=[pl.BlockSpec((MOE_BLOCK, D), blk),
                      pl.BlockSpec((1, D, ff2), exp3), pl.BlockSpec((1, 1, ff2), exp3),
                      pl.BlockSpec((1, ff, D), exp3), pl.BlockSpec((1, 1, D), exp3)],
            out_specs=pl.BlockSpec((MOE_BLOCK, D), blk),
            scratch_shapes=[pltpu.VMEM((D, ff2), BF16), pltpu.VMEM((ff, D), BF16)]),
        out_shape=jax.ShapeDtypeStruct((rows, D), BF16),
        compiler_params=_cparams(("arbitrary",)),
        name="experts",
    )(blk_e, nused, xs, w_gate_up, b_gate_up[:, None, :], w_down, b_down[:, None, :])


def _combine_kernel(nch_ref, rs_ref, ys_hbm, posg_ref, x1_ref, y_ref, buf, sem, *, ne):
    i = pl.program_id(0)
    tm = x1_ref.shape[0]
    nslot_chunks = buf.shape[0] // CHUNK

    def run_copy(c, g):
        return pltpu.make_async_copy(ys_hbm.at[_chunk_rows(g)], buf.at[_chunk_rows(c)], sem)

    total = _for_each_run_chunk(nch_ref, rs_ref, i, ne, lambda c, g: run_copy(c, g).start())

    def zero_chunk(c, carry):
        buf[_chunk_rows(c), :] = jnp.zeros((CHUNK, buf.shape[1]), BF16)
        return carry
    lax.fori_loop(total, nslot_chunks, zero_chunk, 0)

    r0 = lax.broadcasted_iota(jnp.int32, (tm, tm), 0)
    r1 = lax.broadcasted_iota(jnp.int32, (tm, tm), 1)
    posg_t = _dot_nt(jnp.where(r0 == r1, 1.0, 0.0), posg_ref[0], True)
    slot = lax.broadcasted_iota(jnp.int32, (tm, buf.shape[0]), 1)
    w = jnp.zeros(slot.shape, F32)
    for k in range(TOP_K):
        w = w + jnp.where(slot == posg_t[:, k:k + 1].astype(jnp.int32),
                          posg_t[:, TOP_K + k:TOP_K + k + 1], 0.0)

    def wait_one(j, carry):
        run_copy(0, 0).wait()
        return carry
    lax.fori_loop(0, total, wait_one, 0)
    y_ref[...] = x1_ref[...] + jnp.dot(w.astype(BF16), buf[...], preferred_element_type=F32)


def _combine(ys, posg, x1, nch, rs):
    T, D = x1.shape
    nt, _, tm = posg.shape
    ne = nch.shape[1]
    return pl.pallas_call(
        functools.partial(_combine_kernel, ne=ne),
        grid_spec=pltpu.PrefetchScalarGridSpec(
            num_scalar_prefetch=2, grid=(nt,),
            in_specs=[pl.BlockSpec(memory_space=pl.ANY),
                      pl.BlockSpec((1, 2 * TOP_K, tm), lambda i, *_: (i, 0, 0)),
                      pl.BlockSpec((tm, D), lambda i, *_: (i, 0))],
            out_specs=pl.BlockSpec((tm, D), lambda i, *_: (i, 0)),
            scratch_shapes=[pltpu.VMEM((_slots(tm, ne), D), BF16),
                            pltpu.SemaphoreType.DMA(())]),
        out_shape=jax.ShapeDtypeStruct((T, D), F32),
        compiler_params=_cparams(("arbitrary",)),
        name="combine",
    )(nch, rs, ys, posg, x1)


DECODE_PAGES_PER_STEP = 16
NEW_ROWS = 8
SOFTMAX_ROWS = 1024


def _decode_kernel(pt_ref, lam_ref, sw_ref, qx_ref, kn_ref, vn_ref, eh_ref, *rest,
                   pps, page, heads, lam_init):
    k_refs = rest[:pps]
    v_refs = rest[pps:2 * pps]
    o_ref, s_ref, m_ref, coef_ref, acc_ref = rest[2 * pps:]
    s = pl.program_id(1)
    half_steps = pl.num_programs(1) // 2
    ncol = s_ref.shape[1]
    past = half_steps * pps * page
    qx = qx_ref[0]

    def scores(kblock):
        return jnp.dot(kblock.astype(BF16), qx, preferred_element_type=F32)

    @pl.when(s == 0)
    def _():
        m_ref[...] = jnp.full(m_ref.shape, NEG, F32)

    @pl.when(s < half_steps)
    def _():
        m = m_ref[0:1, :]
        for j in range(pps):
            row0 = pl.multiple_of((s * pps + j) * page, page)
            sc = scores(k_refs[j][0])
            s_ref[pl.ds(row0, page), :] = sc
            m = jnp.maximum(m, jnp.max(sc, axis=0, keepdims=True))
        m_ref[...] = jnp.broadcast_to(m, m_ref.shape)

    def weighted(pblock, vblock):
        a = pblock * coef_ref[0:1, :]
        a = a + pltpu.roll(a, ncol - heads, 1)
        pb = jnp.dot(a.astype(BF16), eh_ref[...], preferred_element_type=F32)
        prod = pb * _bf16_round(vblock)
        return jnp.sum(prod.reshape(-1, 8, prod.shape[1]), axis=0)

    @pl.when(s == half_steps)
    def _():
        rown = lax.broadcasted_iota(jnp.int32, (NEW_ROWS, ncol), 0)
        s_new = jnp.where(rown == 0,
                          scores(jnp.broadcast_to(kn_ref[0], (NEW_ROWS, kn_ref.shape[2]))), NEG)
        s_ref[pl.ds(past, NEW_ROWS), :] = s_new
        m = jnp.maximum(m_ref[0:1, :], jnp.max(s_new, axis=0, keepdims=True))

        def exp_rows(start, size, l):
            p = jnp.exp(s_ref[pl.ds(start, size), :] - m)
            s_ref[pl.ds(start, size), :] = p
            return l + jnp.sum(p, axis=0, keepdims=True)

        l = lax.fori_loop(
            0, past // SOFTMAX_ROWS,
            lambda i, l: exp_rows(pl.multiple_of(i * SOFTMAX_ROWS, SOFTMAX_ROWS), SOFTMAX_ROWS, l),
            jnp.zeros((1, ncol), F32))
        l = exp_rows(past, NEW_ROWS, l)
        lam = _lambda_value(lam_ref[...], lam_init)
        col = lax.broadcasted_iota(jnp.int32, (1, ncol), 1)
        coef_ref[...] = jnp.broadcast_to(jnp.where(col < heads, 1.0, -lam) / l, coef_ref.shape)
        acc_ref[...] = weighted(s_ref[pl.ds(past, NEW_ROWS), :],
                                jnp.broadcast_to(vn_ref[0], (NEW_ROWS, vn_ref.shape[2])))

    @pl.when(s >= half_steps)
    def _():
        acc = acc_ref[...]
        for j in range(pps):
            row0 = pl.multiple_of(((s - half_steps) * pps + j) * page, page)
            acc = acc + weighted(s_ref[pl.ds(row0, page), :], v_refs[j][0])
        acc_ref[...] = acc

    @pl.when(s == 2 * half_steps - 1)
    def _():
        o = jnp.sum(acc_ref[...], axis=0, keepdims=True)
        outs = []
        for h in range(o.shape[1] // V7X_LANES):
            outs.append(_subln(o[:, h * V7X_LANES:(h + 1) * V7X_LANES], sw_ref[...], lam_init))
        o_ref[0] = jnp.concatenate(outs, axis=1)


def _attn_decode(q, k_new, v_new, cache_k, cache_v, page_table, lamv, subln_w, *, lam_init):
    B, D = q.shape
    n_pages = page_table.shape[1]
    page = cache_k.shape[1]
    pps = DECODE_PAGES_PER_STEP
    half_steps = n_pages // pps
    heads = D // (2 * QK_GROUP)
    assert n_pages % pps == 0 and (n_pages * page) % SOFTMAX_ROWS == 0
    ncol = V7X_LANES
    cols = jnp.arange(ncol)
    group = jnp.arange(D) // QK_GROUP
    colmask = ((group % 2) * heads + group // 2)[:, None] == cols[None, :]
    qx = jnp.where(colmask[None], q[:, :, None], jnp.zeros((), BF16))
    eh = (cols[:, None] == (jnp.arange(D) // V7X_LANES)[None, :]).astype(BF16)

    def k_map(j):
        return lambda b, s, pt: (pt[b, jnp.minimum(s, half_steps - 1) * pps + j], 0, 0)

    def v_map(j):
        return lambda b, s, pt: (pt[b, jnp.maximum(s - half_steps, 0) * pps + j], 0, 0)

    vec3 = lambda b, s, pt: (b, 0, 0)
    const = lambda b, s, pt: (0, 0)
    out = pl.pallas_call(
        functools.partial(_decode_kernel, pps=pps, page=page, heads=heads, lam_init=lam_init),
        grid_spec=pltpu.PrefetchScalarGridSpec(
            num_scalar_prefetch=1, grid=(B, 2 * half_steps),
            in_specs=[pl.BlockSpec(lamv.shape, const), pl.BlockSpec((1, V7X_LANES), const),
                      pl.BlockSpec((1, D, ncol), vec3),
                      pl.BlockSpec((1, 1, D), vec3), pl.BlockSpec((1, 1, D), vec3),
                      pl.BlockSpec((ncol, D), const)]
                     + [pl.BlockSpec((1, page, D), k_map(j)) for j in range(pps)]
                     + [pl.BlockSpec((1, page, D), v_map(j)) for j in range(pps)],
            out_specs=pl.BlockSpec((1, 1, D), vec3),
            scratch_shapes=[pltpu.VMEM((n_pages * page + NEW_ROWS, ncol), F32),
                            pltpu.VMEM((8, ncol), F32), pltpu.VMEM((8, ncol), F32),
                            pltpu.VMEM((8, D), F32)]),
        out_shape=jax.ShapeDtypeStruct((B, 1, D), F32),
        compiler_params=_cparams(("arbitrary", "arbitrary")),
        name="decode",
    )(page_table, lamv, subln_w[None, :], qx, k_new[:, None, :], v_new[:, None, :], eh,
      *([cache_k] * pps), *([cache_v] * pps))
    return out[:, 0, :]


def _moe(h, posg, cnt, x1, w_gate_up, b_gate_up, w_down, b_down):
    nt, _, tm = posg.shape
    ne = cnt.shape[1]
    max_chunks = (nt * tm * TOP_K) // CHUNK + nt * ne
    nb = -(-max_chunks // CHUNKS_PER_BLOCK) + ne
    nch, rs, tail_start, tail_n, nused, blk_e = _moe_tables(cnt, nb)
    xs = _dispatch(h, posg, nch, rs, tail_start, tail_n, nb=nb)
    ys = _experts(xs, blk_e, nused, w_gate_up, b_gate_up, w_down, b_down)
    return _combine(ys, posg, x1, nch, rs)


PROJ_TILE = 512


def kernel(x_prompt, x_sample, cache_k, cache_v, state_conv, page_table, norm1_w, w_in,
           q_norm_w, k_norm_w, lambda_q1, lambda_k1, lambda_q2, lambda_k2, subln_w,
           conv_w, conv_b, conv_ln_g, conv_ln_b, w_out, norm2_w, router_w, router_b,
           w_gate_up, b_gate_up, w_down, b_down):
    B, S, D = x_prompt.shape
    Bs, Ss, _ = x_sample.shape
    depth = norm1_w.shape[0]
    n_phys, page, heads, _, qk = cache_k.shape[1:]
    vdim = cache_v.shape[-1]
    qc, vc, cc = heads * 2 * qk, heads * vdim, conv_w.shape[2]
    taps = conv_w.shape[1]
    assert Ss == 1 and qk == QK_GROUP and vdim == V7X_LANES and (B * S) % MOE_TILE == 0
    assert Bs * Ss <= MOE_TILE and S >= taps - 1
    n_past = page_table.shape[1] * page
    T = B * S
    pos_p = jnp.arange(S, dtype=F32)
    pos_s = jnp.full((Bs,), n_past, F32)
    xp = x_prompt.reshape(T, D)
    xs = x_sample.reshape(Bs, D)
    pad_rows = MOE_TILE - Bs
    outs = [[] for _ in range(6)]
    for l in range(depth):
        lam_init = 0.8 - 0.6 * math.exp(-0.3 * l)
        lamv = jnp.stack([lambda_q1[l], lambda_k1[l], lambda_q2[l], lambda_k2[l]])
        conv_p = (conv_w[l], conv_b[l], conv_ln_g[l], conv_ln_b[l])
        proj_p = (norm1_w[l], w_in[l], q_norm_w[l], k_norm_w[l])
        tail_p = (w_out[l], norm2_w[l], router_w[l], router_b[l])

        q, k, v, u, kb, vb = _proj(xp, pos_p, S // PROJ_TILE, PROJ_TILE, *proj_p,
                                   qc=qc, vc=vc, cc=cc, exact_norm=False, bf16_kv=True)
        attn = _attn_prompt(q, kb, vb, lamv, subln_w[l], batch=B, seq=S, heads=heads,
                            lam_init=lam_init)
        conv = _conv_prompt(u, *conv_p, batch=B, seq=S)
        x1, h, posg, cnt = _tail(attn, conv, xp, *tail_p, tm=MOE_TILE)
        outs[0].append(k.reshape(B, S, heads, 2, qk))
        outs[1].append(v.reshape(B, S, heads, vdim))
        outs[2].append(u.reshape(B, S, cc)[:, S - (taps - 1):])

        qs, ks_, vs_, us = _proj(xs, pos_s, 1, Bs, *proj_p, qc=qc, vc=vc, cc=cc,
                                 exact_norm=True, bf16_kv=False)
        attn_s = _attn_decode(qs, ks_, vs_, cache_k[l].reshape(n_phys, page, qc),
                              cache_v[l].reshape(n_phys, page, vc), page_table, lamv,
                              subln_w[l], lam_init=lam_init)
        conv_s = _conv_step(state_conv[l], us, *conv_p)
        x1s, hs, posg_s, cnt_s = _tail(attn_s, conv_s, xs, *tail_p, tm=Bs)
        outs[3].append(ks_.reshape(Bs, Ss, heads, 2, qk))
        outs[4].append(vs_.reshape(Bs, Ss, heads, vdim))
        outs[5].append(jnp.concatenate([state_conv[l][:, Ss:], us[:, None, :]], axis=1))

        unused = jnp.concatenate([jnp.full((1, TOP_K, pad_rows), -1.0, F32),
                                  jnp.zeros((1, TOP_K, pad_rows), F32)], axis=1)
        y = _moe(jnp.concatenate([h, jnp.pad(hs, ((0, pad_rows), (0, 0)))]),
                 jnp.concatenate([posg, jnp.concatenate([posg_s, unused], axis=2)]),
                 jnp.concatenate([cnt[:, :, 0], cnt_s[:, :, 0]]),
                 jnp.concatenate([x1, jnp.pad(x1s, ((0, pad_rows), (0, 0)))]),
                 w_gate_up[l], b_gate_up[l], w_down[l], b_down[l])
        xp = y[:T]
        xs = y[T:T + Bs]
    return (xp.reshape(B, S, D), xs.reshape(Bs, Ss, D)) + tuple(jnp.stack(o) for o in outs)
```

```python
import functools
import math

import jax
import jax.numpy as jnp
from jax import lax
from jax.experimental import pallas as pl
from jax.experimental.pallas import tpu as pltpu

F32 = jnp.float32
BF16 = jnp.bfloat16
I32 = jnp.int32
HIGHEST = lax.Precision.HIGHEST

EPS = 1e-6
ROPE_THETA = 10000.0
SWIGLU_LIMIT = 7.0
SWIGLU_ALPHA = 1.702
TOP_K = 4
NEG = -1e30
QK_GROUP = 64

V7X_LANES = 128
V7X_SUBLANES = 8
VMEM_LIMIT = 56 * 1024 * 1024
BF16_ROWS = 16

MOE_TILE = 256
CHUNK = BF16_ROWS
MOE_BLOCK = 512
CHUNKS_PER_BLOCK = MOE_BLOCK // CHUNK
PROJ_TILE = 512


def _cparams(sem, vmem=VMEM_LIMIT):
    return pltpu.CompilerParams(dimension_semantics=sem, vmem_limit_bytes=vmem)


def _dot(a, b, exact=False):
    if exact:
        return jnp.dot(a.astype(F32), b.astype(F32), precision=HIGHEST,
                       preferred_element_type=F32)
    return jnp.dot(a.astype(BF16), b.astype(BF16), preferred_element_type=F32)


def _dot_nt(a, b, exact=False):
    dn = (((1,), (1,)), ((), ()))
    if exact:
        return lax.dot_general(a.astype(F32), b.astype(F32), dn, precision=HIGHEST,
                               preferred_element_type=F32)
    return lax.dot_general(a.astype(BF16), b.astype(BF16), dn, preferred_element_type=F32)


def _bf16_round(x):
    return x.astype(BF16).astype(F32)


def _rope_norm(p, gsum, w, cos, sin, first_half, exact_norm):
    ss = _dot(p * p, gsum, exact_norm)
    n = p * lax.rsqrt(ss * (1.0 / QK_GROUP) + EPS) * w
    outs = []
    for j in range(p.shape[1] // V7X_LANES):
        nj = n[:, j * V7X_LANES:(j + 1) * V7X_LANES]
        rot = jnp.where(first_half, pltpu.roll(nj, V7X_LANES - QK_GROUP // 2, 1),
                        pltpu.roll(nj, QK_GROUP // 2, 1))
        outs.append(nj * cos + rot * sin)
    return jnp.concatenate(outs, axis=1)


def _proj_kernel(x_ref, n1_ref, w_ref, qw_ref, kw_ref, cos_ref, sin_ref, gsum_ref,
                 q_ref, k_ref, v_ref, u_ref, *rest, qc, vc, cc, scale, exact_norm):
    x = x_ref[...]
    h = x * lax.rsqrt(jnp.mean(x * x, axis=-1, keepdims=True) + EPS) * n1_ref[...]
    hm = h.astype(BF16)
    cos = cos_ref[...]
    sin = sin_ref[...]
    lane = lax.broadcasted_iota(I32, cos.shape, 1)
    first_half = (lane % QK_GROUP) < QK_GROUP // 2
    gsum = gsum_ref[...]

    q = _rope_norm(_dot(hm, w_ref[:, 0:qc]), gsum, qw_ref[...], cos, sin, first_half, exact_norm)
    q_ref[...] = (q * scale).astype(q_ref.dtype)
    k = _rope_norm(_dot(hm, w_ref[:, qc:2 * qc]), gsum, kw_ref[...], cos, sin, first_half,
                   exact_norm)
    v = _dot(hm, w_ref[:, 2 * qc:2 * qc + vc])
    v_ref[...] = v
    o = 2 * qc + vc
    ua = _dot(hm, w_ref[:, o:o + cc])
    ub = _dot(hm, w_ref[:, o + cc:o + 2 * cc])
    u_ref[...] = ua * jax.nn.sigmoid(ub)
    if rest:
        kb_ref, vb_ref = rest
        kt = k.T
        k_ref[0] = kt
        kb_ref[0, :, 0] = kt.astype(BF16).reshape(kb_ref.shape[1], kb_ref.shape[3], kt.shape[1])
        vb_ref[...] = v.astype(BF16)
    else:
        k_ref[...] = k


def _rope_tables(pos):
    half = QK_GROUP // 2
    inv = jnp.power(ROPE_THETA, -jnp.arange(half, dtype=F32) / half)
    ang = pos[:, None] * inv[None, :]
    reps = V7X_LANES // QK_GROUP
    cos = jnp.tile(jnp.cos(ang), (1, 2 * reps))
    s = jnp.sin(ang)
    sin = jnp.tile(jnp.concatenate([-s, s], axis=1), (1, reps))
    return cos, sin


def _proj(x2d, pos_rows, n_pos_blocks, tm, norm1_w, w_in, q_norm_w, k_norm_w, *, qc, vc, cc,
          exact_norm, attn_layout):
    T, D = x2d.shape
    cos, sin = _rope_tables(pos_rows)
    gi = jnp.arange(qc) // QK_GROUP
    gsum = (gi[:, None] == gi[None, :]).astype(F32 if exact_norm else BF16)
    qw = jnp.tile(q_norm_w, qc // QK_GROUP)[None, :]
    kw = jnp.tile(k_norm_w, qc // QK_GROUP)[None, :]
    w = w_in.astype(BF16)
    row = lambda i: (i, 0)
    full = lambda i: (0, 0)
    out_shape = [jax.ShapeDtypeStruct((T, qc), BF16),
                 jax.ShapeDtypeStruct((T, qc), F32),
                 jax.ShapeDtypeStruct((T, vc), F32),
                 jax.ShapeDtypeStruct((T, cc), F32)]
    out_specs = [pl.BlockSpec((tm, qc), row), pl.BlockSpec((tm, qc), row),
                 pl.BlockSpec((tm, vc), row), pl.BlockSpec((tm, cc), row)]
    if attn_layout:
        nseq = T // (n_pos_blocks * tm)
        heads = qc // (2 * QK_GROUP)
        seq_tile = lambda i: (i // n_pos_blocks, 0, i % n_pos_blocks)
        out_shape[1] = jax.ShapeDtypeStruct((nseq, qc, n_pos_blocks * tm), F32)
        out_specs[1] = pl.BlockSpec((1, qc, tm), seq_tile)
        out_shape += [jax.ShapeDtypeStruct((nseq, heads, n_pos_blocks, 2 * QK_GROUP, tm), BF16),
                      jax.ShapeDtypeStruct((T, vc), BF16)]
        out_specs += [pl.BlockSpec((1, heads, 1, 2 * QK_GROUP, tm),
                                   lambda i: (i // n_pos_blocks, 0, i % n_pos_blocks, 0, 0)),
                      pl.BlockSpec((tm, vc), row)]
    return pl.pallas_call(
        functools.partial(_proj_kernel, qc=qc, vc=vc, cc=cc, scale=QK_GROUP ** -0.5,
                          exact_norm=exact_norm),
        grid=(T // tm,),
        in_specs=[pl.BlockSpec((tm, D), row),
                  pl.BlockSpec((1, D), full),
                  pl.BlockSpec(w.shape, full),
                  pl.BlockSpec((1, qc), full),
                  pl.BlockSpec((1, qc), full),
                  pl.BlockSpec((tm, V7X_LANES), lambda i: (i % n_pos_blocks, 0)),
                  pl.BlockSpec((tm, V7X_LANES), lambda i: (i % n_pos_blocks, 0)),
                  pl.BlockSpec((qc, qc), full)],
        out_specs=out_specs,
        out_shape=out_shape,
        compiler_params=_cparams(("arbitrary",)),
        name="proj" if attn_layout else "proj_step",
    )(x2d, norm1_w[None, :], w, qw, kw, cos, sin, gsum)


def _lambda_value(lv, lam_init):
    a = jnp.sum(lv[0:1] * lv[1:2], axis=-1, keepdims=True)
    b = jnp.sum(lv[2:3] * lv[3:4], axis=-1, keepdims=True)
    return jnp.exp(a) - jnp.exp(b) + lam_init


def _subln(o, w, lam_init):
    y = o * lax.rsqrt(jnp.mean(o * o, axis=-1, keepdims=True) + EPS)
    return y * w * (1.0 - lam_init)


def _attn_kernel(lam_ref, sw_ref, q_ref, k_ref, v_ref, o_ref, *, tq, lam_init):
    i = pl.program_id(2)
    lam = _lambda_value(lam_ref[...], lam_init)
    q = q_ref[...]
    lane = lax.broadcasted_iota(I32, q.shape, 1)
    zero = jnp.zeros_like(q)
    qs = (jnp.where(lane < QK_GROUP, q, zero), jnp.where(lane >= QK_GROUP, q, zero))

    def chunk(j, carry, masked):
        kc = k_ref[0, 0, j]
        vc = v_ref[pl.ds(pl.multiple_of(j * tq, tq), tq), :]
        out = []
        for c in range(2):
            m, l, acc = carry[c]
            s = jnp.dot(qs[c], kc, preferred_element_type=F32)
            if masked:
                row = lax.broadcasted_iota(I32, s.shape, 0)
                col = lax.broadcasted_iota(I32, s.shape, 1)
                s = jnp.where(col <= row, s, NEG)
            m_new = jnp.maximum(m, jnp.max(s, axis=-1, keepdims=True))
            p = jnp.exp(s - m_new)
            alpha = jnp.exp(m - m_new)
            l = alpha * l + jnp.sum(p, axis=-1, keepdims=True)
            acc = alpha * acc + _dot(p, vc)
            out.append((m_new, l, acc))
        return tuple(out)

    init = tuple((jnp.full((tq, 1), NEG, F32), jnp.zeros((tq, 1), F32),
                  jnp.zeros((tq, V7X_LANES), F32)) for _ in range(2))
    carry = lax.fori_loop(0, i, lambda j, c: chunk(j, c, False), init)
    (_, l0, a0), (_, l1, a1) = chunk(i, carry, True)
    o = a0 / l0 - lam * (a1 / l1)
    o_ref[...] = _subln(o, sw_ref[...], lam_init).astype(o_ref.dtype)


def _attn_prompt(q, kbt, vb, lamv, subln_w, *, lam_init):
    batch, heads, nq, _, tq = kbt.shape
    seq = nq * tq
    return pl.pallas_call(
        functools.partial(_attn_kernel, tq=tq, lam_init=lam_init),
        grid=(batch, heads, nq),
        in_specs=[pl.BlockSpec(lamv.shape, lambda b, h, i: (0, 0)),
                  pl.BlockSpec((1, V7X_LANES), lambda b, h, i: (0, 0)),
                  pl.BlockSpec((tq, V7X_LANES), lambda b, h, i: (b * nq + i, h)),
                  pl.BlockSpec((1, 1, nq, V7X_LANES, tq), lambda b, h, i: (b, h, 0, 0, 0)),
                  pl.BlockSpec((seq, V7X_LANES), lambda b, h, i: (b, h))],
        out_specs=pl.BlockSpec((tq, V7X_LANES), lambda b, h, i: (b * nq + i, h)),
        out_shape=jax.ShapeDtypeStruct((batch * seq, heads * V7X_LANES), BF16),
        compiler_params=_cparams(("arbitrary", "arbitrary", "arbitrary")),
        name="attn",
    )(lamv, subln_w[None, :], q, kbt, vb)


DECODE_PAGES_PER_STEP = 16
SOFTMAX_PAGES = 16


def _decode_kernel(pt_ref, lam_ref, sw_ref, qm_ref, kn_ref, vn_ref, *rest,
                   pps, page, heads, lam_init):
    k_refs = rest[:pps]
    v_refs = rest[pps:2 * pps]
    o_ref, s_ref, m_ref, coef_ref, acc_ref = rest[2 * pps:]
    s = pl.program_id(1)
    half_steps = pl.num_programs(1) // 2
    rows = s_ref.shape[1]
    qm = qm_ref[0]
    row = lax.broadcasted_iota(I32, (rows, V7X_LANES), 0)
    n_pages = half_steps * pps

    @pl.when(s == 0)
    def _():
        m_ref[...] = jnp.full(m_ref.shape, NEG, F32)

    @pl.when(s < half_steps)
    def _():
        m = m_ref[...]
        for j in range(pps):
            sc = jnp.dot(qm, k_refs[j][0].astype(BF16), preferred_element_type=F32)
            s_ref[s * pps + j] = sc
            m = jnp.maximum(m, sc)
        m_ref[...] = m

    def head_weights(p):
        a = p * coef_ref[...]
        return _bf16_round(a + pltpu.roll(a, rows - heads, 0))

    @pl.when(s == half_steps)
    def _():
        s_new = jnp.sum(qm.astype(F32) * _bf16_round(kn_ref[0]), axis=-1, keepdims=True)
        m = jnp.maximum(jnp.max(m_ref[...], axis=-1, keepdims=True), s_new)

        def exp_pages(t, l):
            pages = pl.ds(pl.multiple_of(t * SOFTMAX_PAGES, SOFTMAX_PAGES), SOFTMAX_PAGES)
            p = jnp.exp(s_ref[pages] - m[None])
            s_ref[pages] = p
            return l + jnp.sum(p, axis=0)

        lsum = lax.fori_loop(0, n_pages // SOFTMAX_PAGES, exp_pages,
                             jnp.zeros((rows, V7X_LANES), F32))
        p_new = jnp.exp(s_new - m)
        l = jnp.sum(lsum, axis=-1, keepdims=True) + p_new
        lam = _lambda_value(lam_ref[...], lam_init)
        coef = jnp.where(row[:, 0:1] < heads, 1.0, -lam) / l
        coef_ref[...] = jnp.broadcast_to(coef, coef_ref.shape)
        acc_ref[...] = (head_weights(jnp.broadcast_to(p_new, (rows, V7X_LANES)))
                        * _bf16_round(vn_ref[0]))

    @pl.when(s >= half_steps)
    def _():
        lane = lax.broadcasted_iota(I32, (rows, V7X_LANES), 1)
        keep = (lane % heads == row) & (row < heads)
        acc = acc_ref[...]
        for j in range(pps):
            a = head_weights(s_ref[(s - half_steps) * pps + j])
            parts = []
            for c in range(heads):
                idx = (c * page + lane) // heads
                parts.append(jnp.where(keep, jnp.take_along_axis(a, idx, axis=1), 0.0))
            a_exp = jnp.concatenate(parts, axis=1).astype(BF16)
            acc = acc + jnp.dot(a_exp, v_refs[j][0].astype(BF16), preferred_element_type=F32)
        acc_ref[...] = acc

    @pl.when(s == 2 * half_steps - 1)
    def _():
        o_ref[0] = _subln(acc_ref[...], sw_ref[...], lam_init)


def _attn_decode(q, k_new, v_new, cache_kt, cache_vr, page_table, lamv, subln_w, *, heads,
                 lam_init):
    B, D = q.shape
    n_pages = page_table.shape[1]
    page = cache_kt.shape[2]
    vdim = cache_vr.shape[2]
    pps = DECODE_PAGES_PER_STEP
    half_steps = n_pages // pps
    rows = 2 * heads
    assert rows == V7X_SUBLANES and page == V7X_LANES and vdim == V7X_LANES
    assert n_pages % pps == 0 and n_pages % SOFTMAX_PAGES == 0
    group = jnp.arange(D) // QK_GROUP
    rowmask = ((group % 2) * heads + group // 2)[None, :] == jnp.arange(rows)[:, None]
    qm = jnp.where(rowmask[None], q[:, None, :], jnp.zeros((), BF16))
    vn = jnp.pad(v_new.reshape(B, heads, vdim), ((0, 0), (0, rows - heads), (0, 0)))

    def k_map(j):
        return lambda b, s, pt: (pt[b, jnp.minimum(s, half_steps - 1) * pps + j], 0, 0)

    def v_map(j):
        return lambda b, s, pt: (pt[b, jnp.maximum(s - half_steps, 0) * pps + j], 0, 0)

    vec3 = lambda b, s, pt: (b, 0, 0)
    const = lambda b, s, pt: (0, 0)
    out = pl.pallas_call(
        functools.partial(_decode_kernel, pps=pps, page=page, heads=heads, lam_init=lam_init),
        grid_spec=pltpu.PrefetchScalarGridSpec(
            num_scalar_prefetch=1, grid=(B, 2 * half_steps),
            in_specs=[pl.BlockSpec(lamv.shape, const), pl.BlockSpec((1, vdim), const),
                      pl.BlockSpec((1, rows, D), vec3), pl.BlockSpec((1, 1, D), vec3),
                      pl.BlockSpec((1, rows, vdim), vec3)]
                     + [pl.BlockSpec((1, D, page), k_map(j)) for j in range(pps)]
                     + [pl.BlockSpec((1, page * heads, vdim), v_map(j)) for j in range(pps)],
            out_specs=pl.BlockSpec((1, rows, vdim), vec3),
            scratch_shapes=[pltpu.VMEM((n_pages, rows, page), F32),
                            pltpu.VMEM((rows, V7X_LANES), F32),
                            pltpu.VMEM((rows, V7X_LANES), F32),
                            pltpu.VMEM((rows, vdim), F32)]),
        out_shape=jax.ShapeDtypeStruct((B, rows, vdim), F32),
        compiler_params=_cparams(("arbitrary", "arbitrary")),
        name="decode",
    )(page_table, lamv, subln_w[None, :], qm, k_new[:, None, :], vn,
      *([cache_kt] * pps), *([cache_vr] * pps))
    return out[:, :heads, :].reshape(B, heads * vdim)


CONV_HALO = 32


def _ln_swish(y, b_ref, g_ref, be_ref):
    y = y + b_ref[...]
    mu = jnp.mean(y, axis=-1, keepdims=True)
    yc = y - mu
    z = yc * lax.rsqrt(jnp.mean(yc * yc, axis=-1, keepdims=True) + EPS) * g_ref[...] + be_ref[...]
    return z * jax.nn.sigmoid(z)


def _conv_kernel(u_ref, w_ref, b_ref, g_ref, be_ref, o_ref, buf_ref, *, tc, taps):
    @pl.when(pl.program_id(1) == 0)
    def _():
        buf_ref[0:CONV_HALO, :] = jnp.zeros((CONV_HALO, buf_ref.shape[1]), F32)

    buf_ref[CONV_HALO:CONV_HALO + tc, :] = _bf16_round(u_ref[...])
    acc = jnp.zeros((tc, buf_ref.shape[1]), F32)
    for k in range(taps):
        acc = acc + (buf_ref[pl.ds(CONV_HALO - (taps - 1) + k, tc), :]
                     * _bf16_round(w_ref[k:k + 1, :]))
    o_ref[...] = _ln_swish(acc, b_ref, g_ref, be_ref).astype(o_ref.dtype)
    buf_ref[0:CONV_HALO, :] = buf_ref[tc:tc + CONV_HALO, :]


def _conv_prompt(u2d, conv_w, conv_b, ln_g, ln_b, *, batch, seq, tc=512):
    taps, C = conv_w.shape
    nt = seq // tc
    vec = lambda b, i: (0, 0)
    return pl.pallas_call(
        functools.partial(_conv_kernel, tc=tc, taps=taps),
        grid=(batch, nt),
        in_specs=[pl.BlockSpec((tc, C), lambda b, i: (b * nt + i, 0)),
                  pl.BlockSpec((taps, C), vec), pl.BlockSpec((1, C), vec),
                  pl.BlockSpec((1, C), vec), pl.BlockSpec((1, C), vec)],
        out_specs=pl.BlockSpec((tc, C), lambda b, i: (b * nt + i, 0)),
        out_shape=jax.ShapeDtypeStruct((batch * seq, C), BF16),
        scratch_shapes=[pltpu.VMEM((tc + CONV_HALO, C), F32)],
        compiler_params=_cparams(("arbitrary", "arbitrary")),
        name="conv",
    )(u2d, conv_w, conv_b[None, :], ln_g[None, :], ln_b[None, :])


def _conv_step_kernel(st_ref, u_ref, w_ref, b_ref, g_ref, be_ref, o_ref, *, taps):
    acc = _bf16_round(u_ref[...]) * _bf16_round(w_ref[taps - 1:taps, :])
    for k in range(taps - 1):
        acc = acc + _bf16_round(st_ref[:, k, :]) * _bf16_round(w_ref[k:k + 1, :])
    o_ref[...] = _ln_swish(acc, b_ref, g_ref, be_ref)


def _conv_step(state, u, conv_w, conv_b, ln_g, ln_b):
    taps, C = conv_w.shape
    return pl.pallas_call(
        functools.partial(_conv_step_kernel, taps=taps),
        out_shape=jax.ShapeDtypeStruct(u.shape, F32),
        compiler_params=_cparams(None),
        name="conv_step",
    )(state, u, conv_w, conv_b[None, :], ln_g[None, :], ln_b[None, :])


def _tail_kernel(a_ref, c_ref, x_ref, wo_ref, n2_ref, rw_ref, rb_ref, *rest, n_valid):
    x1_ref, h_ref, posg_ref, cnt_ref = rest[-4:]
    tm = x_ref.shape[0]
    half = a_ref.shape[1]
    x1 = (x_ref[...] + _dot(a_ref[...], wo_ref[0:half, :])
          + _dot(c_ref[...], wo_ref[half:, :]))
    x1_ref[...] = x1
    h = (x1 * lax.rsqrt(jnp.mean(x1 * x1, axis=-1, keepdims=True) + EPS)
         * n2_ref[...]).astype(BF16)
    h_ref[...] = h
    logits = _dot_nt(rw_ref[...], h) + rb_ref[...]
    ne = logits.shape[0]
    eidx = lax.broadcasted_iota(I32, logits.shape, 0)
    valid = lax.broadcasted_iota(I32, (1, tm), 1) < n_valid

    sels, vals = [], []
    l = logits
    for _ in range(TOP_K):
        m = jnp.max(l, axis=0, keepdims=True)
        first = jnp.min(jnp.where(l == m, eidx, ne), axis=0, keepdims=True)
        sel = (eidx == first) & valid
        l = jnp.where(eidx == first, -jnp.inf, l)
        sels.append(sel)
        vals.append(m)
    ex = [jnp.exp(v - vals[0]) for v in vals]
    den = ex[0] + ex[1] + ex[2] + ex[3]
    gates = [jnp.where(valid, e / den, 0.0) for e in ex]

    msel = jnp.zeros(logits.shape, F32)
    for sel in sels:
        msel = msel + jnp.where(sel, 1.0, 0.0)
    r0 = lax.broadcasted_iota(I32, (tm, tm), 0)
    r1 = lax.broadcasted_iota(I32, (tm, tm), 1)
    upper = jnp.where(r0 < r1, 1.0, 0.0).astype(BF16)
    rank = jnp.dot(msel.astype(BF16), upper, preferred_element_type=F32)
    cnt = jnp.sum(msel, axis=1, keepdims=True)
    pcnt = jnp.ceil(cnt * (1.0 / CHUNK)) * CHUNK
    e0 = lax.broadcasted_iota(I32, (ne, ne), 0)
    e1 = lax.broadcasted_iota(I32, (ne, ne), 1)
    lower = jnp.where(e1 < e0, 1.0, 0.0)
    off = jnp.dot(lower.astype(BF16), jnp.broadcast_to(pcnt, (ne, V7X_LANES)).astype(BF16),
                  preferred_element_type=F32)[:, 0:1]
    pos = off + rank
    rows = [jnp.where(valid, jnp.sum(jnp.where(sel, pos, 0.0), axis=0, keepdims=True), -1.0)
            for sel in sels]
    posg_ref[0] = jnp.concatenate(rows + gates, axis=0)
    cnt_ref[0] = jnp.broadcast_to(cnt, (ne, V7X_LANES)).astype(I32)


def _tail(attn, conv, x2d, w_out, norm2_w, router_w, router_b, *, tm, n_valid, total_tiles,
          first_tile=0, into=None):
    T, D = x2d.shape
    half = attn.shape[1]
    ne = router_w.shape[1]
    nt = T // tm
    wo = w_out.astype(BF16)
    rw = router_w.T.astype(BF16)
    row = lambda i: (i, 0)
    full = lambda i: (0, 0)
    orow = lambda i: (first_tile + i, 0)
    otile = lambda i: (first_tile + i, 0, 0)
    extra = list(into) if into is not None else []
    return pl.pallas_call(
        functools.partial(_tail_kernel, n_valid=n_valid),
        grid=(nt,),
        in_specs=[pl.BlockSpec((tm, half), row), pl.BlockSpec((tm, half), row),
                  pl.BlockSpec((tm, D), row), pl.BlockSpec((D, D), full),
                  pl.BlockSpec((1, D), full), pl.BlockSpec((ne, D), full),
                  pl.BlockSpec((ne, 1), full)]
                 + [pl.BlockSpec(memory_space=pl.ANY)] * len(extra),
        out_specs=[pl.BlockSpec((tm, D), orow), pl.BlockSpec((tm, D), orow),
                   pl.BlockSpec((1, 2 * TOP_K, tm), otile),
                   pl.BlockSpec((1, ne, V7X_LANES), otile)],
        out_shape=[jax.ShapeDtypeStruct((total_tiles * tm, D), F32),
                   jax.ShapeDtypeStruct((total_tiles * tm, D), BF16),
                   jax.ShapeDtypeStruct((total_tiles, 2 * TOP_K, tm), F32),
                   jax.ShapeDtypeStruct((total_tiles, ne, V7X_LANES), I32)],
        input_output_aliases={7 + j: j for j in range(len(extra))},
        compiler_params=_cparams(("arbitrary",)),
        name="tail",
    )(attn, conv, x2d, wo, norm2_w[None, :], rw, router_b[:, None], *extra)


def _slots(tm, ne):
    worst = TOP_K * tm + ne * (CHUNK - 1)
    return -(-worst // V7X_LANES) * V7X_LANES


def _prefix_sum(x, axis, exclusive):
    n = x.shape[axis]
    i = jnp.arange(n)
    tri = (i[:, None] < i[None, :]) if exclusive else (i[:, None] <= i[None, :])
    xm = jnp.moveaxis(x, axis, -1)
    out = jnp.sum(xm[..., :, None] * tri.astype(x.dtype), axis=-2)
    return jnp.moveaxis(out, -1, axis)


def _moe_tables(cnt, nb):
    nch = (cnt + (CHUNK - 1)) // CHUNK
    tot = jnp.sum(nch, axis=0)
    nblk = (tot + (CHUNKS_PER_BLOCK - 1)) // CHUNKS_PER_BLOCK
    bend = _prefix_sum(nblk, 0, exclusive=False)
    gstart = (bend - nblk) * CHUNKS_PER_BLOCK
    rs = gstart[None, :] + _prefix_sum(nch, 0, exclusive=True)
    tail_start = gstart + tot
    tail_n = nblk * CHUNKS_PER_BLOCK - tot
    nused = bend[-1:]
    blk = jnp.minimum(jnp.arange(nb, dtype=I32), nused - 1)
    blk_e = jnp.sum((bend[None, :] <= blk[:, None]).astype(I32), axis=1)
    i32 = lambda a: a.astype(I32)
    return i32(nch), i32(rs), i32(tail_start), i32(tail_n), i32(nused), i32(blk_e)


def _chunk_rows(c):
    return pl.ds(pl.multiple_of(c * CHUNK, CHUNK), CHUNK)


def _for_each_run_chunk(nch_ref, rs_ref, tile, ne, fn):
    def per_expert(e, c0):
        n = nch_ref[tile, e]
        g0 = rs_ref[tile, e]

        def body(j, carry):
            fn(c0 + j, g0 + j)
            return carry
        lax.fori_loop(0, n, body, 0)
        return c0 + n
    return lax.fori_loop(0, ne, per_expert, 0)


def _one_hot_rows(pos, nrows):
    r = lax.broadcasted_iota(I32, (nrows, pos.shape[1]), 0)
    p = jnp.zeros(r.shape, F32)
    for k in range(TOP_K):
        p = p + jnp.where(r == pos[k:k + 1], 1.0, 0.0)
    return p.astype(BF16)


def _dispatch_kernel(nch_ref, rs_ref, ts_ref, tn_ref, h_ref, posg_ref, xs_hbm,
                     buf, zbuf, sem, zsem, *, ne):
    i = pl.program_id(0)
    pos = posg_ref[0][0:TOP_K].astype(I32)
    buf[...] = jnp.dot(_one_hot_rows(pos, buf.shape[0]), h_ref[...],
                       preferred_element_type=F32).astype(BF16)

    def run_copy(c, g):
        return pltpu.make_async_copy(buf.at[_chunk_rows(c)], xs_hbm.at[_chunk_rows(g)], sem)

    def zero_copy(g):
        return pltpu.make_async_copy(zbuf, xs_hbm.at[_chunk_rows(g)], zsem)

    @pl.when(i == 0)
    def _():
        zbuf[...] = jnp.zeros(zbuf.shape, BF16)
        for phase in ("start", "wait"):
            def per_expert(e, carry):
                def body(j, c):
                    cp = zero_copy(ts_ref[e] + j)
                    cp.start() if phase == "start" else cp.wait()
                    return c
                return lax.fori_loop(0, tn_ref[e], body, carry)
            lax.fori_loop(0, ne, per_expert, 0)

    total = _for_each_run_chunk(nch_ref, rs_ref, i, ne, lambda c, g: run_copy(c, g).start())

    def wait_one(j, carry):
        run_copy(0, 0).wait()
        return carry
    lax.fori_loop(0, total, wait_one, 0)


def _dispatch(h, posg, nch, rs, tail_start, tail_n, *, nb):
    T, D = h.shape
    nt, _, tm = posg.shape
    ne = nch.shape[1]
    return pl.pallas_call(
        functools.partial(_dispatch_kernel, ne=ne),
        grid_spec=pltpu.PrefetchScalarGridSpec(
            num_scalar_prefetch=4, grid=(nt,),
            in_specs=[pl.BlockSpec((tm, D), lambda i, *_: (i, 0)),
                      pl.BlockSpec((1, 2 * TOP_K, tm), lambda i, *_: (i, 0, 0))],
            out_specs=pl.BlockSpec(memory_space=pl.ANY),
            scratch_shapes=[pltpu.VMEM((_slots(tm, ne), D), BF16),
                            pltpu.VMEM((CHUNK, D), BF16),
                            pltpu.SemaphoreType.DMA(()), pltpu.SemaphoreType.DMA(())]),
        out_shape=jax.ShapeDtypeStruct((nb * MOE_BLOCK, D), BF16),
        compiler_params=_cparams(("arbitrary",)),
        name="dispatch",
    )(nch, rs, tail_start, tail_n, h, posg)


def _experts_kernel(be_ref, nu_ref, xs_ref, wgu_ref, bgu_ref, wd_ref, bd_ref, ys_ref,
                    wgu_s, wd_s):
    b = pl.program_id(0)

    @pl.when(b < nu_ref[0])
    def _():
        @pl.when((b == 0) | (be_ref[b] != be_ref[jnp.maximum(b - 1, 0)]))
        def _():
            wgu_s[...] = wgu_ref[0].astype(BF16)
            wd_s[...] = wd_ref[0].astype(BF16)

        ff = wd_s.shape[0]
        gu = jnp.dot(xs_ref[...], wgu_s[...], preferred_element_type=F32) + bgu_ref[0]
        g = jnp.minimum(gu[:, :ff], SWIGLU_LIMIT)
        u = jnp.clip(gu[:, ff:], -SWIGLU_LIMIT, SWIGLU_LIMIT)
        act = (u + 1.0) * g * jax.nn.sigmoid(SWIGLU_ALPHA * g)
        ys = jnp.dot(act.astype(BF16), wd_s[...], preferred_element_type=F32) + bd_ref[0]
        ys_ref[...] = ys.astype(ys_ref.dtype)


def _experts(xs, blk_e, nused, w_gate_up, b_gate_up, w_down, b_down):
    rows, D = xs.shape
    nb = rows // MOE_BLOCK
    ne, _, ff2 = w_gate_up.shape
    ff = w_down.shape[1]
    blk = lambda b, be, nu: (jnp.minimum(b, nu[0] - 1), 0)
    exp3 = lambda b, be, nu: (be[b], 0, 0)
    return pl.pallas_call(
        _experts_kernel,
        grid_spec=pltpu.PrefetchScalarGridSpec(
            num_scalar_prefetch=2, grid=(nb,),
            in_specs=[pl.BlockSpec((MOE_BLOCK, D), blk),
                      pl.BlockSpec((1, D, ff2), exp3), pl.BlockSpec((1, 1, ff2), exp3),
                      pl.BlockSpec((1, ff, D), exp3), pl.BlockSpec((1, 1, D), exp3)],
            out_specs=pl.BlockSpec((MOE_BLOCK, D), blk),
            scratch_shapes=[pltpu.VMEM((D, ff2), BF16), pltpu.VMEM((ff, D), BF16)]),
        out_shape=jax.ShapeDtypeStruct((rows, D), BF16),
        compiler_params=_cparams(("arbitrary",)),
        name="experts",
    )(blk_e, nused, xs, w_gate_up, b_gate_up[:, None, :], w_down, b_down[:, None, :])


def _combine_kernel(nch_ref, rs_ref, ys_hbm, posg_ref, x1_ref, y_ref, ylast_ref, buf, sem, *, ne):
    i = pl.program_id(0)
    tm = x1_ref.shape[0]
    nslot_chunks = buf.shape[0] // CHUNK

    def run_copy(c, g):
        return pltpu.make_async_copy(ys_hbm.at[_chunk_rows(g)], buf.at[_chunk_rows(c)], sem)

    total = _for_each_run_chunk(nch_ref, rs_ref, i, ne, lambda c, g: run_copy(c, g).start())

    def zero_chunk(c, carry):
        buf[_chunk_rows(c), :] = jnp.zeros((CHUNK, buf.shape[1]), BF16)
        return carry
    lax.fori_loop(total, nslot_chunks, zero_chunk, 0)

    r0 = lax.broadcasted_iota(I32, (tm, tm), 0)
    r1 = lax.broadcasted_iota(I32, (tm, tm), 1)
    posg_t = _dot_nt(jnp.where(r0 == r1, 1.0, 0.0), posg_ref[0], exact=True)
    slot = lax.broadcasted_iota(I32, (tm, buf.shape[0]), 1)
    w = jnp.zeros(slot.shape, F32)
    for k in range(TOP_K):
        w = w + jnp.where(slot == posg_t[:, k:k + 1].astype(I32),
                          posg_t[:, TOP_K + k:TOP_K + k + 1], 0.0)

    def wait_one(j, carry):
        run_copy(0, 0).wait()
        return carry
    lax.fori_loop(0, total, wait_one, 0)
    y = x1_ref[...] + jnp.dot(w.astype(BF16), buf[...], preferred_element_type=F32)
    last = pl.num_programs(0) - 1

    @pl.when(i < last)
    def _():
        y_ref[...] = y

    @pl.when(i == last)
    def _():
        ylast_ref[...] = y


def _combine(ys, posg, x1, nch, rs):
    T, D = x1.shape
    nt, _, tm = posg.shape
    ne = nch.shape[1]
    return pl.pallas_call(
        functools.partial(_combine_kernel, ne=ne),
        grid_spec=pltpu.PrefetchScalarGridSpec(
            num_scalar_prefetch=2, grid=(nt,),
            in_specs=[pl.BlockSpec(memory_space=pl.ANY),
                      pl.BlockSpec((1, 2 * TOP_K, tm), lambda i, *_: (i, 0, 0)),
                      pl.BlockSpec((tm, D), lambda i, *_: (i, 0))],
            out_specs=[pl.BlockSpec((tm, D), lambda i, *_: (jnp.minimum(i, nt - 2), 0)),
                       pl.BlockSpec((tm, D), lambda i, *_: (0, 0))],
            scratch_shapes=[pltpu.VMEM((_slots(tm, ne), D), BF16),
                            pltpu.SemaphoreType.DMA(())]),
        out_shape=[jax.ShapeDtypeStruct((T - tm, D), F32), jax.ShapeDtypeStruct((tm, D), F32)],
        compiler_params=_cparams(("arbitrary",)),
        name="combine",
    )(nch, rs, ys, posg, x1)


def _moe(h, posg, cnt, x1, w_gate_up, b_gate_up, w_down, b_down):
    nt, _, tm = posg.shape
    ne = cnt.shape[1]
    max_chunks = (nt * tm * TOP_K) // CHUNK + nt * ne
    nb = -(-max_chunks // CHUNKS_PER_BLOCK) + ne
    nch, rs, tail_start, tail_n, nused, blk_e = _moe_tables(cnt, nb)
    xs = _dispatch(h, posg, nch, rs, tail_start, tail_n, nb=nb)
    ys = _experts(xs, blk_e, nused, w_gate_up, b_gate_up, w_down, b_down)
    return _combine(ys, posg, x1, nch, rs)


def kernel(x_prompt, x_sample, cache_k, cache_v, state_conv, page_table, norm1_w, w_in,
           q_norm_w, k_norm_w, lambda_q1, lambda_k1, lambda_q2, lambda_k2, subln_w,
           conv_w, conv_b, conv_ln_g, conv_ln_b, w_out, norm2_w, router_w, router_b,
           w_gate_up, b_gate_up, w_down, b_down):
    B, S, D = x_prompt.shape
    Bs, Ss, _ = x_sample.shape
    depth = norm1_w.shape[0]
    n_phys, page, heads, _, qk = cache_k.shape[1:]
    vdim = cache_v.shape[-1]
    qc, vc, cc = heads * 2 * qk, heads * vdim, conv_w.shape[2]
    taps = conv_w.shape[1]
    assert Ss == 1 and qk == QK_GROUP and vdim == V7X_LANES and (B * S) % MOE_TILE == 0
    assert Bs <= MOE_TILE and S >= taps - 1
    n_past = page_table.shape[1] * page
    T = B * S
    nt_p = T // MOE_TILE
    pos_p = jnp.arange(S, dtype=F32)
    pos_s = jnp.full((Bs,), n_past, F32)
    xp = x_prompt.reshape(T, D)
    xs = x_sample.reshape(Bs, D)
    pad_tile = lambda a: jnp.pad(a, ((0, MOE_TILE - Bs), (0, 0)))
    outs = [[] for _ in range(6)]
    for l in range(depth):
        lam_init = 0.8 - 0.6 * math.exp(-0.3 * l)
        lamv = jnp.stack([lambda_q1[l], lambda_k1[l], lambda_q2[l], lambda_k2[l]])
        conv_p = (conv_w[l], conv_b[l], conv_ln_g[l], conv_ln_b[l])
        proj_p = (norm1_w[l], w_in[l], q_norm_w[l], k_norm_w[l])
        tail_p = (w_out[l], norm2_w[l], router_w[l], router_b[l])

        q, kt, v, u, kbt, vb = _proj(xp, pos_p, S // PROJ_TILE, PROJ_TILE, *proj_p,
                                     qc=qc, vc=vc, cc=cc, exact_norm=False, attn_layout=True)
        attn = _attn_prompt(q, kbt, vb, lamv, subln_w[l], lam_init=lam_init)
        conv = _conv_prompt(u, *conv_p, batch=B, seq=S)
        bufs = _tail(attn, conv, xp, *tail_p, tm=MOE_TILE, n_valid=MOE_TILE,
                     total_tiles=nt_p + 1)
        outs[0].append(jnp.transpose(kt.reshape(B, heads, 2, qk, S), (0, 4, 1, 2, 3)))
        outs[1].append(v.reshape(B, S, heads, vdim))
        outs[2].append(u.reshape(B, S, cc)[:, S - (taps - 1):])

        qs, ks_, vs_, us = _proj(xs, pos_s, 1, Bs, *proj_p, qc=qc, vc=vc, cc=cc,
                                 exact_norm=True, attn_layout=False)
        cache_kt = jnp.transpose(cache_k[l], (0, 2, 3, 4, 1)).reshape(n_phys, qc, page)
        cache_vr = cache_v[l].reshape(n_phys, page * heads, vdim)
        attn_s = _attn_decode(qs, ks_, vs_, cache_kt, cache_vr, page_table, lamv, subln_w[l],
                              heads=heads, lam_init=lam_init)
        conv_s = _conv_step(state_conv[l], us, *conv_p)
        x1, h, posg, cnt = _tail(pad_tile(attn_s), pad_tile(conv_s), pad_tile(xs), *tail_p,
                                 tm=MOE_TILE, n_valid=Bs, total_tiles=nt_p + 1,
                                 first_tile=nt_p, into=bufs)
        outs[3].append(ks_.reshape(Bs, Ss, heads, 2, qk))
        outs[4].append(vs_.reshape(Bs, Ss, heads, vdim))
        outs[5].append(jnp.concatenate([state_conv[l][:, Ss:], us[:, None, :]], axis=1))

        xp, y_last = _moe(h, posg, cnt[:, :, 0], x1, w_gate_up[l], b_gate_up[l], w_down[l],
                          b_down[l])
        xs = y_last[:Bs]
    return (xp.reshape(B, S, D), xs.reshape(Bs, Ss, D)) + tuple(jnp.stack(o) for o in outs)
```

```python
import functools
import math

import jax
import jax.numpy as jnp
from jax import lax
from jax.experimental import pallas as pl
from jax.experimental.pallas import tpu as pltpu

F32 = jnp.float32
BF16 = jnp.bfloat16
I32 = jnp.int32
HIGHEST = lax.Precision.HIGHEST

EPS = 1e-6
ROPE_THETA = 10000.0
SWIGLU_LIMIT = 7.0
SWIGLU_ALPHA = 1.702
TOP_K = 4
NEG = -1e30
QK_GROUP = 64

V7X_LANES = 128
V7X_SUBLANES = 8
VMEM_LIMIT = 56 * 1024 * 1024
BF16_ROWS = 16

MOE_TILE = 256
CHUNK = BF16_ROWS
MOE_BLOCK = 512
CHUNKS_PER_BLOCK = MOE_BLOCK // CHUNK
PROJ_TILE = 512


def _cparams(sem, vmem=VMEM_LIMIT):
    return pltpu.CompilerParams(dimension_semantics=sem, vmem_limit_bytes=vmem)


def _dot(a, b, exact=False):
    if exact:
        return jnp.dot(a.astype(F32), b.astype(F32), precision=HIGHEST,
                       preferred_element_type=F32)
    return jnp.dot(a.astype(BF16), b.astype(BF16), preferred_element_type=F32)


def _dot_nt(a, b, exact=False):
    dn = (((1,), (1,)), ((), ()))
    if exact:
        return lax.dot_general(a.astype(F32), b.astype(F32), dn, precision=HIGHEST,
                               preferred_element_type=F32)
    return lax.dot_general(a.astype(BF16), b.astype(BF16), dn, preferred_element_type=F32)


def _bf16_round(x):
    return x.astype(BF16).astype(F32)


def _rope_norm(p, gsum, w, cos, sin, first_half, exact_norm):
    ss = _dot(p * p, gsum, exact_norm)
    n = p * lax.rsqrt(ss * (1.0 / QK_GROUP) + EPS) * w
    outs = []
    for j in range(p.shape[1] // V7X_LANES):
        nj = n[:, j * V7X_LANES:(j + 1) * V7X_LANES]
        rot = jnp.where(first_half, pltpu.roll(nj, V7X_LANES - QK_GROUP // 2, 1),
                        pltpu.roll(nj, QK_GROUP // 2, 1))
        outs.append(nj * cos + rot * sin)
    return jnp.concatenate(outs, axis=1)


def _proj_kernel(x_ref, n1_ref, w_ref, qw_ref, kw_ref, cos_ref, sin_ref, gsum_ref,
                 q_ref, k_ref, v_ref, u_ref, *rest, qc, vc, cc, scale, exact_norm):
    x = x_ref[...]
    h = x * lax.rsqrt(jnp.mean(x * x, axis=-1, keepdims=True) + EPS) * n1_ref[...]
    hm = h.astype(BF16)
    cos = cos_ref[...]
    sin = sin_ref[...]
    lane = lax.broadcasted_iota(I32, cos.shape, 1)
    first_half = (lane % QK_GROUP) < QK_GROUP // 2
    gsum = gsum_ref[...]

    q = _rope_norm(_dot(hm, w_ref[:, 0:qc]), gsum, qw_ref[...], cos, sin, first_half, exact_norm)
    q_ref[...] = (q * scale).astype(q_ref.dtype)
    k = _rope_norm(_dot(hm, w_ref[:, qc:2 * qc]), gsum, kw_ref[...], cos, sin, first_half,
                   exact_norm)
    v = _dot(hm, w_ref[:, 2 * qc:2 * qc + vc])
    v_ref[...] = v
    o = 2 * qc + vc
    ua = _dot(hm, w_ref[:, o:o + cc])
    ub = _dot(hm, w_ref[:, o + cc:o + 2 * cc])
    u_ref[...] = ua * jax.nn.sigmoid(ub)
    if rest:
        kb_ref, vb_ref = rest
        kt = k.T
        k_ref[0] = kt
        kb_ref[0, :, 0] = kt.astype(BF16).reshape(kb_ref.shape[1], kb_ref.shape[3], kt.shape[1])
        vb_ref[...] = v.astype(BF16)
    else:
        k_ref[...] = k


def _rope_tables(pos):
    half = QK_GROUP // 2
    inv = jnp.power(ROPE_THETA, -jnp.arange(half, dtype=F32) / half)
    ang = pos[:, None] * inv[None, :]
    reps = V7X_LANES // QK_GROUP
    cos = jnp.tile(jnp.cos(ang), (1, 2 * reps))
    s = jnp.sin(ang)
    sin = jnp.tile(jnp.concatenate([-s, s], axis=1), (1, reps))
    return cos, sin


def _proj(x2d, pos_rows, n_pos_blocks, tm, norm1_w, w_in, q_norm_w, k_norm_w, *, qc, vc, cc,
          exact_norm, attn_layout):
    T, D = x2d.shape
    cos, sin = _rope_tables(pos_rows)
    gi = jnp.arange(qc) // QK_GROUP
    gsum = (gi[:, None] == gi[None, :]).astype(F32 if exact_norm else BF16)
    qw = jnp.tile(q_norm_w, qc // QK_GROUP)[None, :]
    kw = jnp.tile(k_norm_w, qc // QK_GROUP)[None, :]
    w = w_in.astype(BF16)
    row = lambda i: (i, 0)
    full = lambda i: (0, 0)
    out_shape = [jax.ShapeDtypeStruct((T, qc), BF16),
                 jax.ShapeDtypeStruct((T, qc), F32),
                 jax.ShapeDtypeStruct((T, vc), F32),
                 jax.ShapeDtypeStruct((T, cc), F32)]
    out_specs = [pl.BlockSpec((tm, qc), row), pl.BlockSpec((tm, qc), row),
                 pl.BlockSpec((tm, vc), row), pl.BlockSpec((tm, cc), row)]
    if attn_layout:
        nseq = T // (n_pos_blocks * tm)
        heads = qc // (2 * QK_GROUP)
        seq_tile = lambda i: (i // n_pos_blocks, 0, i % n_pos_blocks)
        out_shape[1] = jax.ShapeDtypeStruct((nseq, qc, n_pos_blocks * tm), F32)
        out_specs[1] = pl.BlockSpec((1, qc, tm), seq_tile)
        out_shape += [jax.ShapeDtypeStruct((nseq, heads, n_pos_blocks, 2 * QK_GROUP, tm), BF16),
                      jax.ShapeDtypeStruct((T, vc), BF16)]
        out_specs += [pl.BlockSpec((1, heads, 1, 2 * QK_GROUP, tm),
                                   lambda i: (i // n_pos_blocks, 0, i % n_pos_blocks, 0, 0)),
                      pl.BlockSpec((tm, vc), row)]
    return pl.pallas_call(
        functools.partial(_proj_kernel, qc=qc, vc=vc, cc=cc, scale=QK_GROUP ** -0.5,
                          exact_norm=exact_norm),
        grid=(T // tm,),
        in_specs=[pl.BlockSpec((tm, D), row),
                  pl.BlockSpec((1, D), full),
                  pl.BlockSpec(w.shape, full),
                  pl.BlockSpec((1, qc), full),
                  pl.BlockSpec((1, qc), full),
                  pl.BlockSpec((tm, V7X_LANES), lambda i: (i % n_pos_blocks, 0)),
                  pl.BlockSpec((tm, V7X_LANES), lambda i: (i % n_pos_blocks, 0)),
                  pl.BlockSpec((qc, qc), full)],
        out_specs=out_specs,
        out_shape=out_shape,
        compiler_params=_cparams(("arbitrary",)),
        name="proj" if attn_layout else "proj_step",
    )(x2d, norm1_w[None, :], w, qw, kw, cos, sin, gsum)


def _lambda_value(lv, lam_init):
    a = jnp.sum(lv[0:1] * lv[1:2], axis=-1, keepdims=True)
    b = jnp.sum(lv[2:3] * lv[3:4], axis=-1, keepdims=True)
    return jnp.exp(a) - jnp.exp(b) + lam_init


def _subln(o, w, lam_init):
    y = o * lax.rsqrt(jnp.mean(o * o, axis=-1, keepdims=True) + EPS)
    return y * w * (1.0 - lam_init)


def _attn_kernel(lam_ref, sw_ref, q_ref, k_ref, v_ref, o_ref, *, tq, lam_init):
    i = pl.program_id(2)
    lam = _lambda_value(lam_ref[...], lam_init)
    q = q_ref[...]
    lane = lax.broadcasted_iota(I32, q.shape, 1)
    zero = jnp.zeros_like(q)
    qs = (jnp.where(lane < QK_GROUP, q, zero), jnp.where(lane >= QK_GROUP, q, zero))

    def chunk(j, carry, masked):
        kc = k_ref[0, 0, j]
        vc = v_ref[pl.ds(pl.multiple_of(j * tq, tq), tq), :]
        out = []
        for c in range(2):
            m, l, acc = carry[c]
            s = jnp.dot(qs[c], kc, preferred_element_type=F32)
            if masked:
                row = lax.broadcasted_iota(I32, s.shape, 0)
                col = lax.broadcasted_iota(I32, s.shape, 1)
                s = jnp.where(col <= row, s, NEG)
            m_new = jnp.maximum(m, jnp.max(s, axis=-1, keepdims=True))
            p = jnp.exp(s - m_new)
            alpha = jnp.exp(m - m_new)
            l = alpha * l + jnp.sum(p, axis=-1, keepdims=True)
            acc = alpha * acc + _dot(p, vc)
            out.append((m_new, l, acc))
        return tuple(out)

    init = tuple((jnp.full((tq, 1), NEG, F32), jnp.zeros((tq, 1), F32),
                  jnp.zeros((tq, V7X_LANES), F32)) for _ in range(2))
    carry = lax.fori_loop(0, i, lambda j, c: chunk(j, c, False), init)
    (_, l0, a0), (_, l1, a1) = chunk(i, carry, True)
    o = a0 / l0 - lam * (a1 / l1)
    o_ref[...] = _subln(o, sw_ref[...], lam_init).astype(o_ref.dtype)


def _attn_prompt(q, kbt, vb, lamv, subln_w, *, lam_init):
    batch, heads, nq, _, tq = kbt.shape
    seq = nq * tq
    return pl.pallas_call(
        functools.partial(_attn_kernel, tq=tq, lam_init=lam_init),
        grid=(batch, heads, nq),
        in_specs=[pl.BlockSpec(lamv.shape, lambda b, h, i: (0, 0)),
                  pl.BlockSpec((1, V7X_LANES), lambda b, h, i: (0, 0)),
                  pl.BlockSpec((tq, V7X_LANES), lambda b, h, i: (b * nq + i, h)),
                  pl.BlockSpec((1, 1, nq, V7X_LANES, tq), lambda b, h, i: (b, h, 0, 0, 0)),
                  pl.BlockSpec((seq, V7X_LANES), lambda b, h, i: (b, h))],
        out_specs=pl.BlockSpec((tq, V7X_LANES), lambda b, h, i: (b * nq + i, h)),
        out_shape=jax.ShapeDtypeStruct((batch * seq, heads * V7X_LANES), BF16),
        compiler_params=_cparams(("arbitrary", "arbitrary", "arbitrary")),
        name="attn",
    )(lamv, subln_w[None, :], q, kbt, vb)


DECODE_PAGES_PER_STEP = 32
SOFTMAX_PAGES = 16


def _decode_kernel(pt_ref, lam_ref, sw_ref, qm_ref, kn_ref, vn_ref, *rest,
                   pps, page, heads, lam_init):
    k_refs = rest[:pps]
    v_refs = rest[pps:2 * pps]
    o_ref, s_ref, m_ref, coef_ref, acc_ref = rest[2 * pps:]
    s = pl.program_id(1)
    half_steps = pl.num_programs(1) // 2
    rows = s_ref.shape[1]
    qm = qm_ref[0]
    row = lax.broadcasted_iota(I32, (rows, V7X_LANES), 0)
    n_pages = half_steps * pps

    @pl.when(s == 0)
    def _():
        m_ref[...] = jnp.full(m_ref.shape, NEG, F32)

    @pl.when(s < half_steps)
    def _():
        m = m_ref[...]
        for j in range(pps):
            sc = jnp.dot(qm, k_refs[j][0].astype(BF16), preferred_element_type=F32)
            s_ref[s * pps + j] = sc
            m = jnp.maximum(m, sc)
        m_ref[...] = m

    def head_weights(p):
        a = p * coef_ref[...]
        return _bf16_round(a + pltpu.roll(a, rows - heads, 0))

    @pl.when(s == half_steps)
    def _():
        s_new = jnp.sum(qm.astype(F32) * _bf16_round(kn_ref[0]), axis=-1, keepdims=True)
        m = jnp.maximum(jnp.max(m_ref[...], axis=-1, keepdims=True), s_new)

        def exp_pages(t, l):
            pages = pl.ds(pl.multiple_of(t * SOFTMAX_PAGES, SOFTMAX_PAGES), SOFTMAX_PAGES)
            p = jnp.exp(s_ref[pages] - m[None])
            s_ref[pages] = p
            return l + jnp.sum(p, axis=0)

        lsum = lax.fori_loop(0, n_pages // SOFTMAX_PAGES, exp_pages,
                             jnp.zeros((rows, V7X_LANES), F32))
        p_new = jnp.exp(s_new - m)
        l = jnp.sum(lsum, axis=-1, keepdims=True) + p_new
        lam = _lambda_value(lam_ref[...], lam_init)
        coef = jnp.where(row[:, 0:1] < heads, 1.0, -lam) / l
        coef_ref[...] = jnp.broadcast_to(coef, coef_ref.shape)
        acc_ref[...] = (head_weights(jnp.broadcast_to(p_new, (rows, V7X_LANES)))
                        * _bf16_round(vn_ref[0]))

    @pl.when(s >= half_steps)
    def _():
        lane = lax.broadcasted_iota(I32, (rows, V7X_LANES), 1)
        keep = (lane % heads == row) & (row < heads)
        acc = acc_ref[...]
        for j in range(pps):
            a = head_weights(s_ref[(s - half_steps) * pps + j])
            parts = []
            for c in range(heads):
                idx = (c * page + lane) // heads
                parts.append(jnp.where(keep, jnp.take_along_axis(a, idx, axis=1), 0.0))
            a_exp = jnp.concatenate(parts, axis=1).astype(BF16)
            acc = acc + jnp.dot(a_exp, v_refs[j][0].astype(BF16), preferred_element_type=F32)
        acc_ref[...] = acc

    @pl.when(s == 2 * half_steps - 1)
    def _():
        o_ref[0] = _subln(acc_ref[...], sw_ref[...], lam_init)


def _attn_decode(q, k_new, v_new, cache_kt, cache_vr, page_table, lamv, subln_w, *, heads,
                 lam_init):
    B, D = q.shape
    n_pages = page_table.shape[1]
    page = cache_kt.shape[2]
    vdim = cache_vr.shape[2]
    pps = DECODE_PAGES_PER_STEP
    half_steps = n_pages // pps
    rows = 2 * heads
    assert rows == V7X_SUBLANES and page == V7X_LANES and vdim == V7X_LANES
    assert n_pages % pps == 0 and n_pages % SOFTMAX_PAGES == 0
    group = jnp.arange(D) // QK_GROUP
    rowmask = ((group % 2) * heads + group // 2)[None, :] == jnp.arange(rows)[:, None]
    qm = jnp.where(rowmask[None], q[:, None, :], jnp.zeros((), BF16))
    vn = jnp.pad(v_new.reshape(B, heads, vdim), ((0, 0), (0, rows - heads), (0, 0)))

    def k_map(j):
        return lambda b, s, pt: (pt[b, jnp.minimum(s, half_steps - 1) * pps + j], 0, 0)

    def v_map(j):
        return lambda b, s, pt: (pt[b, jnp.maximum(s - half_steps, 0) * pps + j], 0, 0)

    vec3 = lambda b, s, pt: (b, 0, 0)
    const = lambda b, s, pt: (0, 0)
    out = pl.pallas_call(
        functools.partial(_decode_kernel, pps=pps, page=page, heads=heads, lam_init=lam_init),
        grid_spec=pltpu.PrefetchScalarGridSpec(
            num_scalar_prefetch=1, grid=(B, 2 * half_steps),
            in_specs=[pl.BlockSpec(lamv.shape, const), pl.BlockSpec((1, vdim), const),
                      pl.BlockSpec((1, rows, D), vec3), pl.BlockSpec((1, 1, D), vec3),
                      pl.BlockSpec((1, rows, vdim), vec3)]
                     + [pl.BlockSpec((1, D, page), k_map(j)) for j in range(pps)]
                     + [pl.BlockSpec((1, page * heads, vdim), v_map(j)) for j in range(pps)],
            out_specs=pl.BlockSpec((1, rows, vdim), vec3),
            scratch_shapes=[pltpu.VMEM((n_pages, rows, page), F32),
                            pltpu.VMEM((rows, V7X_LANES), F32),
                            pltpu.VMEM((rows, V7X_LANES), F32),
                            pltpu.VMEM((rows, vdim), F32)]),
        out_shape=jax.ShapeDtypeStruct((B, rows, vdim), F32),
        compiler_params=_cparams(("arbitrary", "arbitrary")),
        name="decode",
    )(page_table, lamv, subln_w[None, :], qm, k_new[:, None, :], vn,
      *([cache_kt] * pps), *([cache_vr] * pps))
    return out[:, :heads, :].reshape(B, heads * vdim)


CONV_HALO = 32


def _ln_swish(y, b_ref, g_ref, be_ref):
    y = y + b_ref[...]
    mu = jnp.mean(y, axis=-1, keepdims=True)
    yc = y - mu
    z = yc * lax.rsqrt(jnp.mean(yc * yc, axis=-1, keepdims=True) + EPS) * g_ref[...] + be_ref[...]
    return z * jax.nn.sigmoid(z)


def _conv_kernel(u_ref, w_ref, b_ref, g_ref, be_ref, o_ref, buf_ref, part_ref, *, tc, taps):
    sub = V7X_SUBLANES
    first = CONV_HALO - (taps - 1)

    @pl.when(pl.program_id(1) == 0)
    def _():
        buf_ref[0:CONV_HALO, :] = jnp.zeros((CONV_HALO, buf_ref.shape[1]), F32)
        buf_ref[CONV_HALO + tc:, :] = jnp.zeros((sub, buf_ref.shape[1]), F32)

    buf_ref[CONV_HALO:CONV_HALO + tc, :] = _bf16_round(u_ref[...])
    acc = None
    for r in range(sub):
        part = None
        for a in range(-(-(first + taps) // sub)):
            k = sub * a + r - first
            if 0 <= k < taps:
                term = buf_ref[sub * a:sub * a + tc + sub, :] * _bf16_round(w_ref[k:k + 1, :])
                part = term if part is None else part + term
        if r == 0:
            acc = part[0:tc]
        else:
            part_ref[...] = part
            acc = acc + part_ref[r:r + tc, :]
    o_ref[...] = _ln_swish(acc, b_ref, g_ref, be_ref).astype(o_ref.dtype)
    buf_ref[0:CONV_HALO, :] = buf_ref[tc:tc + CONV_HALO, :]


def _conv_prompt(u2d, conv_w, conv_b, ln_g, ln_b, *, batch, seq, tc=512):
    taps, C = conv_w.shape
    nt = seq // tc
    vec = lambda b, i: (0, 0)
    return pl.pallas_call(
        functools.partial(_conv_kernel, tc=tc, taps=taps),
        grid=(batch, nt),
        in_specs=[pl.BlockSpec((tc, C), lambda b, i: (b * nt + i, 0)),
                  pl.BlockSpec((taps, C), vec), pl.BlockSpec((1, C), vec),
                  pl.BlockSpec((1, C), vec), pl.BlockSpec((1, C), vec)],
        out_specs=pl.BlockSpec((tc, C), lambda b, i: (b * nt + i, 0)),
        out_shape=jax.ShapeDtypeStruct((batch * seq, C), BF16),
        scratch_shapes=[pltpu.VMEM((tc + CONV_HALO + V7X_SUBLANES, C), F32),
                        pltpu.VMEM((tc + V7X_SUBLANES, C), F32)],
        compiler_params=_cparams(("arbitrary", "arbitrary")),
        name="conv",
    )(u2d, conv_w, conv_b[None, :], ln_g[None, :], ln_b[None, :])


def _conv_step_kernel(st_ref, u_ref, w_ref, b_ref, g_ref, be_ref, o_ref, *, taps):
    acc = _bf16_round(u_ref[...]) * _bf16_round(w_ref[taps - 1:taps, :])
    for k in range(taps - 1):
        acc = acc + _bf16_round(st_ref[:, k, :]) * _bf16_round(w_ref[k:k + 1, :])
    o_ref[...] = _ln_swish(acc, b_ref, g_ref, be_ref)


def _conv_step(state, u, conv_w, conv_b, ln_g, ln_b):
    taps, C = conv_w.shape
    return pl.pallas_call(
        functools.partial(_conv_step_kernel, taps=taps),
        out_shape=jax.ShapeDtypeStruct(u.shape, F32),
        compiler_params=_cparams(None),
        name="conv_step",
    )(state, u, conv_w, conv_b[None, :], ln_g[None, :], ln_b[None, :])


def _tail_kernel(a_ref, c_ref, x_ref, wo_ref, n2_ref, rw_ref, rb_ref, *rest, n_valid):
    x1_ref, h_ref, posg_ref, cnt_ref = rest[-4:]
    tm = x_ref.shape[0]
    half = a_ref.shape[1]
    x1 = (x_ref[...] + _dot(a_ref[...], wo_ref[0:half, :])
          + _dot(c_ref[...], wo_ref[half:, :]))
    x1_ref[...] = x1
    h = (x1 * lax.rsqrt(jnp.mean(x1 * x1, axis=-1, keepdims=True) + EPS)
         * n2_ref[...]).astype(BF16)
    h_ref[...] = h
    logits = _dot_nt(rw_ref[...], h) + rb_ref[...]
    ne = logits.shape[0]
    eidx = lax.broadcasted_iota(I32, logits.shape, 0)
    valid = lax.broadcasted_iota(I32, (1, tm), 1) < n_valid

    sels, vals = [], []
    l = logits
    for _ in range(TOP_K):
        m = jnp.max(l, axis=0, keepdims=True)
        first = jnp.min(jnp.where(l == m, eidx, ne), axis=0, keepdims=True)
        sel = (eidx == first) & valid
        l = jnp.where(eidx == first, -jnp.inf, l)
        sels.append(sel)
        vals.append(m)
    ex = [jnp.exp(v - vals[0]) for v in vals]
    den = ex[0] + ex[1] + ex[2] + ex[3]
    gates = [jnp.where(valid, e / den, 0.0) for e in ex]

    msel = jnp.zeros(logits.shape, F32)
    for sel in sels:
        msel = msel + jnp.where(sel, 1.0, 0.0)
    r0 = lax.broadcasted_iota(I32, (tm, tm), 0)
    r1 = lax.broadcasted_iota(I32, (tm, tm), 1)
    upper = jnp.where(r0 < r1, 1.0, 0.0).astype(BF16)
    rank = jnp.dot(msel.astype(BF16), upper, preferred_element_type=F32)
    cnt = jnp.sum(msel, axis=1, keepdims=True)
    pcnt = jnp.ceil(cnt * (1.0 / CHUNK)) * CHUNK
    e0 = lax.broadcasted_iota(I32, (ne, ne), 0)
    e1 = lax.broadcasted_iota(I32, (ne, ne), 1)
    lower = jnp.where(e1 < e0, 1.0, 0.0)
    off = jnp.dot(lower.astype(BF16), jnp.broadcast_to(pcnt, (ne, V7X_LANES)).astype(BF16),
                  preferred_element_type=F32)[:, 0:1]
    pos = off + rank
    rows = [jnp.where(valid, jnp.sum(jnp.where(sel, pos, 0.0), axis=0, keepdims=True), -1.0)
            for sel in sels]
    posg_ref[0] = jnp.concatenate(rows + gates, axis=0)
    cnt_ref[0] = jnp.broadcast_to(cnt, (ne, V7X_LANES)).astype(I32)


def _tail(attn, conv, x2d, w_out, norm2_w, router_w, router_b, *, tm, n_valid, total_tiles,
          first_tile=0, into=None):
    T, D = x2d.shape
    half = attn.shape[1]
    ne = router_w.shape[1]
    nt = T // tm
    wo = w_out.astype(BF16)
    rw = router_w.T.astype(BF16)
    row = lambda i: (i, 0)
    full = lambda i: (0, 0)
    orow = lambda i: (first_tile + i, 0)
    otile = lambda i: (first_tile + i, 0, 0)
    extra = list(into) if into is not None else []
    return pl.pallas_call(
        functools.partial(_tail_kernel, n_valid=n_valid),
        grid=(nt,),
        in_specs=[pl.BlockSpec((tm, half), row), pl.BlockSpec((tm, half), row),
                  pl.BlockSpec((tm, D), row), pl.BlockSpec((D, D), full),
                  pl.BlockSpec((1, D), full), pl.BlockSpec((ne, D), full),
                  pl.BlockSpec((ne, 1), full)]
                 + [pl.BlockSpec(memory_space=pl.ANY)] * len(extra),
        out_specs=[pl.BlockSpec((tm, D), orow), pl.BlockSpec((tm, D), orow),
                   pl.BlockSpec((1, 2 * TOP_K, tm), otile),
                   pl.BlockSpec((1, ne, V7X_LANES), otile)],
        out_shape=[jax.ShapeDtypeStruct((total_tiles * tm, D), F32),
                   jax.ShapeDtypeStruct((total_tiles * tm, D), BF16),
                   jax.ShapeDtypeStruct((total_tiles, 2 * TOP_K, tm), F32),
                   jax.ShapeDtypeStruct((total_tiles, ne, V7X_LANES), I32)],
        input_output_aliases={7 + j: j for j in range(len(extra))},
        compiler_params=_cparams(("arbitrary",)),
        name="tail",
    )(attn, conv, x2d, wo, norm2_w[None, :], rw, router_b[:, None], *extra)


def _slots(tm, ne):
    worst = TOP_K * tm + ne * (CHUNK - 1)
    return -(-worst // V7X_LANES) * V7X_LANES


def _prefix_sum(x, axis, exclusive):
    n = x.shape[axis]
    i = jnp.arange(n)
    tri = (i[:, None] < i[None, :]) if exclusive else (i[:, None] <= i[None, :])
    xm = jnp.moveaxis(x, axis, -1)
    out = jnp.sum(xm[..., :, None] * tri.astype(x.dtype), axis=-2)
    return jnp.moveaxis(out, -1, axis)


def _moe_tables(cnt, nb, slot_chunks):
    nch = (cnt + (CHUNK - 1)) // CHUNK
    tot = jnp.sum(nch, axis=0)
    nblk = (tot + (CHUNKS_PER_BLOCK - 1)) // CHUNKS_PER_BLOCK
    bend = _prefix_sum(nblk, 0, exclusive=False)
    gstart = (bend - nblk) * CHUNKS_PER_BLOCK
    rs = gstart[None, :] + _prefix_sum(nch, 0, exclusive=True)
    tail_start = gstart + tot
    tail_n = nblk * CHUNKS_PER_BLOCK - tot
    nused = bend[-1:]
    blk = jnp.minimum(jnp.arange(nb, dtype=I32), nused - 1)
    blk_e = jnp.sum((bend[None, :] <= blk[:, None]).astype(I32), axis=1)
    cend = _prefix_sum(nch, 1, exclusive=False)
    c = jnp.arange(slot_chunks, dtype=I32)
    run = jnp.minimum(jnp.sum((cend[:, None, :] <= c[None, :, None]).astype(I32), axis=2),
                      nch.shape[1] - 1)
    shift = rs - (cend - nch)
    dest = c[None, :] + jnp.sum(jnp.where(run[:, :, None] == jnp.arange(nch.shape[1]),
                                          shift[:, None, :], 0), axis=2)
    i32 = lambda a: a.astype(I32)
    return (i32(cend[:, -1]), i32(dest), i32(tail_start), i32(tail_n), i32(nused), i32(blk_e))


def _chunk_rows(c):
    return pl.ds(pl.multiple_of(c * CHUNK, CHUNK), CHUNK)


def _for_each_chunk(n, fn):
    def body(c, carry):
        fn(c)
        return carry
    lax.fori_loop(0, n, body, 0)


def _one_hot_rows(pos, nrows):
    r = lax.broadcasted_iota(I32, (nrows, pos.shape[1]), 0)
    p = jnp.zeros(r.shape, F32)
    for k in range(TOP_K):
        p = p + jnp.where(r == pos[k:k + 1], 1.0, 0.0)
    return p.astype(BF16)


def _dispatch_kernel(n_ref, dest_ref, ts_ref, tn_ref, h_ref, posg_ref, xs_hbm,
                     buf, zbuf, sem, zsem, *, ne):
    i = pl.program_id(0)
    slot = i % 2
    pos = posg_ref[0][0:TOP_K].astype(I32)
    buf[slot] = jnp.dot(_one_hot_rows(pos, buf.shape[1]), h_ref[...],
                        preferred_element_type=F32).astype(BF16)

    def run_copy(tile, c):
        s = tile % 2
        return pltpu.make_async_copy(buf.at[s, _chunk_rows(c)],
                                     xs_hbm.at[_chunk_rows(dest_ref[tile, c])], sem.at[s])

    def zero_copy(g):
        return pltpu.make_async_copy(zbuf, xs_hbm.at[_chunk_rows(g)], zsem)

    @pl.when(i == 0)
    def _():
        zbuf[...] = jnp.zeros(zbuf.shape, BF16)
        for phase in ("start", "wait"):
            def per_expert(e, carry):
                def body(j, c):
                    cp = zero_copy(ts_ref[e] + j)
                    cp.start() if phase == "start" else cp.wait()
                    return c
                return lax.fori_loop(0, tn_ref[e], body, carry)
            lax.fori_loop(0, ne, per_expert, 0)

    _for_each_chunk(n_ref[i], lambda c: run_copy(i, c).start())

    @pl.when(i > 0)
    def _():
        _for_each_chunk(n_ref[i - 1], lambda c: run_copy(i - 1, c).wait())

    @pl.when(i == pl.num_programs(0) - 1)
    def _():
        _for_each_chunk(n_ref[i], lambda c: run_copy(i, c).wait())


def _dispatch(h, posg, nchunks, dest, tail_start, tail_n, *, nb):
    T, D = h.shape
    nt, _, tm = posg.shape
    ne = tail_n.shape[0]
    return pl.pallas_call(
        functools.partial(_dispatch_kernel, ne=ne),
        grid_spec=pltpu.PrefetchScalarGridSpec(
            num_scalar_prefetch=4, grid=(nt,),
            in_specs=[pl.BlockSpec((tm, D), lambda i, *_: (i, 0)),
                      pl.BlockSpec((1, 2 * TOP_K, tm), lambda i, *_: (i, 0, 0))],
            out_specs=pl.BlockSpec(memory_space=pl.ANY),
            scratch_shapes=[pltpu.VMEM((2, _slots(tm, ne), D), BF16),
                            pltpu.VMEM((CHUNK, D), BF16),
                            pltpu.SemaphoreType.DMA((2,)), pltpu.SemaphoreType.DMA(())]),
        out_shape=jax.ShapeDtypeStruct((nb * MOE_BLOCK, D), BF16),
        compiler_params=_cparams(("arbitrary",)),
        name="dispatch",
    )(nchunks, dest, tail_start, tail_n, h, posg)


def _experts_kernel(be_ref, nu_ref, xs_ref, wgu_ref, bgu_ref, wd_ref, bd_ref, ys_ref,
                    wgu_s, wd_s):
    b = pl.program_id(0)

    @pl.when(b < nu_ref[0])
    def _():
        @pl.when((b == 0) | (be_ref[b] != be_ref[jnp.maximum(b - 1, 0)]))
        def _():
            wgu_s[...] = wgu_ref[0].astype(BF16)
            wd_s[...] = wd_ref[0].astype(BF16)

        ff = wd_s.shape[0]
        gu = jnp.dot(xs_ref[...], wgu_s[...], preferred_element_type=F32) + bgu_ref[0]
        g = jnp.minimum(gu[:, :ff], SWIGLU_LIMIT)
        u = jnp.clip(gu[:, ff:], -SWIGLU_LIMIT, SWIGLU_LIMIT)
        act = (u + 1.0) * g * jax.nn.sigmoid(SWIGLU_ALPHA * g)
        ys = jnp.dot(act.astype(BF16), wd_s[...], preferred_element_type=F32) + bd_ref[0]
        ys_ref[...] = ys.astype(ys_ref.dtype)


def _experts(xs, blk_e, nused, w_gate_up, b_gate_up, w_down, b_down):
    rows, D = xs.shape
    nb = rows // MOE_BLOCK
    ne, _, ff2 = w_gate_up.shape
    ff = w_down.shape[1]
    blk = lambda b, be, nu: (jnp.minimum(b, nu[0] - 1), 0)
    exp3 = lambda b, be, nu: (be[b], 0, 0)
    return pl.pallas_call(
        _experts_kernel,
        grid_spec=pltpu.PrefetchScalarGridSpec(
            num_scalar_prefetch=2, grid=(nb,),
            in_specs=[pl.BlockSpec((MOE_BLOCK, D), blk),
                      pl.BlockSpec((1, D, ff2), exp3), pl.BlockSpec((1, 1, ff2), exp3),
                      pl.BlockSpec((1, ff, D), exp3), pl.BlockSpec((1, 1, D), exp3)],
            out_specs=pl.BlockSpec((MOE_BLOCK, D), blk),
            scratch_shapes=[pltpu.VMEM((D, ff2), BF16), pltpu.VMEM((ff, D), BF16)]),
        out_shape=jax.ShapeDtypeStruct((rows, D), BF16),
        compiler_params=_cparams(("arbitrary",)),
        name="experts",
    )(blk_e, nused, xs, w_gate_up, b_gate_up[:, None, :], w_down, b_down[:, None, :])


def _combine_kernel(n_ref, dest_ref, ys_hbm, posg_ref, x1_ref, y_ref, ylast_ref, buf, sem):
    i = pl.program_id(0)
    last = pl.num_programs(0) - 1
    tm = x1_ref.shape[0]
    nslots = buf.shape[1]

    def run_copy(tile, c):
        s = tile % 2
        return pltpu.make_async_copy(ys_hbm.at[_chunk_rows(dest_ref[tile, c])],
                                     buf.at[s, _chunk_rows(c)], sem.at[s])

    def fetch(tile):
        _for_each_chunk(n_ref[tile], lambda c: run_copy(tile, c).start())
        s = tile % 2

        def zero_chunk(c, carry):
            buf[s, _chunk_rows(c), :] = jnp.zeros((CHUNK, buf.shape[2]), BF16)
            return carry
        lax.fori_loop(n_ref[tile], nslots // CHUNK, zero_chunk, 0)

    @pl.when(i == 0)
    def _():
        fetch(i)

    @pl.when(i < last)
    def _():
        fetch(i + 1)

    r0 = lax.broadcasted_iota(I32, (tm, tm), 0)
    r1 = lax.broadcasted_iota(I32, (tm, tm), 1)
    posg_t = _dot_nt(jnp.where(r0 == r1, 1.0, 0.0), posg_ref[0], exact=True)
    slot = lax.broadcasted_iota(I32, (tm, nslots), 1)
    w = jnp.zeros(slot.shape, F32)
    for k in range(TOP_K):
        w = w + jnp.where(slot == posg_t[:, k:k + 1].astype(I32),
                          posg_t[:, TOP_K + k:TOP_K + k + 1], 0.0)

    _for_each_chunk(n_ref[i], lambda c: run_copy(i, c).wait())
    y = x1_ref[...] + jnp.dot(w.astype(BF16), buf[i % 2], preferred_element_type=F32)

    @pl.when(i < last)
    def _():
        y_ref[...] = y

    @pl.when(i == last)
    def _():
        ylast_ref[...] = y


def _combine(ys, posg, x1, nchunks, dest):
    T, D = x1.shape
    nt, _, tm = posg.shape
    return pl.pallas_call(
        _combine_kernel,
        grid_spec=pltpu.PrefetchScalarGridSpec(
            num_scalar_prefetch=2, grid=(nt,),
            in_specs=[pl.BlockSpec(memory_space=pl.ANY),
                      pl.BlockSpec((1, 2 * TOP_K, tm), lambda i, *_: (i, 0, 0)),
                      pl.BlockSpec((tm, D), lambda i, *_: (i, 0))],
            out_specs=[pl.BlockSpec((tm, D), lambda i, *_: (jnp.minimum(i, nt - 2), 0)),
                       pl.BlockSpec((tm, D), lambda i, *_: (0, 0))],
            scratch_shapes=[pltpu.VMEM((2, dest.shape[1] * CHUNK, D), BF16),
                            pltpu.SemaphoreType.DMA((2,))]),
        out_shape=[jax.ShapeDtypeStruct((T - tm, D), F32), jax.ShapeDtypeStruct((tm, D), F32)],
        compiler_params=_cparams(("arbitrary",)),
        name="combine",
    )(nchunks, dest, ys, posg, x1)


def _moe(h, posg, cnt, x1, w_gate_up, b_gate_up, w_down, b_down):
    nt, _, tm = posg.shape
    ne = cnt.shape[1]
    max_chunks = (nt * tm * TOP_K) // CHUNK + nt * ne
    nb = -(-max_chunks // CHUNKS_PER_BLOCK) + ne
    nchunks, dest, tail_start, tail_n, nused, blk_e = _moe_tables(cnt, nb, _slots(tm, ne) // CHUNK)
    xs = _dispatch(h, posg, nchunks, dest, tail_start, tail_n, nb=nb)
    ys = _experts(xs, blk_e, nused, w_gate_up, b_gate_up, w_down, b_down)
    return _combine(ys, posg, x1, nchunks, dest)


def kernel(x_prompt, x_sample, cache_k, cache_v, state_conv, page_table, norm1_w, w_in,
           q_norm_w, k_norm_w, lambda_q1, lambda_k1, lambda_q2, lambda_k2, subln_w,
           conv_w, conv_b, conv_ln_g, conv_ln_b, w_out, norm2_w, router_w, router_b,
           w_gate_up, b_gate_up, w_down, b_down):
    B, S, D = x_prompt.shape
    Bs, Ss, _ = x_sample.shape
    depth = norm1_w.shape[0]
    n_phys, page, heads, _, qk = cache_k.shape[1:]
    vdim = cache_v.shape[-1]
    qc, vc, cc = heads * 2 * qk, heads * vdim, conv_w.shape[2]
    taps = conv_w.shape[1]
    assert Ss == 1 and qk == QK_GROUP and vdim == V7X_LANES and (B * S) % MOE_TILE == 0
    assert Bs <= MOE_TILE and S >= taps - 1
    n_past = page_table.shape[1] * page
    T = B * S
    nt_p = T // MOE_TILE
    pos_p = jnp.arange(S, dtype=F32)
    pos_s = jnp.full((Bs,), n_past, F32)
    xp = x_prompt.reshape(T, D)
    xs = x_sample.reshape(Bs, D)
    pad_tile = lambda a: jnp.pad(a, ((0, MOE_TILE - Bs), (0, 0)))
    outs = [[] for _ in range(6)]
    for l in range(depth):
        lam_init = 0.8 - 0.6 * math.exp(-0.3 * l)
        lamv = jnp.stack([lambda_q1[l], lambda_k1[l], lambda_q2[l], lambda_k2[l]])
        conv_p = (conv_w[l], conv_b[l], conv_ln_g[l], conv_ln_b[l])
        proj_p = (norm1_w[l], w_in[l], q_norm_w[l], k_norm_w[l])
        tail_p = (w_out[l], norm2_w[l], router_w[l], router_b[l])

        q, kt, v, u, kbt, vb = _proj(xp, pos_p, S // PROJ_TILE, PROJ_TILE, *proj_p,
                                     qc=qc, vc=vc, cc=cc, exact_norm=False, attn_layout=True)
        attn = _attn_prompt(q, kbt, vb, lamv, subln_w[l], lam_init=lam_init)
        conv = _conv_prompt(u, *conv_p, batch=B, seq=S)
        bufs = _tail(attn, conv, xp, *tail_p, tm=MOE_TILE, n_valid=MOE_TILE,
                     total_tiles=nt_p + 1)
        outs[0].append(jnp.transpose(kt.reshape(B, heads, 2, qk, S), (0, 4, 1, 2, 3)))
        outs[1].append(v.reshape(B, S, heads, vdim))
        outs[2].append(u.reshape(B, S, cc)[:, S - (taps - 1):])

        qs, ks_, vs_, us = _proj(xs, pos_s, 1, Bs, *proj_p, qc=qc, vc=vc, cc=cc,
                                 exact_norm=True, attn_layout=False)
        cache_kt = jnp.transpose(cache_k[l], (0, 2, 3, 4, 1)).reshape(n_phys, qc, page)
        cache_vr = cache_v[l].reshape(n_phys, page * heads, vdim)
        attn_s = _attn_decode(qs, ks_, vs_, cache_kt, cache_vr, page_table, lamv, subln_w[l],
                              heads=heads, lam_init=lam_init)
        conv_s = _conv_step(state_conv[l], us, *conv_p)
        x1, h, posg, cnt = _tail(pad_tile(attn_s), pad_tile(conv_s), pad_tile(xs), *tail_p,
                                 tm=MOE_TILE, n_valid=Bs, total_tiles=nt_p + 1,
                                 first_tile=nt_p, into=bufs)
        outs[3].append(ks_.reshape(Bs, Ss, heads, 2, qk))
        outs[4].append(vs_.reshape(Bs, Ss, heads, vdim))
        outs[5].append(jnp.concatenate([state_conv[l][:, Ss:], us[:, None, :]], axis=1))

        xp, y_last = _moe(h, posg, cnt[:, :, 0], x1, w_gate_up[l], b_gate_up[l], w_down[l],
                          b_down[l])
        xs = y_last[:Bs]
    return (xp.reshape(B, S, D), xs.reshape(Bs, Ss, D)) + tuple(jnp.stack(o) for o in outs)
```

```python
import functools
import math

import jax
import jax.numpy as jnp
from jax import lax
from jax.experimental import pallas as pl
from jax.experimental.pallas import tpu as pltpu

F32 = jnp.float32
BF16 = jnp.bfloat16
I32 = jnp.int32
HIGHEST = lax.Precision.HIGHEST

EPS = 1e-6
ROPE_THETA = 10000.0
SWIGLU_LIMIT = 7.0
SWIGLU_ALPHA = 1.702
TOP_K = 4
NEG = -1e30
QK_GROUP = 64

V7X_LANES = 128
V7X_SUBLANES = 8
VMEM_LIMIT = 56 * 1024 * 1024
BF16_ROWS = 16

MOE_TILE = 256
CHUNK = BF16_ROWS
MOE_BLOCK = 512
CHUNKS_PER_BLOCK = MOE_BLOCK // CHUNK
PROJ_TILE = 512


def _cparams(sem, vmem=VMEM_LIMIT):
    return pltpu.CompilerParams(dimension_semantics=sem, vmem_limit_bytes=vmem)


def _dot(a, b, exact=False):
    if exact:
        return jnp.dot(a.astype(F32), b.astype(F32), precision=HIGHEST,
                       preferred_element_type=F32)
    return jnp.dot(a.astype(BF16), b.astype(BF16), preferred_element_type=F32)


def _dot_nt(a, b, exact=False):
    dn = (((1,), (1,)), ((), ()))
    if exact:
        return lax.dot_general(a.astype(F32), b.astype(F32), dn, precision=HIGHEST,
                               preferred_element_type=F32)
    return lax.dot_general(a.astype(BF16), b.astype(BF16), dn, preferred_element_type=F32)


def _bf16_round(x):
    return x.astype(BF16).astype(F32)


def _rope_norm(p, gsum, w, cos, sin, first_half, exact_norm):
    ss = _dot(p * p, gsum, exact_norm)
    n = p * lax.rsqrt(ss * (1.0 / QK_GROUP) + EPS) * w
    outs = []
    for j in range(p.shape[1] // V7X_LANES):
        nj = n[:, j * V7X_LANES:(j + 1) * V7X_LANES]
        rot = jnp.where(first_half, pltpu.roll(nj, V7X_LANES - QK_GROUP // 2, 1),
                        pltpu.roll(nj, QK_GROUP // 2, 1))
        outs.append(nj * cos + rot * sin)
    return jnp.concatenate(outs, axis=1)


def _proj_kernel(x_ref, n1_ref, w_ref, qw_ref, kw_ref, cos_ref, sin_ref, gsum_ref,
                 q_ref, k_ref, v_ref, u_ref, *rest, qc, vc, cc, scale, exact_norm):
    x = x_ref[...]
    h = x * lax.rsqrt(jnp.mean(x * x, axis=-1, keepdims=True) + EPS) * n1_ref[...]
    hm = h.astype(BF16)
    cos = cos_ref[...]
    sin = sin_ref[...]
    lane = lax.broadcasted_iota(I32, cos.shape, 1)
    first_half = (lane % QK_GROUP) < QK_GROUP // 2
    gsum = gsum_ref[...]

    q = _rope_norm(_dot(hm, w_ref[:, 0:qc]), gsum, qw_ref[...], cos, sin, first_half, exact_norm)
    q_ref[...] = (q * scale).astype(q_ref.dtype)
    k = _rope_norm(_dot(hm, w_ref[:, qc:2 * qc]), gsum, kw_ref[...], cos, sin, first_half,
                   exact_norm)
    v = _dot(hm, w_ref[:, 2 * qc:2 * qc + vc])
    v_ref[...] = v
    o = 2 * qc + vc
    ua = _dot(hm, w_ref[:, o:o + cc])
    ub = _dot(hm, w_ref[:, o + cc:o + 2 * cc])
    u_ref[...] = ua * jax.nn.sigmoid(ub)
    if rest:
        kb_ref, vb_ref = rest
        kt = k.T
        k_ref[0] = kt
        kb_ref[0, :, 0] = kt.astype(BF16).reshape(kb_ref.shape[1], kb_ref.shape[3], kt.shape[1])
        vb_ref[...] = v.astype(BF16)
    else:
        k_ref[...] = k


def _rope_tables(pos):
    half = QK_GROUP // 2
    inv = jnp.power(ROPE_THETA, -jnp.arange(half, dtype=F32) / half)
    ang = pos[:, None] * inv[None, :]
    reps = V7X_LANES // QK_GROUP
    cos = jnp.tile(jnp.cos(ang), (1, 2 * reps))
    s = jnp.sin(ang)
    sin = jnp.tile(jnp.concatenate([-s, s], axis=1), (1, reps))
    return cos, sin


def _proj(x2d, pos_rows, n_pos_blocks, tm, norm1_w, w_in, q_norm_w, k_norm_w, *, qc, vc, cc,
          exact_norm, attn_layout):
    T, D = x2d.shape
    cos, sin = _rope_tables(pos_rows)
    gi = jnp.arange(qc) // QK_GROUP
    gsum = (gi[:, None] == gi[None, :]).astype(F32 if exact_norm else BF16)
    qw = jnp.tile(q_norm_w, qc // QK_GROUP)[None, :]
    kw = jnp.tile(k_norm_w, qc // QK_GROUP)[None, :]
    w = w_in.astype(BF16)
    row = lambda i: (i, 0)
    full = lambda i: (0, 0)
    out_shape = [jax.ShapeDtypeStruct((T, qc), BF16),
                 jax.ShapeDtypeStruct((T, qc), F32),
                 jax.ShapeDtypeStruct((T, vc), F32),
                 jax.ShapeDtypeStruct((T, cc), F32)]
    out_specs = [pl.BlockSpec((tm, qc), row), pl.BlockSpec((tm, qc), row),
                 pl.BlockSpec((tm, vc), row), pl.BlockSpec((tm, cc), row)]
    if attn_layout:
        nseq = T // (n_pos_blocks * tm)
        heads = qc // (2 * QK_GROUP)
        seq_tile = lambda i: (i // n_pos_blocks, 0, i % n_pos_blocks)
        out_shape[1] = jax.ShapeDtypeStruct((nseq, qc, n_pos_blocks * tm), F32)
        out_specs[1] = pl.BlockSpec((1, qc, tm), seq_tile)
        out_shape += [jax.ShapeDtypeStruct((nseq, heads, n_pos_blocks, 2 * QK_GROUP, tm), BF16),
                      jax.ShapeDtypeStruct((T, vc), BF16)]
        out_specs += [pl.BlockSpec((1, heads, 1, 2 * QK_GROUP, tm),
                                   lambda i: (i // n_pos_blocks, 0, i % n_pos_blocks, 0, 0)),
                      pl.BlockSpec((tm, vc), row)]
    return pl.pallas_call(
        functools.partial(_proj_kernel, qc=qc, vc=vc, cc=cc, scale=QK_GROUP ** -0.5,
                          exact_norm=exact_norm),
        grid=(T // tm,),
        in_specs=[pl.BlockSpec((tm, D), row),
                  pl.BlockSpec((1, D), full),
                  pl.BlockSpec(w.shape, full),
                  pl.BlockSpec((1, qc), full),
                  pl.BlockSpec((1, qc), full),
                  pl.BlockSpec((tm, V7X_LANES), lambda i: (i % n_pos_blocks, 0)),
                  pl.BlockSpec((tm, V7X_LANES), lambda i: (i % n_pos_blocks, 0)),
                  pl.BlockSpec((qc, qc), full)],
        out_specs=out_specs,
        out_shape=out_shape,
        compiler_params=_cparams(("arbitrary",)),
        name="proj" if attn_layout else "proj_step",
    )(x2d, norm1_w[None, :], w, qw, kw, cos, sin, gsum)


def _lambda_value(lv, lam_init):
    a = jnp.sum(lv[0:1] * lv[1:2], axis=-1, keepdims=True)
    b = jnp.sum(lv[2:3] * lv[3:4], axis=-1, keepdims=True)
    return jnp.exp(a) - jnp.exp(b) + lam_init


def _subln(o, w, lam_init):
    y = o * lax.rsqrt(jnp.mean(o * o, axis=-1, keepdims=True) + EPS)
    return y * w * (1.0 - lam_init)


def _attn_body(i, lam_ref, sw_ref, q_ref, k_ref, v_ref, o_ref, *, tq, lam_init):
    lam = _lambda_value(lam_ref[...], lam_init)
    q = q_ref[...]
    lane = lax.broadcasted_iota(I32, q.shape, 1)
    zero = jnp.zeros_like(q)
    qs = (jnp.where(lane < QK_GROUP, q, zero), jnp.where(lane >= QK_GROUP, q, zero))

    def chunk(j, carry, masked):
        kc = k_ref[0, 0, j]
        vc = v_ref[pl.ds(pl.multiple_of(j * tq, tq), tq), :]
        out = []
        for c in range(2):
            m, l, acc = carry[c]
            s = jnp.dot(qs[c], kc, preferred_element_type=F32)
            if masked:
                row = lax.broadcasted_iota(I32, s.shape, 0)
                col = lax.broadcasted_iota(I32, s.shape, 1)
                s = jnp.where(col <= row, s, NEG)
            m_new = jnp.maximum(m, jnp.max(s, axis=-1, keepdims=True))
            p = jnp.exp(s - m_new)
            alpha = jnp.exp(m - m_new)
            l = alpha * l + jnp.sum(p, axis=-1, keepdims=True)
            acc = alpha * acc + _dot(p, vc)
            out.append((m_new, l, acc))
        return tuple(out)

    init = tuple((jnp.full((tq, 1), NEG, F32), jnp.zeros((tq, 1), F32),
                  jnp.zeros((tq, V7X_LANES), F32)) for _ in range(2))
    carry = lax.fori_loop(0, i, lambda j, c: chunk(j, c, False), init)
    (_, l0, a0), (_, l1, a1) = chunk(i, carry, True)
    o = a0 / l0 - lam * (a1 / l1)
    o_ref[...] = _subln(o, sw_ref[...], lam_init).astype(o_ref.dtype)


DECODE_PAGES_PER_STEP = 32
SOFTMAX_PAGES = 16


def _decode_body(s, half_steps, lam_ref, sw_ref, qm_ref, kn_ref, vn_ref, k_refs, v_refs,
                 o_ref, s_ref, m_ref, coef_ref, acc_ref, *, page, heads, lam_init):
    pps = len(k_refs)
    rows = s_ref.shape[1]
    qm = qm_ref[0]
    row = lax.broadcasted_iota(I32, (rows, V7X_LANES), 0)
    n_pages = half_steps * pps

    @pl.when(s == 0)
    def _():
        m_ref[...] = jnp.full(m_ref.shape, NEG, F32)

    @pl.when(s < half_steps)
    def _():
        m = m_ref[...]
        for j in range(pps):
            sc = jnp.dot(qm, k_refs[j][0].astype(BF16), preferred_element_type=F32)
            s_ref[s * pps + j] = sc
            m = jnp.maximum(m, sc)
        m_ref[...] = m

    def head_weights(p):
        a = p * coef_ref[...]
        return _bf16_round(a + pltpu.roll(a, rows - heads, 0))

    @pl.when(s == half_steps)
    def _():
        s_new = jnp.sum(qm.astype(F32) * _bf16_round(kn_ref[0]), axis=-1, keepdims=True)
        m = jnp.maximum(jnp.max(m_ref[...], axis=-1, keepdims=True), s_new)

        def exp_pages(t, l):
            pages = pl.ds(pl.multiple_of(t * SOFTMAX_PAGES, SOFTMAX_PAGES), SOFTMAX_PAGES)
            p = jnp.exp(s_ref[pages] - m[None])
            s_ref[pages] = p
            return l + jnp.sum(p, axis=0)

        lsum = lax.fori_loop(0, n_pages // SOFTMAX_PAGES, exp_pages,
                             jnp.zeros((rows, V7X_LANES), F32))
        p_new = jnp.exp(s_new - m)
        l = jnp.sum(lsum, axis=-1, keepdims=True) + p_new
        lam = _lambda_value(lam_ref[...], lam_init)
        coef = jnp.where(row[:, 0:1] < heads, 1.0, -lam) / l
        coef_ref[...] = jnp.broadcast_to(coef, coef_ref.shape)
        acc_ref[...] = (head_weights(jnp.broadcast_to(p_new, (rows, V7X_LANES)))
                        * _bf16_round(vn_ref[0]))

    @pl.when(s >= half_steps)
    def _():
        lane = lax.broadcasted_iota(I32, (rows, V7X_LANES), 1)
        keep = (lane % heads == row) & (row < heads)
        acc = acc_ref[...]
        for j in range(pps):
            a = head_weights(s_ref[(s - half_steps) * pps + j])
            parts = []
            for c in range(heads):
                idx = (c * page + lane) // heads
                parts.append(jnp.where(keep, jnp.take_along_axis(a, idx, axis=1), 0.0))
            a_exp = jnp.concatenate(parts, axis=1).astype(BF16)
            acc = acc + jnp.dot(a_exp, v_refs[j][0].astype(BF16), preferred_element_type=F32)
        acc_ref[...] = acc

    @pl.when(s == 2 * half_steps - 1)
    def _():
        o_ref[0] = _subln(acc_ref[...], sw_ref[...], lam_init)


def _attention_kernel(pt_ref, lam_ref, sw_ref, q_ref, k_ref, v_ref, qm_ref, kn_ref, vn_ref, *rest,
                      pps, ratio, nq, steps_per_seq, tq, page, heads, lam_init):
    k_pages = rest[:pps]
    v_pages = rest[pps:2 * pps]
    o_ref, od_ref, s_ref, m_ref, coef_ref, acc_ref = rest[2 * pps:]
    t = pl.program_id(0)
    _decode_body(t % steps_per_seq, steps_per_seq // 2, lam_ref, sw_ref, qm_ref, kn_ref, vn_ref,
                 k_pages, v_pages, od_ref, s_ref, m_ref, coef_ref, acc_ref,
                 page=page, heads=heads, lam_init=lam_init)

    @pl.when(t % ratio == 0)
    def _():
        _attn_body((t // ratio) % nq, lam_ref, sw_ref, q_ref, k_ref, v_ref, o_ref,
                   tq=tq, lam_init=lam_init)


def _attention(q, kbt, vb, q_step, k_new, v_new, cache_kt, cache_vr, page_table, lamv, subln_w,
               *, lam_init):
    batch, heads, nq, _, tq = kbt.shape
    seq = nq * tq
    Bs, D = q_step.shape
    n_pages = page_table.shape[1]
    page = cache_kt.shape[2]
    vdim = cache_vr.shape[2]
    pps = DECODE_PAGES_PER_STEP
    half_steps = n_pages // pps
    steps_per_seq = 2 * half_steps
    rows = 2 * heads
    n_dec, n_att = Bs * steps_per_seq, batch * heads * nq
    assert rows == V7X_SUBLANES and page == V7X_LANES and vdim == V7X_LANES
    assert n_pages % pps == 0 and n_pages % SOFTMAX_PAGES == 0 and n_dec % n_att == 0
    ratio = n_dec // n_att
    group = jnp.arange(D) // QK_GROUP
    rowmask = ((group % 2) * heads + group // 2)[None, :] == jnp.arange(rows)[:, None]
    qm = jnp.where(rowmask[None], q_step[:, None, :], jnp.zeros((), BF16))
    vn = jnp.pad(v_new.reshape(Bs, heads, vdim), ((0, 0), (0, rows - heads), (0, 0)))

    def k_map(j):
        return lambda t, pt: (pt[t // steps_per_seq,
                                 jnp.minimum(t % steps_per_seq, half_steps - 1) * pps + j], 0, 0)

    def v_map(j):
        return lambda t, pt: (pt[t // steps_per_seq,
                                 jnp.maximum(t % steps_per_seq - half_steps, 0) * pps + j], 0, 0)

    def att(t):
        a = t // ratio
        return a // (heads * nq), (a // nq) % heads, a % nq

    def q_map(t, pt):
        b, h, i = att(t)
        return b * nq + i, h

    seq3 = lambda t, pt: (t // steps_per_seq, 0, 0)
    const = lambda t, pt: (0, 0)
    out, out_step = pl.pallas_call(
        functools.partial(_attention_kernel, pps=pps, ratio=ratio, nq=nq,
                          steps_per_seq=steps_per_seq, tq=tq, page=page, heads=heads,
                          lam_init=lam_init),
        grid_spec=pltpu.PrefetchScalarGridSpec(
            num_scalar_prefetch=1, grid=(n_dec,),
            in_specs=[pl.BlockSpec(lamv.shape, const), pl.BlockSpec((1, vdim), const),
                      pl.BlockSpec((tq, V7X_LANES), q_map),
                      pl.BlockSpec((1, 1, nq, V7X_LANES, tq),
                                   lambda t, pt: att(t)[:2] + (0, 0, 0)),
                      pl.BlockSpec((seq, V7X_LANES), lambda t, pt: att(t)[:2]),
                      pl.BlockSpec((1, rows, D), seq3), pl.BlockSpec((1, 1, D), seq3),
                      pl.BlockSpec((1, rows, vdim), seq3)]
                     + [pl.BlockSpec((1, D, page), k_map(j)) for j in range(pps)]
                     + [pl.BlockSpec((1, page * heads, vdim), v_map(j)) for j in range(pps)],
            out_specs=[pl.BlockSpec((tq, V7X_LANES), q_map),
                       pl.BlockSpec((1, rows, vdim), seq3)],
            scratch_shapes=[pltpu.VMEM((n_pages, rows, page), F32),
                            pltpu.VMEM((rows, V7X_LANES), F32),
                            pltpu.VMEM((rows, V7X_LANES), F32),
                            pltpu.VMEM((rows, vdim), F32)]),
        out_shape=[jax.ShapeDtypeStruct((batch * seq, heads * V7X_LANES), BF16),
                   jax.ShapeDtypeStruct((Bs, rows, vdim), F32)],
        compiler_params=_cparams(("arbitrary",)),
        name="attention",
    )(page_table, lamv, subln_w[None, :], q, kbt, vb, qm, k_new[:, None, :], vn,
      *([cache_kt] * pps), *([cache_vr] * pps))
    return out, out_step[:, :heads, :].reshape(Bs, heads * vdim)


CONV_HALO = 32


def _ln_swish(y, b_ref, g_ref, be_ref):
    y = y + b_ref[...]
    mu = jnp.mean(y, axis=-1, keepdims=True)
    yc = y - mu
    z = yc * lax.rsqrt(jnp.mean(yc * yc, axis=-1, keepdims=True) + EPS) * g_ref[...] + be_ref[...]
    return z * jax.nn.sigmoid(z)


def _conv_kernel(u_ref, w_ref, b_ref, g_ref, be_ref, o_ref, buf_ref, part_ref, *, tc, taps):
    sub = V7X_SUBLANES
    first = CONV_HALO - (taps - 1)

    @pl.when(pl.program_id(1) == 0)
    def _():
        buf_ref[0:CONV_HALO, :] = jnp.zeros((CONV_HALO, buf_ref.shape[1]), F32)
        buf_ref[CONV_HALO + tc:, :] = jnp.zeros((sub, buf_ref.shape[1]), F32)

    buf_ref[CONV_HALO:CONV_HALO + tc, :] = _bf16_round(u_ref[...])
    acc = None
    for r in range(sub):
        part = None
        for a in range(-(-(first + taps) // sub)):
            k = sub * a + r - first
            if 0 <= k < taps:
                term = buf_ref[sub * a:sub * a + tc + sub, :] * _bf16_round(w_ref[k:k + 1, :])
                part = term if part is None else part + term
        if r == 0:
            acc = part[0:tc]
        else:
            part_ref[...] = part
            acc = acc + part_ref[r:r + tc, :]
    o_ref[...] = _ln_swish(acc, b_ref, g_ref, be_ref).astype(o_ref.dtype)
    buf_ref[0:CONV_HALO, :] = buf_ref[tc:tc + CONV_HALO, :]


def _conv_prompt(u2d, conv_w, conv_b, ln_g, ln_b, *, batch, seq, tc=512):
    taps, C = conv_w.shape
    nt = seq // tc
    vec = lambda b, i: (0, 0)
    return pl.pallas_call(
        functools.partial(_conv_kernel, tc=tc, taps=taps),
        grid=(batch, nt),
        in_specs=[pl.BlockSpec((tc, C), lambda b, i: (b * nt + i, 0)),
                  pl.BlockSpec((taps, C), vec), pl.BlockSpec((1, C), vec),
                  pl.BlockSpec((1, C), vec), pl.BlockSpec((1, C), vec)],
        out_specs=pl.BlockSpec((tc, C), lambda b, i: (b * nt + i, 0)),
        out_shape=jax.ShapeDtypeStruct((batch * seq, C), BF16),
        scratch_shapes=[pltpu.VMEM((tc + CONV_HALO + V7X_SUBLANES, C), F32),
                        pltpu.VMEM((tc + V7X_SUBLANES, C), F32)],
        compiler_params=_cparams(("arbitrary", "arbitrary")),
        name="conv",
    )(u2d, conv_w, conv_b[None, :], ln_g[None, :], ln_b[None, :])


def _conv_step_kernel(st_ref, u_ref, w_ref, b_ref, g_ref, be_ref, o_ref, *, taps):
    acc = u_ref[...] * w_ref[taps - 1:taps, :]
    for k in range(taps - 1):
        acc = acc + st_ref[:, k, :] * w_ref[k:k + 1, :]
    o_ref[...] = _ln_swish(acc, b_ref, g_ref, be_ref)


def _conv_step(state, u, conv_w, conv_b, ln_g, ln_b):
    taps, C = conv_w.shape
    return pl.pallas_call(
        functools.partial(_conv_step_kernel, taps=taps),
        out_shape=jax.ShapeDtypeStruct(u.shape, F32),
        compiler_params=_cparams(None),
        name="conv_step",
    )(state, u, conv_w, conv_b[None, :], ln_g[None, :], ln_b[None, :])


def _tail_kernel(a_ref, c_ref, x_ref, wo_ref, n2_ref, rw_ref, rb_ref, *rest, n_valid):
    x1_ref, h_ref, posg_ref, cnt_ref = rest[-4:]
    tm = x_ref.shape[0]
    half = a_ref.shape[1]
    x1 = (x_ref[...] + _dot(a_ref[...], wo_ref[0:half, :])
          + _dot(c_ref[...], wo_ref[half:, :]))
    x1_ref[...] = x1
    h = (x1 * lax.rsqrt(jnp.mean(x1 * x1, axis=-1, keepdims=True) + EPS)
         * n2_ref[...]).astype(BF16)
    h_ref[...] = h
    logits = _dot_nt(rw_ref[...], h) + rb_ref[...]
    ne = logits.shape[0]
    eidx = lax.broadcasted_iota(I32, logits.shape, 0)
    valid = lax.broadcasted_iota(I32, (1, tm), 1) < n_valid

    sels, vals = [], []
    l = logits
    for _ in range(TOP_K):
        m = jnp.max(l, axis=0, keepdims=True)
        first = jnp.min(jnp.where(l == m, eidx, ne), axis=0, keepdims=True)
        sel = (eidx == first) & valid
        l = jnp.where(eidx == first, -jnp.inf, l)
        sels.append(sel)
        vals.append(m)
    ex = [jnp.exp(v - vals[0]) for v in vals]
    den = ex[0] + ex[1] + ex[2] + ex[3]
    gates = [jnp.where(valid, e / den, 0.0) for e in ex]

    msel = jnp.zeros(logits.shape, F32)
    for sel in sels:
        msel = msel + jnp.where(sel, 1.0, 0.0)
    r0 = lax.broadcasted_iota(I32, (tm, tm), 0)
    r1 = lax.broadcasted_iota(I32, (tm, tm), 1)
    upper = jnp.where(r0 < r1, 1.0, 0.0).astype(BF16)
    rank = jnp.dot(msel.astype(BF16), upper, preferred_element_type=F32)
    cnt = jnp.sum(msel, axis=1, keepdims=True)
    pcnt = jnp.ceil(cnt * (1.0 / CHUNK)) * CHUNK
    e0 = lax.broadcasted_iota(I32, (ne, ne), 0)
    e1 = lax.broadcasted_iota(I32, (ne, ne), 1)
    lower = jnp.where(e1 < e0, 1.0, 0.0)
    off = jnp.dot(lower.astype(BF16), jnp.broadcast_to(pcnt, (ne, V7X_LANES)).astype(BF16),
                  preferred_element_type=F32)[:, 0:1]
    pos = off + rank
    rows = [jnp.where(valid, jnp.sum(jnp.where(sel, pos, 0.0), axis=0, keepdims=True), -1.0)
            for sel in sels]
    posg_ref[0] = jnp.concatenate(rows + gates, axis=0)
    cnt_ref[0] = jnp.broadcast_to(cnt, (ne, V7X_LANES)).astype(I32)


def _tail(attn, conv, x2d, w_out, norm2_w, router_w, router_b, *, tm, n_valid, total_tiles,
          first_tile=0, into=None):
    T, D = x2d.shape
    half = attn.shape[1]
    ne = router_w.shape[1]
    nt = T // tm
    wo = w_out.astype(BF16)
    rw = router_w.T.astype(BF16)
    row = lambda i: (i, 0)
    full = lambda i: (0, 0)
    orow = lambda i: (first_tile + i, 0)
    otile = lambda i: (first_tile + i, 0, 0)
    extra = list(into) if into is not None else []
    return pl.pallas_call(
        functools.partial(_tail_kernel, n_valid=n_valid),
        grid=(nt,),
        in_specs=[pl.BlockSpec((tm, half), row), pl.BlockSpec((tm, half), row),
                  pl.BlockSpec((tm, D), row), pl.BlockSpec((D, D), full),
                  pl.BlockSpec((1, D), full), pl.BlockSpec((ne, D), full),
                  pl.BlockSpec((ne, 1), full)]
                 + [pl.BlockSpec(memory_space=pl.ANY)] * len(extra),
        out_specs=[pl.BlockSpec((tm, D), orow), pl.BlockSpec((tm, D), orow),
                   pl.BlockSpec((1, 2 * TOP_K, tm), otile),
                   pl.BlockSpec((1, ne, V7X_LANES), otile)],
        out_shape=[jax.ShapeDtypeStruct((total_tiles * tm, D), F32),
                   jax.ShapeDtypeStruct((total_tiles * tm, D), BF16),
                   jax.ShapeDtypeStruct((total_tiles, 2 * TOP_K, tm), F32),
                   jax.ShapeDtypeStruct((total_tiles, ne, V7X_LANES), I32)],
        input_output_aliases={7 + j: j for j in range(len(extra))},
        compiler_params=_cparams(("arbitrary",)),
        name="tail",
    )(attn, conv, x2d, wo, norm2_w[None, :], rw, router_b[:, None], *extra)


def _slots(tm, ne):
    worst = TOP_K * tm + ne * (CHUNK - 1)
    return -(-worst // V7X_LANES) * V7X_LANES


def _prefix_sum(x, axis, exclusive):
    n = x.shape[axis]
    i = jnp.arange(n)
    tri = (i[:, None] < i[None, :]) if exclusive else (i[:, None] <= i[None, :])
    xm = jnp.moveaxis(x, axis, -1)
    out = jnp.sum(xm[..., :, None] * tri.astype(x.dtype), axis=-2)
    return jnp.moveaxis(out, -1, axis)


def _moe_tables(cnt, nb, slot_chunks):
    nch = (cnt + (CHUNK - 1)) // CHUNK
    tot = jnp.sum(nch, axis=0)
    nblk = (tot + (CHUNKS_PER_BLOCK - 1)) // CHUNKS_PER_BLOCK
    bend = _prefix_sum(nblk, 0, exclusive=False)
    gstart = (bend - nblk) * CHUNKS_PER_BLOCK
    rs = gstart[None, :] + _prefix_sum(nch, 0, exclusive=True)
    tail_start = gstart + tot
    tail_n = nblk * CHUNKS_PER_BLOCK - tot
    nused = bend[-1:]
    blk = jnp.minimum(jnp.arange(nb, dtype=I32), nused - 1)
    blk_e = jnp.sum((bend[None, :] <= blk[:, None]).astype(I32), axis=1)
    cend = _prefix_sum(nch, 1, exclusive=False)
    c = jnp.arange(slot_chunks, dtype=I32)
    run = jnp.minimum(jnp.sum((cend[:, None, :] <= c[None, :, None]).astype(I32), axis=2),
                      nch.shape[1] - 1)
    shift = rs - (cend - nch)
    dest = c[None, :] + jnp.sum(jnp.where(run[:, :, None] == jnp.arange(nch.shape[1]),
                                          shift[:, None, :], 0), axis=2)
    i32 = lambda a: a.astype(I32)
    return (i32(cend[:, -1]), i32(dest), i32(tail_start), i32(tail_n), i32(nused), i32(blk_e))


def _chunk_rows(c):
    return pl.ds(pl.multiple_of(c * CHUNK, CHUNK), CHUNK)


def _for_each_chunk(n, fn):
    def body(c, carry):
        fn(c)
        return carry
    lax.fori_loop(0, n, body, 0)


WAIT_GROUP = 8


def _wait_chunks(n, copy_of_rows):
    _for_each_chunk(n // WAIT_GROUP, lambda c: copy_of_rows(WAIT_GROUP * CHUNK).wait())
    _for_each_chunk(n % WAIT_GROUP, lambda c: copy_of_rows(CHUNK).wait())


def _one_hot_rows(pos, nrows):
    r = lax.broadcasted_iota(I32, (nrows, pos.shape[1]), 0)
    p = jnp.zeros(r.shape, F32)
    for k in range(TOP_K):
        p = p + jnp.where(r == pos[k:k + 1], 1.0, 0.0)
    return p.astype(BF16)


def _dispatch_kernel(n_ref, dest_ref, ts_ref, tn_ref, h_ref, posg_ref, xs_hbm,
                     buf, zbuf, sem, zsem, *, ne):
    i = pl.program_id(0)
    slot = i % 2
    pos = posg_ref[0][0:TOP_K].astype(I32)
    buf[slot] = jnp.dot(_one_hot_rows(pos, buf.shape[1]), h_ref[...],
                        preferred_element_type=F32).astype(BF16)

    def run_copy(tile, c):
        s = tile % 2
        return pltpu.make_async_copy(buf.at[s, _chunk_rows(c)],
                                     xs_hbm.at[_chunk_rows(dest_ref[tile, c])], sem.at[s])

    def zero_copy(g):
        return pltpu.make_async_copy(zbuf, xs_hbm.at[_chunk_rows(g)], zsem)

    @pl.when(i == 0)
    def _():
        zbuf[...] = jnp.zeros(zbuf.shape, BF16)
        for phase in ("start", "wait"):
            def per_expert(e, carry):
                def body(j, c):
                    cp = zero_copy(ts_ref[e] + j)
                    cp.start() if phase == "start" else cp.wait()
                    return c
                return lax.fori_loop(0, tn_ref[e], body, carry)
            lax.fori_loop(0, ne, per_expert, 0)

    _for_each_chunk(n_ref[i], lambda c: run_copy(i, c).start())

    def wait_tile(tile):
        s = tile % 2
        _wait_chunks(n_ref[tile], lambda r: pltpu.make_async_copy(
            buf.at[s, 0:r], xs_hbm.at[0:r], sem.at[s]))

    @pl.when(i > 0)
    def _():
        wait_tile(i - 1)

    @pl.when(i == pl.num_programs(0) - 1)
    def _():
        wait_tile(i)


def _dispatch(h, posg, nchunks, dest, tail_start, tail_n, *, nb):
    T, D = h.shape
    nt, _, tm = posg.shape
    ne = tail_n.shape[0]
    return pl.pallas_call(
        functools.partial(_dispatch_kernel, ne=ne),
        grid_spec=pltpu.PrefetchScalarGridSpec(
            num_scalar_prefetch=4, grid=(nt,),
            in_specs=[pl.BlockSpec((tm, D), lambda i, *_: (i, 0)),
                      pl.BlockSpec((1, 2 * TOP_K, tm), lambda i, *_: (i, 0, 0))],
            out_specs=pl.BlockSpec(memory_space=pl.ANY),
            scratch_shapes=[pltpu.VMEM((2, _slots(tm, ne), D), BF16),
                            pltpu.VMEM((CHUNK, D), BF16),
                            pltpu.SemaphoreType.DMA((2,)), pltpu.SemaphoreType.DMA(())]),
        out_shape=jax.ShapeDtypeStruct((nb * MOE_BLOCK, D), BF16),
        compiler_params=_cparams(("arbitrary",)),
        name="dispatch",
    )(nchunks, dest, tail_start, tail_n, h, posg)


def _experts_kernel(be_ref, nu_ref, xs_ref, wgu_ref, bgu_ref, wd_ref, bd_ref, ys_ref,
                    wgu_s, wd_s):
    b = pl.program_id(0)

    @pl.when(b < nu_ref[0])
    def _():
        @pl.when((b == 0) | (be_ref[b] != be_ref[jnp.maximum(b - 1, 0)]))
        def _():
            wgu_s[...] = wgu_ref[0].astype(BF16)
            wd_s[...] = wd_ref[0].astype(BF16)

        ff = wd_s.shape[0]
        gu = jnp.dot(xs_ref[...], wgu_s[...], preferred_element_type=F32) + bgu_ref[0]
        g = jnp.minimum(gu[:, :ff], SWIGLU_LIMIT)
        u = jnp.clip(gu[:, ff:], -SWIGLU_LIMIT, SWIGLU_LIMIT)
        act = (u + 1.0) * g * jax.nn.sigmoid(SWIGLU_ALPHA * g)
        ys = jnp.dot(act.astype(BF16), wd_s[...], preferred_element_type=F32) + bd_ref[0]
        ys_ref[...] = ys.astype(ys_ref.dtype)


def _experts(xs, blk_e, nused, w_gate_up, b_gate_up, w_down, b_down):
    rows, D = xs.shape
    nb = rows // MOE_BLOCK
    ne, _, ff2 = w_gate_up.shape
    ff = w_down.shape[1]
    blk = lambda b, be, nu: (jnp.minimum(b, nu[0] - 1), 0)
    exp3 = lambda b, be, nu: (be[b], 0, 0)
    return pl.pallas_call(
        _experts_kernel,
        grid_spec=pltpu.PrefetchScalarGridSpec(
            num_scalar_prefetch=2, grid=(nb,),
            in_specs=[pl.BlockSpec((MOE_BLOCK, D), blk),
                      pl.BlockSpec((1, D, ff2), exp3), pl.BlockSpec((1, 1, ff2), exp3),
                      pl.BlockSpec((1, ff, D), exp3), pl.BlockSpec((1, 1, D), exp3)],
            out_specs=pl.BlockSpec((MOE_BLOCK, D), blk),
            scratch_shapes=[pltpu.VMEM((D, ff2), BF16), pltpu.VMEM((ff, D), BF16)]),
        out_shape=jax.ShapeDtypeStruct((rows, D), BF16),
        compiler_params=_cparams(("arbitrary",)),
        name="experts",
    )(blk_e, nused, xs, w_gate_up, b_gate_up[:, None, :], w_down, b_down[:, None, :])


def _combine_kernel(n_ref, dest_ref, ys_hbm, posg_ref, x1_ref, y_ref, ylast_ref, buf, sem):
    i = pl.program_id(0)
    last = pl.num_programs(0) - 1
    tm = x1_ref.shape[0]
    nslots = buf.shape[1]

    def run_copy(tile, c):
        s = tile % 2
        return pltpu.make_async_copy(ys_hbm.at[_chunk_rows(dest_ref[tile, c])],
                                     buf.at[s, _chunk_rows(c)], sem.at[s])

    def fetch(tile):
        _for_each_chunk(n_ref[tile], lambda c: run_copy(tile, c).start())
        s = tile % 2

        def zero_chunk(c, carry):
            buf[s, _chunk_rows(c), :] = jnp.zeros((CHUNK, buf.shape[2]), BF16)
            return carry
        lax.fori_loop(n_ref[tile], nslots // CHUNK, zero_chunk, 0)

    @pl.when(i == 0)
    def _():
        fetch(i)

    @pl.when(i < last)
    def _():
        fetch(i + 1)

    r0 = lax.broadcasted_iota(I32, (tm, tm), 0)
    r1 = lax.broadcasted_iota(I32, (tm, tm), 1)
    posg_t = _dot_nt(jnp.where(r0 == r1, 1.0, 0.0), posg_ref[0], exact=True)
    slot = lax.broadcasted_iota(I32, (tm, nslots), 1)
    w = jnp.zeros(slot.shape, F32)
    for k in range(TOP_K):
        w = w + jnp.where(slot == posg_t[:, k:k + 1].astype(I32),
                          posg_t[:, TOP_K + k:TOP_K + k + 1], 0.0)

    _wait_chunks(n_ref[i], lambda r: pltpu.make_async_copy(
        ys_hbm.at[0:r], buf.at[i % 2, 0:r], sem.at[i % 2]))
    y = x1_ref[...] + jnp.dot(w.astype(BF16), buf[i % 2], preferred_element_type=F32)

    @pl.when(i < last)
    def _():
        y_ref[...] = y

    @pl.when(i == last)
    def _():
        ylast_ref[...] = y


def _combine(ys, posg, x1, nchunks, dest):
    T, D = x1.shape
    nt, _, tm = posg.shape
    return pl.pallas_call(
        _combine_kernel,
        grid_spec=pltpu.PrefetchScalarGridSpec(
            num_scalar_prefetch=2, grid=(nt,),
            in_specs=[pl.BlockSpec(memory_space=pl.ANY),
                      pl.BlockSpec((1, 2 * TOP_K, tm), lambda i, *_: (i, 0, 0)),
                      pl.BlockSpec((tm, D), lambda i, *_: (i, 0))],
            out_specs=[pl.BlockSpec((tm, D), lambda i, *_: (jnp.minimum(i, nt - 2), 0)),
                       pl.BlockSpec((tm, D), lambda i, *_: (0, 0))],
            scratch_shapes=[pltpu.VMEM((2, dest.shape[1] * CHUNK, D), BF16),
                            pltpu.SemaphoreType.DMA((2,))]),
        out_shape=[jax.ShapeDtypeStruct((T - tm, D), F32), jax.ShapeDtypeStruct((tm, D), F32)],
        compiler_params=_cparams(("arbitrary",)),
        name="combine",
    )(nchunks, dest, ys, posg, x1)


def _moe(h, posg, cnt, x1, w_gate_up, b_gate_up, w_down, b_down):
    nt, _, tm = posg.shape
    ne = cnt.shape[1]
    max_chunks = (nt * tm * TOP_K) // CHUNK + nt * ne
    nb = -(-max_chunks // CHUNKS_PER_BLOCK) + ne
    nchunks, dest, tail_start, tail_n, nused, blk_e = _moe_tables(cnt, nb, _slots(tm, ne) // CHUNK)
    xs = _dispatch(h, posg, nchunks, dest, tail_start, tail_n, nb=nb)
    ys = _experts(xs, blk_e, nused, w_gate_up, b_gate_up, w_down, b_down)
    return _combine(ys, posg, x1, nchunks, dest)


def kernel(x_prompt, x_sample, cache_k, cache_v, state_conv, page_table, norm1_w, w_in,
           q_norm_w, k_norm_w, lambda_q1, lambda_k1, lambda_q2, lambda_k2, subln_w,
           conv_w, conv_b, conv_ln_g, conv_ln_b, w_out, norm2_w, router_w, router_b,
           w_gate_up, b_gate_up, w_down, b_down):
    B, S, D = x_prompt.shape
    Bs, Ss, _ = x_sample.shape
    depth = norm1_w.shape[0]
    n_phys, page, heads, _, qk = cache_k.shape[1:]
    vdim = cache_v.shape[-1]
    qc, vc, cc = heads * 2 * qk, heads * vdim, conv_w.shape[2]
    taps = conv_w.shape[1]
    assert Ss == 1 and qk == QK_GROUP and vdim == V7X_LANES and (B * S) % MOE_TILE == 0
    assert Bs <= MOE_TILE and S >= taps - 1
    n_past = page_table.shape[1] * page
    T = B * S
    nt_p = T // MOE_TILE
    pos_p = jnp.arange(S, dtype=F32)
    pos_s = jnp.full((Bs,), n_past, F32)
    xp = x_prompt.reshape(T, D)
    xs = x_sample.reshape(Bs, D)
    pad_tile = lambda a: jnp.pad(a, ((0, MOE_TILE - Bs), (0, 0)))
    outs = [[] for _ in range(6)]
    for l in range(depth):
        lam_init = 0.8 - 0.6 * math.exp(-0.3 * l)
        lamv = jnp.stack([lambda_q1[l], lambda_k1[l], lambda_q2[l], lambda_k2[l]])
        conv_p = (conv_w[l], conv_b[l], conv_ln_g[l], conv_ln_b[l])
        proj_p = (norm1_w[l], w_in[l], q_norm_w[l], k_norm_w[l])
        tail_p = (w_out[l], norm2_w[l], router_w[l], router_b[l])

        q, kt, v, u, kbt, vb = _proj(xp, pos_p, S // PROJ_TILE, PROJ_TILE, *proj_p,
                                     qc=qc, vc=vc, cc=cc, exact_norm=False, attn_layout=True)
        qs, ks_, vs_, us = _proj(xs, pos_s, 1, Bs, *proj_p, qc=qc, vc=vc, cc=cc,
                                 exact_norm=True, attn_layout=False)
        cache_kt = jnp.transpose(cache_k[l], (0, 2, 3, 4, 1)).reshape(n_phys, qc, page)
        cache_vr = cache_v[l].reshape(n_phys, page * heads, vdim)
        attn, attn_s = _attention(q, kbt, vb, qs, ks_, vs_, cache_kt, cache_vr, page_table, lamv,
                                  subln_w[l], lam_init=lam_init)

        conv = _conv_prompt(u, *conv_p, batch=B, seq=S)
        bufs = _tail(attn, conv, xp, *tail_p, tm=MOE_TILE, n_valid=MOE_TILE,
                     total_tiles=nt_p + 1)
        outs[0].append(jnp.transpose(kt.reshape(B, heads, 2, qk, S), (0, 4, 1, 2, 3)))
        outs[1].append(v.reshape(B, S, heads, vdim))
        outs[2].append(u.reshape(B, S, cc)[:, S - (taps - 1):])

        conv_s = _conv_step(state_conv[l], us, *conv_p)
        x1, h, posg, cnt = _tail(pad_tile(attn_s), pad_tile(conv_s), pad_tile(xs), *tail_p,
                                 tm=MOE_TILE, n_valid=Bs, total_tiles=nt_p + 1,
                                 first_tile=nt_p, into=bufs)
        outs[3].append(ks_.reshape(Bs, Ss, heads, 2, qk))
        outs[4].append(vs_.reshape(Bs, Ss, heads, vdim))
        outs[5].append(jnp.concatenate([state_conv[l][:, Ss:], us[:, None, :]], axis=1))

        xp, y_last = _moe(h, posg, cnt[:, :, 0], x1, w_gate_up[l], b_gate_up[l], w_down[l],
                          b_down[l])
        xs = y_last[:Bs]
    return (xp.reshape(B, S, D), xs.reshape(Bs, Ss, D)) + tuple(jnp.stack(o) for o in outs)
```

```python
import functools
import math

import jax
import jax.numpy as jnp
from jax import lax
from jax.experimental import pallas as pl
from jax.experimental.pallas import tpu as pltpu

F32 = jnp.float32
BF16 = jnp.bfloat16
I32 = jnp.int32
HIGHEST = lax.Precision.HIGHEST

EPS = 1e-6
ROPE_THETA = 10000.0
SWIGLU_LIMIT = 7.0
SWIGLU_ALPHA = 1.702
TOP_K = 4
NEG = -1e30
QK_GROUP = 64

V7X_LANES = 128
V7X_SUBLANES = 8
VMEM_LIMIT = 56 * 1024 * 1024
BF16_ROWS = 16

MOE_TILE = 256
CHUNK = BF16_ROWS
MOE_BLOCK = 512
CHUNKS_PER_BLOCK = MOE_BLOCK // CHUNK
PROJ_TILE = 512


def _cparams(sem, vmem=VMEM_LIMIT):
    return pltpu.CompilerParams(dimension_semantics=sem, vmem_limit_bytes=vmem)


def _dot(a, b, exact=False):
    if exact:
        return jnp.dot(a.astype(F32), b.astype(F32), precision=HIGHEST,
                       preferred_element_type=F32)
    return jnp.dot(a.astype(BF16), b.astype(BF16), preferred_element_type=F32)


def _dot_nt(a, b, exact=False):
    dn = (((1,), (1,)), ((), ()))
    if exact:
        return lax.dot_general(a.astype(F32), b.astype(F32), dn, precision=HIGHEST,
                               preferred_element_type=F32)
    return lax.dot_general(a.astype(BF16), b.astype(BF16), dn, preferred_element_type=F32)


def _bf16_round(x):
    return x.astype(BF16).astype(F32)


def _rope_norm(p, gsum, w, cos, sin, first_half, exact_norm):
    ss = _dot(p * p, gsum, exact_norm)
    n = p * lax.rsqrt(ss * (1.0 / QK_GROUP) + EPS) * w
    outs = []
    for j in range(p.shape[1] // V7X_LANES):
        nj = n[:, j * V7X_LANES:(j + 1) * V7X_LANES]
        rot = jnp.where(first_half, pltpu.roll(nj, V7X_LANES - QK_GROUP // 2, 1),
                        pltpu.roll(nj, QK_GROUP // 2, 1))
        outs.append(nj * cos + rot * sin)
    return jnp.concatenate(outs, axis=1)


def _proj_kernel(x_ref, n1_ref, w_ref, qw_ref, kw_ref, cos_ref, sin_ref, gsum_ref,
                 q_ref, k_ref, v_ref, u_ref, *rest, qc, vc, cc, scale, exact_norm):
    x = x_ref[...]
    h = x * lax.rsqrt(jnp.mean(x * x, axis=-1, keepdims=True) + EPS) * n1_ref[...]
    hm = h.astype(BF16)
    cos = cos_ref[...]
    sin = sin_ref[...]
    lane = lax.broadcasted_iota(I32, cos.shape, 1)
    first_half = (lane % QK_GROUP) < QK_GROUP // 2
    gsum = gsum_ref[...]

    q = _rope_norm(_dot(hm, w_ref[:, 0:qc]), gsum, qw_ref[...], cos, sin, first_half, exact_norm)
    q_ref[...] = (q * scale).astype(q_ref.dtype)
    k = _rope_norm(_dot(hm, w_ref[:, qc:2 * qc]), gsum, kw_ref[...], cos, sin, first_half,
                   exact_norm)
    v = _dot(hm, w_ref[:, 2 * qc:2 * qc + vc])
    v_ref[...] = v
    o = 2 * qc + vc
    ua = _dot(hm, w_ref[:, o:o + cc])
    ub = _dot(hm, w_ref[:, o + cc:o + 2 * cc])
    u_ref[...] = ua * jax.nn.sigmoid(ub)
    if rest:
        kb_ref, vb_ref = rest
        kt = k.T
        k_ref[0] = kt
        kb_ref[0, :, 0] = kt.astype(BF16).reshape(kb_ref.shape[1], kb_ref.shape[3], kt.shape[1])
        vb_ref[...] = v.astype(BF16)
    else:
        k_ref[...] = k


def _rope_tables(pos):
    half = QK_GROUP // 2
    inv = jnp.power(ROPE_THETA, -jnp.arange(half, dtype=F32) / half)
    ang = pos[:, None] * inv[None, :]
    reps = V7X_LANES // QK_GROUP
    cos = jnp.tile(jnp.cos(ang), (1, 2 * reps))
    s = jnp.sin(ang)
    sin = jnp.tile(jnp.concatenate([-s, s], axis=1), (1, reps))
    return cos, sin


def _proj(x2d, pos_rows, n_pos_blocks, tm, norm1_w, w_in, q_norm_w, k_norm_w, *, qc, vc, cc,
          exact_norm, attn_layout):
    T, D = x2d.shape
    cos, sin = _rope_tables(pos_rows)
    gi = jnp.arange(qc) // QK_GROUP
    gsum = (gi[:, None] == gi[None, :]).astype(F32 if exact_norm else BF16)
    qw = jnp.tile(q_norm_w, qc // QK_GROUP)[None, :]
    kw = jnp.tile(k_norm_w, qc // QK_GROUP)[None, :]
    w = w_in.astype(BF16)
    row = lambda i: (i, 0)
    full = lambda i: (0, 0)
    out_shape = [jax.ShapeDtypeStruct((T, qc), BF16),
                 jax.ShapeDtypeStruct((T, qc), F32),
                 jax.ShapeDtypeStruct((T, vc), F32),
                 jax.ShapeDtypeStruct((T, cc), F32)]
    out_specs = [pl.BlockSpec((tm, qc), row), pl.BlockSpec((tm, qc), row),
                 pl.BlockSpec((tm, vc), row), pl.BlockSpec((tm, cc), row)]
    if attn_layout:
        nseq = T // (n_pos_blocks * tm)
        heads = qc // (2 * QK_GROUP)
        seq_tile = lambda i: (i // n_pos_blocks, 0, i % n_pos_blocks)
        out_shape[1] = jax.ShapeDtypeStruct((nseq, qc, n_pos_blocks * tm), F32)
        out_specs[1] = pl.BlockSpec((1, qc, tm), seq_tile)
        out_shape += [jax.ShapeDtypeStruct((nseq, heads, n_pos_blocks, 2 * QK_GROUP, tm), BF16),
                      jax.ShapeDtypeStruct((T, vc), BF16)]
        out_specs += [pl.BlockSpec((1, heads, 1, 2 * QK_GROUP, tm),
                                   lambda i: (i // n_pos_blocks, 0, i % n_pos_blocks, 0, 0)),
                      pl.BlockSpec((tm, vc), row)]
    return pl.pallas_call(
        functools.partial(_proj_kernel, qc=qc, vc=vc, cc=cc, scale=QK_GROUP ** -0.5,
                          exact_norm=exact_norm),
        grid=(T // tm,),
        in_specs=[pl.BlockSpec((tm, D), row),
                  pl.BlockSpec((1, D), full),
                  pl.BlockSpec(w.shape, full),
                  pl.BlockSpec((1, qc), full),
                  pl.BlockSpec((1, qc), full),
                  pl.BlockSpec((tm, V7X_LANES), lambda i: (i % n_pos_blocks, 0)),
                  pl.BlockSpec((tm, V7X_LANES), lambda i: (i % n_pos_blocks, 0)),
                  pl.BlockSpec((qc, qc), full)],
        out_specs=out_specs,
        out_shape=out_shape,
        compiler_params=_cparams(("arbitrary",)),
        name="proj" if attn_layout else "proj_step",
    )(x2d, norm1_w[None, :], w, qw, kw, cos, sin, gsum)


def _lambda_value(lv, lam_init):
    a = jnp.sum(lv[0:1] * lv[1:2], axis=-1, keepdims=True)
    b = jnp.sum(lv[2:3] * lv[3:4], axis=-1, keepdims=True)
    return jnp.exp(a) - jnp.exp(b) + lam_init


def _subln(o, w, lam_init):
    y = o * lax.rsqrt(jnp.mean(o * o, axis=-1, keepdims=True) + EPS)
    return y * w * (1.0 - lam_init)


def _attn_body(i, part, lam_ref, sw_ref, q_ref, k_ref, v_ref, o_ref, ml_ref, a_ref, *, tq,
               lam_init):
    lam = _lambda_value(lam_ref[...], lam_init)
    q = q_ref[...]
    lane = lax.broadcasted_iota(I32, q.shape, 1)
    zero = jnp.zeros_like(q)
    qs = (jnp.where(lane < QK_GROUP, q, zero), jnp.where(lane >= QK_GROUP, q, zero))

    def chunk(j, carry, masked):
        kc = k_ref[0, 0, j]
        vc = v_ref[pl.ds(pl.multiple_of(j * tq, tq), tq), :]
        out = []
        for c in range(2):
            m, l, acc = carry[c]
            s = jnp.dot(qs[c], kc, preferred_element_type=F32)
            if masked:
                row = lax.broadcasted_iota(I32, s.shape, 0)
                col = lax.broadcasted_iota(I32, s.shape, 1)
                s = jnp.where(col <= row, s, NEG)
            m_new = jnp.maximum(m, jnp.max(s, axis=-1, keepdims=True))
            p = jnp.exp(s - m_new)
            alpha = jnp.exp(m - m_new)
            l = alpha * l + jnp.sum(p, axis=-1, keepdims=True)
            acc = alpha * acc + _dot(p, vc)
            out.append((m_new, l, acc))
        return tuple(out)

    split = (i + 1) // 2
    if part == 0:
        init = tuple((jnp.full((tq, 1), NEG, F32), jnp.zeros((tq, 1), F32),
                      jnp.zeros((tq, V7X_LANES), F32)) for _ in range(2))
        carry = lax.fori_loop(0, split, lambda j, c: chunk(j, c, False), init)
        for c in range(2):
            ml_ref[2 * c], ml_ref[2 * c + 1], a_ref[c] = carry[c]
    else:
        carry = tuple((ml_ref[2 * c], ml_ref[2 * c + 1], a_ref[c]) for c in range(2))
        carry = lax.fori_loop(split, i, lambda j, c: chunk(j, c, False), carry)
        (_, l0, a0), (_, l1, a1) = chunk(i, carry, True)
        o = a0 / l0 - lam * (a1 / l1)
        o_ref[...] = _subln(o, sw_ref[...], lam_init).astype(o_ref.dtype)


DECODE_PAGES_PER_STEP = 32
SOFTMAX_PAGES = 16


def _decode_body(s, half_steps, lam_ref, sw_ref, qm_ref, kn_ref, vn_ref, k_refs, v_refs,
                 o_ref, s_ref, m_ref, coef_ref, acc_ref, *, page, heads, lam_init):
    pps = len(k_refs)
    rows = s_ref.shape[1]
    qm = qm_ref[0]
    row = lax.broadcasted_iota(I32, (rows, V7X_LANES), 0)
    n_pages = half_steps * pps

    @pl.when(s == 0)
    def _():
        m_ref[...] = jnp.full(m_ref.shape, NEG, F32)

    @pl.when(s < half_steps)
    def _():
        m = m_ref[...]
        for j in range(pps):
            sc = jnp.dot(qm, k_refs[j][0].astype(BF16), preferred_element_type=F32)
            s_ref[s * pps + j] = sc
            m = jnp.maximum(m, sc)
        m_ref[...] = m

    def head_weights(p):
        a = p * coef_ref[...]
        return _bf16_round(a + pltpu.roll(a, rows - heads, 0))

    @pl.when(s == half_steps)
    def _():
        s_new = jnp.sum(qm.astype(F32) * _bf16_round(kn_ref[0]), axis=-1, keepdims=True)
        m = jnp.maximum(jnp.max(m_ref[...], axis=-1, keepdims=True), s_new)

        def exp_pages(t, l):
            pages = pl.ds(pl.multiple_of(t * SOFTMAX_PAGES, SOFTMAX_PAGES), SOFTMAX_PAGES)
            p = jnp.exp(s_ref[pages] - m[None])
            s_ref[pages] = p
            return l + jnp.sum(p, axis=0)

        lsum = lax.fori_loop(0, n_pages // SOFTMAX_PAGES, exp_pages,
                             jnp.zeros((rows, V7X_LANES), F32))
        p_new = jnp.exp(s_new - m)
        l = jnp.sum(lsum, axis=-1, keepdims=True) + p_new
        lam = _lambda_value(lam_ref[...], lam_init)
        coef = jnp.where(row[:, 0:1] < heads, 1.0, -lam) / l
        coef_ref[...] = jnp.broadcast_to(coef, coef_ref.shape)
        acc_ref[...] = (head_weights(jnp.broadcast_to(p_new, (rows, V7X_LANES)))
                        * _bf16_round(vn_ref[0]))

    @pl.when(s >= half_steps)
    def _():
        lane = lax.broadcasted_iota(I32, (rows, V7X_LANES), 1)
        keep = (lane % heads == row) & (row < heads)
        acc = acc_ref[...]
        for j in range(pps):
            a = head_weights(s_ref[(s - half_steps) * pps + j])
            parts = []
            for c in range(heads):
                idx = (c * page + lane) // heads
                parts.append(jnp.where(keep, jnp.take_along_axis(a, idx, axis=1), 0.0))
            a_exp = jnp.concatenate(parts, axis=1).astype(BF16)
            acc = acc + jnp.dot(a_exp, v_refs[j][0].astype(BF16), preferred_element_type=F32)
        acc_ref[...] = acc

    @pl.when(s == 2 * half_steps - 1)
    def _():
        o_ref[0] = _subln(acc_ref[...], sw_ref[...], lam_init)


def _attention_kernel(pt_ref, lam_ref, sw_ref, q_ref, k_ref, v_ref, qm_ref, kn_ref, vn_ref, *rest,
                      pps, ratio, nq, steps_per_seq, tq, page, heads, lam_init):
    k_pages = rest[:pps]
    v_pages = rest[pps:2 * pps]
    o_ref, od_ref, s_ref, m_ref, coef_ref, acc_ref, ml_ref, a_ref = rest[2 * pps:]
    t = pl.program_id(0)
    _decode_body(t % steps_per_seq, steps_per_seq // 2, lam_ref, sw_ref, qm_ref, kn_ref, vn_ref,
                 k_pages, v_pages, od_ref, s_ref, m_ref, coef_ref, acc_ref,
                 page=page, heads=heads, lam_init=lam_init)

    for part in range(ratio):
        @pl.when(t % ratio == part)
        def _():
            _attn_body((t // ratio) % nq, part, lam_ref, sw_ref, q_ref, k_ref, v_ref, o_ref,
                       ml_ref, a_ref, tq=tq, lam_init=lam_init)


def _attention(q, kbt, vb, q_step, k_new, v_new, cache_kt, cache_vr, page_table, lamv, subln_w,
               *, lam_init):
    batch, heads, nq, _, tq = kbt.shape
    seq = nq * tq
    Bs, D = q_step.shape
    n_pages = page_table.shape[1]
    page = cache_kt.shape[2]
    vdim = cache_vr.shape[2]
    pps = DECODE_PAGES_PER_STEP
    half_steps = n_pages // pps
    steps_per_seq = 2 * half_steps
    rows = 2 * heads
    n_dec, n_att = Bs * steps_per_seq, batch * heads * nq
    assert rows == V7X_SUBLANES and page == V7X_LANES and vdim == V7X_LANES
    assert n_pages % pps == 0 and n_pages % SOFTMAX_PAGES == 0 and n_dec == 2 * n_att
    ratio = n_dec // n_att
    group = jnp.arange(D) // QK_GROUP
    rowmask = ((group % 2) * heads + group // 2)[None, :] == jnp.arange(rows)[:, None]
    qm = jnp.where(rowmask[None], q_step[:, None, :], jnp.zeros((), BF16))
    vn = jnp.pad(v_new.reshape(Bs, heads, vdim), ((0, 0), (0, rows - heads), (0, 0)))

    def k_map(j):
        return lambda t, pt: (pt[t // steps_per_seq,
                                 jnp.minimum(t % steps_per_seq, half_steps - 1) * pps + j], 0, 0)

    def v_map(j):
        return lambda t, pt: (pt[t // steps_per_seq,
                                 jnp.maximum(t % steps_per_seq - half_steps, 0) * pps + j], 0, 0)

    def att(t):
        a = t // ratio
        return a // (heads * nq), (a // nq) % heads, a % nq

    def q_map(t, pt):
        b, h, i = att(t)
        return b * nq + i, h

    seq3 = lambda t, pt: (t // steps_per_seq, 0, 0)
    const = lambda t, pt: (0, 0)
    out, out_step = pl.pallas_call(
        functools.partial(_attention_kernel, pps=pps, ratio=ratio, nq=nq,
                          steps_per_seq=steps_per_seq, tq=tq, page=page, heads=heads,
                          lam_init=lam_init),
        grid_spec=pltpu.PrefetchScalarGridSpec(
            num_scalar_prefetch=1, grid=(n_dec,),
            in_specs=[pl.BlockSpec(lamv.shape, const), pl.BlockSpec((1, vdim), const),
                      pl.BlockSpec((tq, V7X_LANES), q_map),
                      pl.BlockSpec((1, 1, nq, V7X_LANES, tq),
                                   lambda t, pt: att(t)[:2] + (0, 0, 0)),
                      pl.BlockSpec((seq, V7X_LANES), lambda t, pt: att(t)[:2]),
                      pl.BlockSpec((1, rows, D), seq3), pl.BlockSpec((1, 1, D), seq3),
                      pl.BlockSpec((1, rows, vdim), seq3)]
                     + [pl.BlockSpec((1, D, page), k_map(j)) for j in range(pps)]
                     + [pl.BlockSpec((1, page * heads, vdim), v_map(j)) for j in range(pps)],
            out_specs=[pl.BlockSpec((tq, V7X_LANES), q_map),
                       pl.BlockSpec((1, rows, vdim), seq3)],
            scratch_shapes=[pltpu.VMEM((n_pages, rows, page), F32),
                            pltpu.VMEM((rows, V7X_LANES), F32),
                            pltpu.VMEM((rows, V7X_LANES), F32),
                            pltpu.VMEM((rows, vdim), F32),
                            pltpu.VMEM((4, tq, 1), F32),
                            pltpu.VMEM((2, tq, V7X_LANES), F32)]),
        out_shape=[jax.ShapeDtypeStruct((batch * seq, heads * V7X_LANES), BF16),
                   jax.ShapeDtypeStruct((Bs, rows, vdim), F32)],
        compiler_params=_cparams(("arbitrary",)),
        name="attention",
    )(page_table, lamv, subln_w[None, :], q, kbt, vb, qm, k_new[:, None, :], vn,
      *([cache_kt] * pps), *([cache_vr] * pps))
    return out, out_step[:, :heads, :].reshape(Bs, heads * vdim)


CONV_HALO = 32


def _ln_swish(y, b_ref, g_ref, be_ref):
    y = y + b_ref[...]
    mu = jnp.mean(y, axis=-1, keepdims=True)
    yc = y - mu
    z = yc * lax.rsqrt(jnp.mean(yc * yc, axis=-1, keepdims=True) + EPS) * g_ref[...] + be_ref[...]
    return z * jax.nn.sigmoid(z)


def _conv_kernel(u_ref, w_ref, b_ref, g_ref, be_ref, o_ref, buf_ref, part_ref, *, tc, taps):
    sub = V7X_SUBLANES
    first = CONV_HALO - (taps - 1)

    @pl.when(pl.program_id(1) == 0)
    def _():
        buf_ref[0:CONV_HALO, :] = jnp.zeros((CONV_HALO, buf_ref.shape[1]), F32)
        buf_ref[CONV_HALO + tc:, :] = jnp.zeros((sub, buf_ref.shape[1]), F32)

    buf_ref[CONV_HALO:CONV_HALO + tc, :] = _bf16_round(u_ref[...])
    acc = None
    for r in range(sub):
        part = None
        for a in range(-(-(first + taps) // sub)):
            k = sub * a + r - first
            if 0 <= k < taps:
                term = buf_ref[sub * a:sub * a + tc + sub, :] * _bf16_round(w_ref[k:k + 1, :])
                part = term if part is None else part + term
        if r == 0:
            acc = part[0:tc]
        else:
            part_ref[...] = part
            acc = acc + part_ref[r:r + tc, :]
    o_ref[...] = _ln_swish(acc, b_ref, g_ref, be_ref).astype(o_ref.dtype)
    buf_ref[0:CONV_HALO, :] = buf_ref[tc:tc + CONV_HALO, :]


def _conv_prompt(u2d, conv_w, conv_b, ln_g, ln_b, *, batch, seq, tc=512):
    taps, C = conv_w.shape
    nt = seq // tc
    vec = lambda b, i: (0, 0)
    return pl.pallas_call(
        functools.partial(_conv_kernel, tc=tc, taps=taps),
        grid=(batch, nt),
        in_specs=[pl.BlockSpec((tc, C), lambda b, i: (b * nt + i, 0)),
                  pl.BlockSpec((taps, C), vec), pl.BlockSpec((1, C), vec),
                  pl.BlockSpec((1, C), vec), pl.BlockSpec((1, C), vec)],
        out_specs=pl.BlockSpec((tc, C), lambda b, i: (b * nt + i, 0)),
        out_shape=jax.ShapeDtypeStruct((batch * seq, C), BF16),
        scratch_shapes=[pltpu.VMEM((tc + CONV_HALO + V7X_SUBLANES, C), F32),
                        pltpu.VMEM((tc + V7X_SUBLANES, C), F32)],
        compiler_params=_cparams(("arbitrary", "arbitrary")),
        name="conv",
    )(u2d, conv_w, conv_b[None, :], ln_g[None, :], ln_b[None, :])


def _conv_step_kernel(st_ref, u_ref, w_ref, b_ref, g_ref, be_ref, o_ref, *, taps):
    acc = u_ref[...] * w_ref[taps - 1:taps, :]
    for k in range(taps - 1):
        acc = acc + st_ref[:, k, :] * w_ref[k:k + 1, :]
    o_ref[...] = _ln_swish(acc, b_ref, g_ref, be_ref)


def _conv_step(state, u, conv_w, conv_b, ln_g, ln_b):
    taps, C = conv_w.shape
    return pl.pallas_call(
        functools.partial(_conv_step_kernel, taps=taps),
        out_shape=jax.ShapeDtypeStruct(u.shape, F32),
        compiler_params=_cparams(None),
        name="conv_step",
    )(state, u, conv_w, conv_b[None, :], ln_g[None, :], ln_b[None, :])


def _tail_kernel(a_ref, c_ref, x_ref, wo_ref, n2_ref, rw_ref, rb_ref, *rest, n_valid):
    x1_ref, h_ref, posg_ref, cnt_ref = rest[-4:]
    tm = x_ref.shape[0]
    half = a_ref.shape[1]
    x1 = (x_ref[...] + _dot(a_ref[...], wo_ref[0:half, :])
          + _dot(c_ref[...], wo_ref[half:, :]))
    x1_ref[...] = x1
    h = (x1 * lax.rsqrt(jnp.mean(x1 * x1, axis=-1, keepdims=True) + EPS)
         * n2_ref[...]).astype(BF16)
    h_ref[...] = h
    logits = _dot_nt(rw_ref[...], h) + rb_ref[...]
    ne = logits.shape[0]
    eidx = lax.broadcasted_iota(I32, logits.shape, 0)
    valid = lax.broadcasted_iota(I32, (1, tm), 1) < n_valid

    sels, vals = [], []
    l = logits
    for _ in range(TOP_K):
        m = jnp.max(l, axis=0, keepdims=True)
        first = jnp.min(jnp.where(l == m, eidx, ne), axis=0, keepdims=True)
        sel = (eidx == first) & valid
        l = jnp.where(eidx == first, -jnp.inf, l)
        sels.append(sel)
        vals.append(m)
    ex = [jnp.exp(v - vals[0]) for v in vals]
    den = ex[0] + ex[1] + ex[2] + ex[3]
    gates = [jnp.where(valid, e / den, 0.0) for e in ex]

    msel = jnp.zeros(logits.shape, F32)
    for sel in sels:
        msel = msel + jnp.where(sel, 1.0, 0.0)
    r0 = lax.broadcasted_iota(I32, (tm, tm), 0)
    r1 = lax.broadcasted_iota(I32, (tm, tm), 1)
    upper = jnp.where(r0 < r1, 1.0, 0.0).astype(BF16)
    rank = jnp.dot(msel.astype(BF16), upper, preferred_element_type=F32)
    cnt = jnp.sum(msel, axis=1, keepdims=True)
    pcnt = jnp.ceil(cnt * (1.0 / CHUNK)) * CHUNK
    e0 = lax.broadcasted_iota(I32, (ne, ne), 0)
    e1 = lax.broadcasted_iota(I32, (ne, ne), 1)
    lower = jnp.where(e1 < e0, 1.0, 0.0)
    off = jnp.dot(lower.astype(BF16), jnp.broadcast_to(pcnt, (ne, V7X_LANES)).astype(BF16),
                  preferred_element_type=F32)[:, 0:1]
    pos = off + rank
    rows = [jnp.where(valid, jnp.sum(jnp.where(sel, pos, 0.0), axis=0, keepdims=True), -1.0)
            for sel in sels]
    posg_ref[0] = jnp.concatenate(rows + gates, axis=0)
    cnt_ref[0] = jnp.broadcast_to(cnt, (ne, V7X_LANES)).astype(I32)


def _tail(attn, conv, x2d, w_out, norm2_w, router_w, router_b, *, tm, n_valid, total_tiles,
          first_tile=0, into=None):
    T, D = x2d.shape
    half = attn.shape[1]
    ne = router_w.shape[1]
    nt = T // tm
    wo = w_out.astype(BF16)
    rw = router_w.T.astype(BF16)
    row = lambda i: (i, 0)
    full = lambda i: (0, 0)
    orow = lambda i: (first_tile + i, 0)
    otile = lambda i: (first_tile + i, 0, 0)
    extra = list(into) if into is not None else []
    return pl.pallas_call(
        functools.partial(_tail_kernel, n_valid=n_valid),
        grid=(nt,),
        in_specs=[pl.BlockSpec((tm, half), row), pl.BlockSpec((tm, half), row),
                  pl.BlockSpec((tm, D), row), pl.BlockSpec((D, D), full),
                  pl.BlockSpec((1, D), full), pl.BlockSpec((ne, D), full),
                  pl.BlockSpec((ne, 1), full)]
                 + [pl.BlockSpec(memory_space=pl.ANY)] * len(extra),
        out_specs=[pl.BlockSpec((tm, D), orow), pl.BlockSpec((tm, D), orow),
                   pl.BlockSpec((1, 2 * TOP_K, tm), otile),
                   pl.BlockSpec((1, ne, V7X_LANES), otile)],
        out_shape=[jax.ShapeDtypeStruct((total_tiles * tm, D), F32),
                   jax.ShapeDtypeStruct((total_tiles * tm, D), BF16),
                   jax.ShapeDtypeStruct((total_tiles, 2 * TOP_K, tm), F32),
                   jax.ShapeDtypeStruct((total_tiles, ne, V7X_LANES), I32)],
        input_output_aliases={7 + j: j for j in range(len(extra))},
        compiler_params=_cparams(("arbitrary",)),
        name="tail",
    )(attn, conv, x2d, wo, norm2_w[None, :], rw, router_b[:, None], *extra)


def _slots(tm, ne):
    worst = TOP_K * tm + ne * (CHUNK - 1)
    return -(-worst // V7X_LANES) * V7X_LANES


def _prefix_sum(x, axis, exclusive):
    n = x.shape[axis]
    i = jnp.arange(n)
    tri = (i[:, None] < i[None, :]) if exclusive else (i[:, None] <= i[None, :])
    xm = jnp.moveaxis(x, axis, -1)
    out = jnp.sum(xm[..., :, None] * tri.astype(x.dtype), axis=-2)
    return jnp.moveaxis(out, -1, axis)


def _moe_tables(cnt, nb, slot_chunks):
    nch = (cnt + (CHUNK - 1)) // CHUNK
    tot = jnp.sum(nch, axis=0)
    nblk = (tot + (CHUNKS_PER_BLOCK - 1)) // CHUNKS_PER_BLOCK
    bend = _prefix_sum(nblk, 0, exclusive=False)
    gstart = (bend - nblk) * CHUNKS_PER_BLOCK
    rs = gstart[None, :] + _prefix_sum(nch, 0, exclusive=True)
    tail_start = gstart + tot
    tail_n = nblk * CHUNKS_PER_BLOCK - tot
    nused = bend[-1:]
    blk = jnp.minimum(jnp.arange(nb, dtype=I32), nused - 1)
    blk_e = jnp.sum((bend[None, :] <= blk[:, None]).astype(I32), axis=1)
    cend = _prefix_sum(nch, 1, exclusive=False)
    c = jnp.arange(slot_chunks, dtype=I32)
    run = jnp.minimum(jnp.sum((cend[:, None, :] <= c[None, :, None]).astype(I32), axis=2),
                      nch.shape[1] - 1)
    shift = rs - (cend - nch)
    dest = c[None, :] + jnp.sum(jnp.where(run[:, :, None] == jnp.arange(nch.shape[1]),
                                          shift[:, None, :], 0), axis=2)
    i32 = lambda a: a.astype(I32)
    return (i32(cend[:, -1]), i32(dest), i32(tail_start), i32(tail_n), i32(nused), i32(blk_e))


def _chunk_rows(c):
    return pl.ds(pl.multiple_of(c * CHUNK, CHUNK), CHUNK)


def _for_each_chunk(n, fn):
    def body(c, carry):
        fn(c)
        return carry
    lax.fori_loop(0, n, body, 0)


WAIT_GROUP = 8


def _wait_chunks(n, copy_of_rows):
    _for_each_chunk(n // WAIT_GROUP, lambda c: copy_of_rows(WAIT_GROUP * CHUNK).wait())
    _for_each_chunk(n % WAIT_GROUP, lambda c: copy_of_rows(CHUNK).wait())


def _one_hot_rows(pos, nrows):
    r = lax.broadcasted_iota(I32, (nrows, pos.shape[1]), 0)
    p = jnp.zeros(r.shape, F32)
    for k in range(TOP_K):
        p = p + jnp.where(r == pos[k:k + 1], 1.0, 0.0)
    return p.astype(BF16)


def _dispatch_kernel(n_ref, dest_ref, ts_ref, tn_ref, h_ref, posg_ref, xs_hbm,
                     buf, zbuf, sem, zsem, *, ne):
    i = pl.program_id(0)
    slot = i % 2
    pos = posg_ref[0][0:TOP_K].astype(I32)
    buf[slot] = jnp.dot(_one_hot_rows(pos, buf.shape[1]), h_ref[...],
                        preferred_element_type=F32).astype(BF16)

    def run_copy(tile, c):
        s = tile % 2
        return pltpu.make_async_copy(buf.at[s, _chunk_rows(c)],
                                     xs_hbm.at[_chunk_rows(dest_ref[tile, c])], sem.at[s])

    def zero_copy(g):
        return pltpu.make_async_copy(zbuf, xs_hbm.at[_chunk_rows(g)], zsem)

    @pl.when(i == 0)
    def _():
        zbuf[...] = jnp.zeros(zbuf.shape, BF16)
        for phase in ("start", "wait"):
            def per_expert(e, carry):
                def body(j, c):
                    cp = zero_copy(ts_ref[e] + j)
                    cp.start() if phase == "start" else cp.wait()
                    return c
                return lax.fori_loop(0, tn_ref[e], body, carry)
            lax.fori_loop(0, ne, per_expert, 0)

    _for_each_chunk(n_ref[i], lambda c: run_copy(i, c).start())

    def wait_tile(tile):
        s = tile % 2
        _wait_chunks(n_ref[tile], lambda r: pltpu.make_async_copy(
            buf.at[s, 0:r], xs_hbm.at[0:r], sem.at[s]))

    @pl.when(i > 0)
    def _():
        wait_tile(i - 1)

    @pl.when(i == pl.num_programs(0) - 1)
    def _():
        wait_tile(i)


def _dispatch(h, posg, nchunks, dest, tail_start, tail_n, *, nb):
    T, D = h.shape
    nt, _, tm = posg.shape
    ne = tail_n.shape[0]
    return pl.pallas_call(
        functools.partial(_dispatch_kernel, ne=ne),
        grid_spec=pltpu.PrefetchScalarGridSpec(
            num_scalar_prefetch=4, grid=(nt,),
            in_specs=[pl.BlockSpec((tm, D), lambda i, *_: (i, 0)),
                      pl.BlockSpec((1, 2 * TOP_K, tm), lambda i, *_: (i, 0, 0))],
            out_specs=pl.BlockSpec(memory_space=pl.ANY),
            scratch_shapes=[pltpu.VMEM((2, _slots(tm, ne), D), BF16),
                            pltpu.VMEM((CHUNK, D), BF16),
                            pltpu.SemaphoreType.DMA((2,)), pltpu.SemaphoreType.DMA(())]),
        out_shape=jax.ShapeDtypeStruct((nb * MOE_BLOCK, D), BF16),
        compiler_params=_cparams(("arbitrary",)),
        name="dispatch",
    )(nchunks, dest, tail_start, tail_n, h, posg)


def _experts_kernel(be_ref, nu_ref, xs_ref, wgu_ref, bgu_ref, wd_ref, bd_ref, ys_ref,
                    wgu_s, wd_s):
    b = pl.program_id(0)

    @pl.when(b < nu_ref[0])
    def _():
        @pl.when((b == 0) | (be_ref[b] != be_ref[jnp.maximum(b - 1, 0)]))
        def _():
            wgu_s[...] = wgu_ref[0].astype(BF16)
            wd_s[...] = wd_ref[0].astype(BF16)

        ff = wd_s.shape[0]
        gu = jnp.dot(xs_ref[...], wgu_s[...], preferred_element_type=F32) + bgu_ref[0]
        g = jnp.minimum(gu[:, :ff], SWIGLU_LIMIT)
        u = jnp.clip(gu[:, ff:], -SWIGLU_LIMIT, SWIGLU_LIMIT)
        act = (u + 1.0) * g * jax.nn.sigmoid(SWIGLU_ALPHA * g)
        ys = jnp.dot(act.astype(BF16), wd_s[...], preferred_element_type=F32) + bd_ref[0]
        ys_ref[...] = ys.astype(ys_ref.dtype)


def _experts(xs, blk_e, nused, w_gate_up, b_gate_up, w_down, b_down):
    rows, D = xs.shape
    nb = rows // MOE_BLOCK
    ne, _, ff2 = w_gate_up.shape
    ff = w_down.shape[1]
    blk = lambda b, be, nu: (jnp.minimum(b, nu[0] - 1), 0)
    exp3 = lambda b, be, nu: (be[b], 0, 0)
    return pl.pallas_call(
        _experts_kernel,
        grid_spec=pltpu.PrefetchScalarGridSpec(
            num_scalar_prefetch=2, grid=(nb,),
            in_specs=[pl.BlockSpec((MOE_BLOCK, D), blk),
                      pl.BlockSpec((1, D, ff2), exp3), pl.BlockSpec((1, 1, ff2), exp3),
                      pl.BlockSpec((1, ff, D), exp3), pl.BlockSpec((1, 1, D), exp3)],
            out_specs=pl.BlockSpec((MOE_BLOCK, D), blk),
            scratch_shapes=[pltpu.VMEM((D, ff2), BF16), pltpu.VMEM((ff, D), BF16)]),
        out_shape=jax.ShapeDtypeStruct((rows, D), BF16),
        compiler_params=_cparams(("arbitrary",)),
        name="experts",
    )(blk_e, nused, xs, w_gate_up, b_gate_up[:, None, :], w_down, b_down[:, None, :])


def _combine_kernel(n_ref, dest_ref, ys_hbm, posg_ref, x1_ref, y_ref, ylast_ref, buf, sem):
    i = pl.program_id(0)
    last = pl.num_programs(0) - 1
    tm = x1_ref.shape[0]
    nslots = buf.shape[1]

    def run_copy(tile, c):
        s = tile % 2
        return pltpu.make_async_copy(ys_hbm.at[_chunk_rows(dest_ref[tile, c])],
                                     buf.at[s, _chunk_rows(c)], sem.at[s])

    def fetch(tile):
        _for_each_chunk(n_ref[tile], lambda c: run_copy(tile, c).start())
        s = tile % 2

        def zero_chunk(c, carry):
            buf[s, _chunk_rows(c), :] = jnp.zeros((CHUNK, buf.shape[2]), BF16)
            return carry
        lax.fori_loop(n_ref[tile], nslots // CHUNK, zero_chunk, 0)

    @pl.when(i == 0)
    def _():
        fetch(i)

    @pl.when(i < last)
    def _():
        fetch(i + 1)

    r0 = lax.broadcasted_iota(I32, (tm, tm), 0)
    r1 = lax.broadcasted_iota(I32, (tm, tm), 1)
    posg_t = _dot_nt(jnp.where(r0 == r1, 1.0, 0.0), posg_ref[0], exact=True)
    slot = lax.broadcasted_iota(I32, (tm, nslots), 1)
    w = jnp.zeros(slot.shape, F32)
    for k in range(TOP_K):
        w = w + jnp.where(slot == posg_t[:, k:k + 1].astype(I32),
                          posg_t[:, TOP_K + k:TOP_K + k + 1], 0.0)

    _wait_chunks(n_ref[i], lambda r: pltpu.make_async_copy(
        ys_hbm.at[0:r], buf.at[i % 2, 0:r], sem.at[i % 2]))
    y = x1_ref[...] + jnp.dot(w.astype(BF16), buf[i % 2], preferred_element_type=F32)

    @pl.when(i < last)
    def _():
        y_ref[...] = y

    @pl.when(i == last)
    def _():
        ylast_ref[...] = y


def _combine(ys, posg, x1, nchunks, dest):
    T, D = x1.shape
    nt, _, tm = posg.shape
    return pl.pallas_call(
        _combine_kernel,
        grid_spec=pltpu.PrefetchScalarGridSpec(
            num_scalar_prefetch=2, grid=(nt,),
            in_specs=[pl.BlockSpec(memory_space=pl.ANY),
                      pl.BlockSpec((1, 2 * TOP_K, tm), lambda i, *_: (i, 0, 0)),
                      pl.BlockSpec((tm, D), lambda i, *_: (i, 0))],
            out_specs=[pl.BlockSpec((tm, D), lambda i, *_: (jnp.minimum(i, nt - 2), 0)),
                       pl.BlockSpec((tm, D), lambda i, *_: (0, 0))],
            scratch_shapes=[pltpu.VMEM((2, dest.shape[1] * CHUNK, D), BF16),
                            pltpu.SemaphoreType.DMA((2,))]),
        out_shape=[jax.ShapeDtypeStruct((T - tm, D), F32), jax.ShapeDtypeStruct((tm, D), F32)],
        compiler_params=_cparams(("arbitrary",)),
        name="combine",
    )(nchunks, dest, ys, posg, x1)


def _moe(h, posg, cnt, x1, w_gate_up, b_gate_up, w_down, b_down):
    nt, _, tm = posg.shape
    ne = cnt.shape[1]
    max_chunks = (nt * tm * TOP_K) // CHUNK + nt * ne
    nb = -(-max_chunks // CHUNKS_PER_BLOCK) + ne
    nchunks, dest, tail_start, tail_n, nused, blk_e = _moe_tables(cnt, nb, _slots(tm, ne) // CHUNK)
    xs = _dispatch(h, posg, nchunks, dest, tail_start, tail_n, nb=nb)
    ys = _experts(xs, blk_e, nused, w_gate_up, b_gate_up, w_down, b_down)
    return _combine(ys, posg, x1, nchunks, dest)


def kernel(x_prompt, x_sample, cache_k, cache_v, state_conv, page_table, norm1_w, w_in,
           q_norm_w, k_norm_w, lambda_q1, lambda_k1, lambda_q2, lambda_k2, subln_w,
           conv_w, conv_b, conv_ln_g, conv_ln_b, w_out, norm2_w, router_w, router_b,
           w_gate_up, b_gate_up, w_down, b_down):
    B, S, D = x_prompt.shape
    Bs, Ss, _ = x_sample.shape
    depth = norm1_w.shape[0]
    n_phys, page, heads, _, qk = cache_k.shape[1:]
    vdim = cache_v.shape[-1]
    qc, vc, cc = heads * 2 * qk, heads * vdim, conv_w.shape[2]
    taps = conv_w.shape[1]
    assert Ss == 1 and qk == QK_GROUP and vdim == V7X_LANES and (B * S) % MOE_TILE == 0
    assert Bs <= MOE_TILE and S >= taps - 1
    n_past = page_table.shape[1] * page
    T = B * S
    nt_p = T // MOE_TILE
    pos_p = jnp.arange(S, dtype=F32)
    pos_s = jnp.full((Bs,), n_past, F32)
    xp = x_prompt.reshape(T, D)
    xs = x_sample.reshape(Bs, D)
    pad_tile = lambda a: jnp.pad(a, ((0, MOE_TILE - Bs), (0, 0)))
    outs = [[] for _ in range(6)]
    for l in range(depth):
        lam_init = 0.8 - 0.6 * math.exp(-0.3 * l)
        lamv = jnp.stack([lambda_q1[l], lambda_k1[l], lambda_q2[l], lambda_k2[l]])
        conv_p = (conv_w[l], conv_b[l], conv_ln_g[l], conv_ln_b[l])
        proj_p = (norm1_w[l], w_in[l], q_norm_w[l], k_norm_w[l])
        tail_p = (w_out[l], norm2_w[l], router_w[l], router_b[l])

        q, kt, v, u, kbt, vb = _proj(xp, pos_p, S // PROJ_TILE, PROJ_TILE, *proj_p,
                                     qc=qc, vc=vc, cc=cc, exact_norm=False, attn_layout=True)
        qs, ks_, vs_, us = _proj(xs, pos_s, 1, Bs, *proj_p, qc=qc, vc=vc, cc=cc,
                                 exact_norm=True, attn_layout=False)
        cache_kt = jnp.transpose(cache_k[l], (0, 2, 3, 4, 1)).reshape(n_phys, qc, page)
        cache_vr = cache_v[l].reshape(n_phys, page * heads, vdim)
        attn, attn_s = _attention(q, kbt, vb, qs, ks_, vs_, cache_kt, cache_vr, page_table, lamv,
                                  subln_w[l], lam_init=lam_init)

        conv = _conv_prompt(u, *conv_p, batch=B, seq=S)
        bufs = _tail(attn, conv, xp, *tail_p, tm=MOE_TILE, n_valid=MOE_TILE,
                     total_tiles=nt_p + 1)
        outs[0].append(jnp.transpose(kt.reshape(B, heads, 2, qk, S), (0, 4, 1, 2, 3)))
        outs[1].append(v.reshape(B, S, heads, vdim))
        outs[2].append(u.reshape(B, S, cc)[:, S - (taps - 1):])

        conv_s = _conv_step(state_conv[l], us, *conv_p)
        x1, h, posg, cnt = _tail(pad_tile(attn_s), pad_tile(conv_s), pad_tile(xs), *tail_p,
                                 tm=MOE_TILE, n_valid=Bs, total_tiles=nt_p + 1,
                                 first_tile=nt_p, into=bufs)
        outs[3].append(ks_.reshape(Bs, Ss, heads, 2, qk))
        outs[4].append(vs_.reshape(Bs, Ss, heads, vdim))
        outs[5].append(jnp.concatenate([state_conv[l][:, Ss:], us[:, None, :]], axis=1))

        xp, y_last = _moe(h, posg, cnt[:, :, 0], x1, w_gate_up[l], b_gate_up[l], w_down[l],
                          b_down[l])
        xs = y_last[:Bs]
    return (xp.reshape(B, S, D), xs.reshape(Bs, Ss, D)) + tuple(jnp.stack(o) for o in outs)
```

```python
import functools
import math

import jax
import jax.numpy as jnp
from jax import lax
from jax.experimental import pallas as pl
from jax.experimental.pallas import tpu as pltpu

F32 = jnp.float32
BF16 = jnp.bfloat16
I32 = jnp.int32
HIGHEST = lax.Precision.HIGHEST

EPS = 1e-6
ROPE_THETA = 10000.0
SWIGLU_LIMIT = 7.0
SWIGLU_ALPHA = 1.702
TOP_K = 4
NEG = -1e30
QK_GROUP = 64

V7X_LANES = 128
V7X_SUBLANES = 8
VMEM_LIMIT = 56 * 1024 * 1024
BF16_ROWS = 16

MOE_TILE = 256
CHUNK = BF16_ROWS
MOE_BLOCK = 512
CHUNKS_PER_BLOCK = MOE_BLOCK // CHUNK
PROJ_TILE = 512


def _cparams(sem, vmem=VMEM_LIMIT):
    return pltpu.CompilerParams(dimension_semantics=sem, vmem_limit_bytes=vmem)


def _dot(a, b, exact=False):
    if exact:
        return jnp.dot(a.astype(F32), b.astype(F32), precision=HIGHEST,
                       preferred_element_type=F32)
    return jnp.dot(a.astype(BF16), b.astype(BF16), preferred_element_type=F32)


def _dot_nt(a, b, exact=False):
    dn = (((1,), (1,)), ((), ()))
    if exact:
        return lax.dot_general(a.astype(F32), b.astype(F32), dn, precision=HIGHEST,
                               preferred_element_type=F32)
    return lax.dot_general(a.astype(BF16), b.astype(BF16), dn, preferred_element_type=F32)


def _bf16_round(x):
    return x.astype(BF16).astype(F32)


def _rope_norm(p, gsum, w, cos, sin, first_half, exact_norm):
    ss = _dot(p * p, gsum, exact_norm)
    n = p * lax.rsqrt(ss * (1.0 / QK_GROUP) + EPS) * w
    outs = []
    for j in range(p.shape[1] // V7X_LANES):
        nj = n[:, j * V7X_LANES:(j + 1) * V7X_LANES]
        rot = jnp.where(first_half, pltpu.roll(nj, V7X_LANES - QK_GROUP // 2, 1),
                        pltpu.roll(nj, QK_GROUP // 2, 1))
        outs.append(nj * cos + rot * sin)
    return jnp.concatenate(outs, axis=1)


def _proj_kernel(x_ref, n1_ref, w_ref, qw_ref, kw_ref, cos_ref, sin_ref, gsum_ref,
                 q_ref, k_ref, v_ref, u_ref, *rest, qc, vc, cc, scale, exact_norm):
    x = x_ref[...]
    h = x * lax.rsqrt(jnp.mean(x * x, axis=-1, keepdims=True) + EPS) * n1_ref[...]
    hm = h.astype(BF16)
    cos = cos_ref[...]
    sin = sin_ref[...]
    lane = lax.broadcasted_iota(I32, cos.shape, 1)
    first_half = (lane % QK_GROUP) < QK_GROUP // 2
    gsum = gsum_ref[...]

    q = _rope_norm(_dot(hm, w_ref[:, 0:qc]), gsum, qw_ref[...], cos, sin, first_half, exact_norm)
    q_ref[...] = (q * scale).astype(q_ref.dtype)
    k = _rope_norm(_dot(hm, w_ref[:, qc:2 * qc]), gsum, kw_ref[...], cos, sin, first_half,
                   exact_norm)
    v = _dot(hm, w_ref[:, 2 * qc:2 * qc + vc])
    v_ref[...] = v
    o = 2 * qc + vc
    ua = _dot(hm, w_ref[:, o:o + cc])
    ub = _dot(hm, w_ref[:, o + cc:o + 2 * cc])
    u_ref[...] = ua * jax.nn.sigmoid(ub)
    if rest:
        kb_ref, vb_ref = rest
        kt = k.T
        k_ref[0] = kt
        kb_ref[0, :, 0] = kt.astype(BF16).reshape(kb_ref.shape[1], kb_ref.shape[3], kt.shape[1])
        vb_ref[...] = v.astype(BF16)
    else:
        k_ref[...] = k


def _rope_tables(pos):
    half = QK_GROUP // 2
    inv = jnp.power(ROPE_THETA, -jnp.arange(half, dtype=F32) / half)
    ang = pos[:, None] * inv[None, :]
    reps = V7X_LANES // QK_GROUP
    cos = jnp.tile(jnp.cos(ang), (1, 2 * reps))
    s = jnp.sin(ang)
    sin = jnp.tile(jnp.concatenate([-s, s], axis=1), (1, reps))
    return cos, sin


def _proj(x2d, pos_rows, n_pos_blocks, tm, norm1_w, w_in, q_norm_w, k_norm_w, *, qc, vc, cc,
          exact_norm, attn_layout):
    T, D = x2d.shape
    cos, sin = _rope_tables(pos_rows)
    gi = jnp.arange(qc) // QK_GROUP
    gsum = (gi[:, None] == gi[None, :]).astype(F32 if exact_norm else BF16)
    qw = jnp.tile(q_norm_w, qc // QK_GROUP)[None, :]
    kw = jnp.tile(k_norm_w, qc // QK_GROUP)[None, :]
    w = w_in.astype(BF16)
    row = lambda i: (i, 0)
    full = lambda i: (0, 0)
    out_shape = [jax.ShapeDtypeStruct((T, qc), BF16),
                 jax.ShapeDtypeStruct((T, qc), F32),
                 jax.ShapeDtypeStruct((T, vc), F32),
                 jax.ShapeDtypeStruct((T, cc), F32)]
    out_specs = [pl.BlockSpec((tm, qc), row), pl.BlockSpec((tm, qc), row),
                 pl.BlockSpec((tm, vc), row), pl.BlockSpec((tm, cc), row)]
    if attn_layout:
        nseq = T // (n_pos_blocks * tm)
        heads = qc // (2 * QK_GROUP)
        seq_tile = lambda i: (i // n_pos_blocks, 0, i % n_pos_blocks)
        out_shape[1] = jax.ShapeDtypeStruct((nseq, qc, n_pos_blocks * tm), F32)
        out_specs[1] = pl.BlockSpec((1, qc, tm), seq_tile)
        out_shape += [jax.ShapeDtypeStruct((nseq, heads, n_pos_blocks, 2 * QK_GROUP, tm), BF16),
                      jax.ShapeDtypeStruct((T, vc), BF16)]
        out_specs += [pl.BlockSpec((1, heads, 1, 2 * QK_GROUP, tm),
                                   lambda i: (i // n_pos_blocks, 0, i % n_pos_blocks, 0, 0)),
                      pl.BlockSpec((tm, vc), row)]
    return pl.pallas_call(
        functools.partial(_proj_kernel, qc=qc, vc=vc, cc=cc, scale=QK_GROUP ** -0.5,
                          exact_norm=exact_norm),
        grid=(T // tm,),
        in_specs=[pl.BlockSpec((tm, D), row),
                  pl.BlockSpec((1, D), full),
                  pl.BlockSpec(w.shape, full),
                  pl.BlockSpec((1, qc), full),
                  pl.BlockSpec((1, qc), full),
                  pl.BlockSpec((tm, V7X_LANES), lambda i: (i % n_pos_blocks, 0)),
                  pl.BlockSpec((tm, V7X_LANES), lambda i: (i % n_pos_blocks, 0)),
                  pl.BlockSpec((qc, qc), full)],
        out_specs=out_specs,
        out_shape=out_shape,
        compiler_params=_cparams(("arbitrary",)),
        name="proj" if attn_layout else "proj_step",
    )(x2d, norm1_w[None, :], w, qw, kw, cos, sin, gsum)


def _lambda_value(lv, lam_init):
    a = jnp.sum(lv[0:1] * lv[1:2], axis=-1, keepdims=True)
    b = jnp.sum(lv[2:3] * lv[3:4], axis=-1, keepdims=True)
    return jnp.exp(a) - jnp.exp(b) + lam_init


def _subln(o, w, lam_init):
    y = o * lax.rsqrt(jnp.mean(o * o, axis=-1, keepdims=True) + EPS)
    return y * w * (1.0 - lam_init)


def _attn_body(i, lam_ref, sw_ref, q_ref, k_ref, v_ref, o_ref, *, tq, lam_init):
    lam = _lambda_value(lam_ref[...], lam_init)
    q = q_ref[...]
    lane = lax.broadcasted_iota(I32, q.shape, 1)
    zero = jnp.zeros_like(q)
    qs = (jnp.where(lane < QK_GROUP, q, zero), jnp.where(lane >= QK_GROUP, q, zero))

    def chunk(j, carry, masked):
        kc = k_ref[0, 0, j]
        vc = v_ref[pl.ds(pl.multiple_of(j * tq, tq), tq), :]
        out = []
        for c in range(2):
            m, l, acc = carry[c]
            s = jnp.dot(qs[c], kc, preferred_element_type=F32)
            if masked:
                row = lax.broadcasted_iota(I32, s.shape, 0)
                col = lax.broadcasted_iota(I32, s.shape, 1)
                s = jnp.where(col <= row, s, NEG)
            m_new = jnp.maximum(m, jnp.max(s, axis=-1, keepdims=True))
            p = jnp.exp(s - m_new)
            alpha = jnp.exp(m - m_new)
            l = alpha * l + jnp.sum(p, axis=-1, keepdims=True)
            acc = alpha * acc + _dot(p, vc)
            out.append((m_new, l, acc))
        return tuple(out)

    init = tuple((jnp.full((tq, 1), NEG, F32), jnp.zeros((tq, 1), F32),
                  jnp.zeros((tq, V7X_LANES), F32)) for _ in range(2))
    carry = lax.fori_loop(0, i, lambda j, c: chunk(j, c, False), init)
    (_, l0, a0), (_, l1, a1) = chunk(i, carry, True)
    o = a0 / l0 - lam * (a1 / l1)
    o_ref[...] = _subln(o, sw_ref[...], lam_init).astype(o_ref.dtype)


DECODE_PAGES_PER_STEP = 32
SOFTMAX_PAGES = 16
PAGE_RING_DEPTH = 3


def _decode_body(s, half_steps, lam_ref, sw_ref, qm_ref, kn_ref, vn_ref, pages_ref,
                 o_ref, s_ref, m_ref, coef_ref, acc_ref, *, page, heads, lam_init):
    pps = pages_ref.shape[0]
    rows = s_ref.shape[1]
    qm = qm_ref[0]
    row = lax.broadcasted_iota(I32, (rows, V7X_LANES), 0)
    n_pages = half_steps * pps

    @pl.when(s == 0)
    def _():
        m_ref[...] = jnp.full(m_ref.shape, NEG, F32)

    @pl.when(s < half_steps)
    def _():
        m = m_ref[...]
        for j in range(pps):
            sc = jnp.dot(qm, pages_ref[j].astype(BF16), preferred_element_type=F32)
            s_ref[s * pps + j] = sc
            m = jnp.maximum(m, sc)
        m_ref[...] = m

    def head_weights(p):
        a = p * coef_ref[...]
        return _bf16_round(a + pltpu.roll(a, rows - heads, 0))

    @pl.when(s == half_steps)
    def _():
        s_new = jnp.sum(qm.astype(F32) * _bf16_round(kn_ref[0]), axis=-1, keepdims=True)
        m = jnp.maximum(jnp.max(m_ref[...], axis=-1, keepdims=True), s_new)

        def exp_pages(t, l):
            pages = pl.ds(pl.multiple_of(t * SOFTMAX_PAGES, SOFTMAX_PAGES), SOFTMAX_PAGES)
            p = jnp.exp(s_ref[pages] - m[None])
            s_ref[pages] = p
            return l + jnp.sum(p, axis=0)

        lsum = lax.fori_loop(0, n_pages // SOFTMAX_PAGES, exp_pages,
                             jnp.zeros((rows, V7X_LANES), F32))
        p_new = jnp.exp(s_new - m)
        l = jnp.sum(lsum, axis=-1, keepdims=True) + p_new
        lam = _lambda_value(lam_ref[...], lam_init)
        coef = jnp.where(row[:, 0:1] < heads, 1.0, -lam) / l
        coef_ref[...] = jnp.broadcast_to(coef, coef_ref.shape)
        acc_ref[...] = (head_weights(jnp.broadcast_to(p_new, (rows, V7X_LANES)))
                        * _bf16_round(vn_ref[0]))

    @pl.when(s >= half_steps)
    def _():
        lane = lax.broadcasted_iota(I32, (rows, V7X_LANES), 1)
        keep = (lane % heads == row) & (row < heads)
        acc = acc_ref[...]
        for j in range(pps):
            a = head_weights(s_ref[(s - half_steps) * pps + j])
            parts = []
            for c in range(heads):
                idx = (c * page + lane) // heads
                parts.append(jnp.where(keep, jnp.take_along_axis(a, idx, axis=1), 0.0))
            a_exp = jnp.concatenate(parts, axis=1).astype(BF16)
            acc = acc + jnp.dot(a_exp, pages_ref[j].astype(BF16), preferred_element_type=F32)
        acc_ref[...] = acc

    @pl.when(s == 2 * half_steps - 1)
    def _():
        o_ref[0] = _subln(acc_ref[...], sw_ref[...], lam_init)


def _attention_kernel(pt_ref, lam_ref, sw_ref, q_ref, k_ref, v_ref, qm_ref, kn_ref, vn_ref,
                      kt_hbm, vr_hbm, o_ref, od_ref, s_ref, m_ref, coef_ref, acc_ref, pages, sem,
                      *, ratio, nq, steps_per_seq, tq, page, heads, lam_init):
    t = pl.program_id(0)
    depth, pps = pages.shape[:2]
    half_steps = steps_per_seq // 2

    def fetch(step):
        seq = step // steps_per_seq
        s = step % steps_per_seq
        first = (s % half_steps) * pps
        slot = step % depth
        for src_hbm, cond in ((kt_hbm, s < half_steps), (vr_hbm, s >= half_steps)):
            @pl.when(cond)
            def _():
                def start(j, carry):
                    pltpu.make_async_copy(src_hbm.at[pt_ref[seq, first + j]],
                                          pages.at[slot, j], sem.at[slot]).start()
                    return carry
                lax.fori_loop(0, pps, start, 0)

    @pl.when(t == 0)
    def _():
        for ahead in range(depth - 1):
            fetch(t + ahead)

    @pl.when(t + depth - 1 < pl.num_programs(0))
    def _():
        fetch(t + depth - 1)

    slot = t % depth
    pltpu.make_async_copy(kt_hbm.at[0:pps], pages.at[slot], sem.at[slot]).wait()
    _decode_body(t % steps_per_seq, half_steps, lam_ref, sw_ref, qm_ref, kn_ref, vn_ref,
                 pages.at[slot], od_ref, s_ref, m_ref, coef_ref, acc_ref,
                 page=page, heads=heads, lam_init=lam_init)

    @pl.when(t % ratio == 0)
    def _():
        _attn_body((t // ratio) % nq, lam_ref, sw_ref, q_ref, k_ref, v_ref, o_ref,
                   tq=tq, lam_init=lam_init)


def _attention(q, kbt, vb, q_step, k_new, v_new, cache_kt, cache_vr, page_table, lamv, subln_w,
               *, lam_init):
    batch, heads, nq, _, tq = kbt.shape
    seq = nq * tq
    Bs, D = q_step.shape
    n_pages = page_table.shape[1]
    page = cache_kt.shape[2]
    vdim = cache_vr.shape[2]
    pps = DECODE_PAGES_PER_STEP
    half_steps = n_pages // pps
    steps_per_seq = 2 * half_steps
    rows = 2 * heads
    n_dec, n_att = Bs * steps_per_seq, batch * heads * nq
    assert rows == V7X_SUBLANES and page == V7X_LANES and vdim == V7X_LANES
    assert n_pages % pps == 0 and n_pages % SOFTMAX_PAGES == 0 and n_dec % n_att == 0
    assert cache_kt.shape[1:] == cache_vr.shape[1:]
    ratio = n_dec // n_att
    group = jnp.arange(D) // QK_GROUP
    rowmask = ((group % 2) * heads + group // 2)[None, :] == jnp.arange(rows)[:, None]
    qm = jnp.where(rowmask[None], q_step[:, None, :], jnp.zeros((), BF16))
    vn = jnp.pad(v_new.reshape(Bs, heads, vdim), ((0, 0), (0, rows - heads), (0, 0)))

    def att(t):
        a = t // ratio
        return a // (heads * nq), (a // nq) % heads, a % nq

    def q_map(t, pt):
        b, h, i = att(t)
        return b * nq + i, h

    seq3 = lambda t, pt: (t // steps_per_seq, 0, 0)
    const = lambda t, pt: (0, 0)
    out, out_step = pl.pallas_call(
        functools.partial(_attention_kernel, ratio=ratio, nq=nq,
                          steps_per_seq=steps_per_seq, tq=tq, page=page, heads=heads,
                          lam_init=lam_init),
        grid_spec=pltpu.PrefetchScalarGridSpec(
            num_scalar_prefetch=1, grid=(n_dec,),
            in_specs=[pl.BlockSpec(lamv.shape, const), pl.BlockSpec((1, vdim), const),
                      pl.BlockSpec((tq, V7X_LANES), q_map),
                      pl.BlockSpec((1, 1, nq, V7X_LANES, tq),
                                   lambda t, pt: att(t)[:2] + (0, 0, 0)),
                      pl.BlockSpec((seq, V7X_LANES), lambda t, pt: att(t)[:2]),
                      pl.BlockSpec((1, rows, D), seq3), pl.BlockSpec((1, 1, D), seq3),
                      pl.BlockSpec((1, rows, vdim), seq3),
                      pl.BlockSpec(memory_space=pl.ANY), pl.BlockSpec(memory_space=pl.ANY)],
            out_specs=[pl.BlockSpec((tq, V7X_LANES), q_map),
                       pl.BlockSpec((1, rows, vdim), seq3)],
            scratch_shapes=[pltpu.VMEM((n_pages, rows, page), F32),
                            pltpu.VMEM((rows, V7X_LANES), F32),
                            pltpu.VMEM((rows, V7X_LANES), F32),
                            pltpu.VMEM((rows, vdim), F32),
                            pltpu.VMEM((PAGE_RING_DEPTH, pps) + cache_kt.shape[1:], F32),
                            pltpu.SemaphoreType.DMA((PAGE_RING_DEPTH,))]),
        out_shape=[jax.ShapeDtypeStruct((batch * seq, heads * V7X_LANES), BF16),
                   jax.ShapeDtypeStruct((Bs, rows, vdim), F32)],
        compiler_params=_cparams(("arbitrary",)),
        name="attention",
    )(page_table, lamv, subln_w[None, :], q, kbt, vb, qm, k_new[:, None, :], vn,
      cache_kt, cache_vr)
    return out, out_step[:, :heads, :].reshape(Bs, heads * vdim)


CONV_HALO = 32


def _ln_swish(y, b_ref, g_ref, be_ref):
    y = y + b_ref[...]
    mu = jnp.mean(y, axis=-1, keepdims=True)
    yc = y - mu
    z = yc * lax.rsqrt(jnp.mean(yc * yc, axis=-1, keepdims=True) + EPS) * g_ref[...] + be_ref[...]
    return z * jax.nn.sigmoid(z)


def _conv_kernel(u_ref, w_ref, b_ref, g_ref, be_ref, o_ref, buf_ref, part_ref, *, tc, taps):
    sub = V7X_SUBLANES
    first = CONV_HALO - (taps - 1)

    @pl.when(pl.program_id(1) == 0)
    def _():
        buf_ref[0:CONV_HALO, :] = jnp.zeros((CONV_HALO, buf_ref.shape[1]), F32)
        buf_ref[CONV_HALO + tc:, :] = jnp.zeros((sub, buf_ref.shape[1]), F32)

    buf_ref[CONV_HALO:CONV_HALO + tc, :] = _bf16_round(u_ref[...])
    acc = None
    for r in range(sub):
        part = None
        for a in range(-(-(first + taps) // sub)):
            k = sub * a + r - first
            if 0 <= k < taps:
                term = buf_ref[sub * a:sub * a + tc + sub, :] * _bf16_round(w_ref[k:k + 1, :])
                part = term if part is None else part + term
        if r == 0:
            acc = part[0:tc]
        else:
            part_ref[...] = part
            acc = acc + part_ref[r:r + tc, :]
    o_ref[...] = _ln_swish(acc, b_ref, g_ref, be_ref).astype(o_ref.dtype)
    buf_ref[0:CONV_HALO, :] = buf_ref[tc:tc + CONV_HALO, :]


def _conv_prompt(u2d, conv_w, conv_b, ln_g, ln_b, *, batch, seq, tc=512):
    taps, C = conv_w.shape
    nt = seq // tc
    vec = lambda b, i: (0, 0)
    return pl.pallas_call(
        functools.partial(_conv_kernel, tc=tc, taps=taps),
        grid=(batch, nt),
        in_specs=[pl.BlockSpec((tc, C), lambda b, i: (b * nt + i, 0)),
                  pl.BlockSpec((taps, C), vec), pl.BlockSpec((1, C), vec),
                  pl.BlockSpec((1, C), vec), pl.BlockSpec((1, C), vec)],
        out_specs=pl.BlockSpec((tc, C), lambda b, i: (b * nt + i, 0)),
        out_shape=jax.ShapeDtypeStruct((batch * seq, C), BF16),
        scratch_shapes=[pltpu.VMEM((tc + CONV_HALO + V7X_SUBLANES, C), F32),
                        pltpu.VMEM((tc + V7X_SUBLANES, C), F32)],
        compiler_params=_cparams(("arbitrary", "arbitrary")),
        name="conv",
    )(u2d, conv_w, conv_b[None, :], ln_g[None, :], ln_b[None, :])


def _conv_step_kernel(st_ref, u_ref, w_ref, b_ref, g_ref, be_ref, o_ref, *, taps):
    acc = u_ref[...] * w_ref[taps - 1:taps, :]
    for k in range(taps - 1):
        acc = acc + st_ref[:, k, :] * w_ref[k:k + 1, :]
    o_ref[...] = _ln_swish(acc, b_ref, g_ref, be_ref)


def _conv_step(state, u, conv_w, conv_b, ln_g, ln_b):
    taps, C = conv_w.shape
    return pl.pallas_call(
        functools.partial(_conv_step_kernel, taps=taps),
        out_shape=jax.ShapeDtypeStruct(u.shape, F32),
        compiler_params=_cparams(None),
        name="conv_step",
    )(state, u, conv_w, conv_b[None, :], ln_g[None, :], ln_b[None, :])


def _tail_kernel(a_ref, c_ref, x_ref, wo_ref, n2_ref, rw_ref, rb_ref, *rest, n_valid):
    x1_ref, h_ref, posg_ref, cnt_ref = rest[-4:]
    tm = x_ref.shape[0]
    half = a_ref.shape[1]
    x1 = (x_ref[...] + _dot(a_ref[...], wo_ref[0:half, :])
          + _dot(c_ref[...], wo_ref[half:, :]))
    x1_ref[...] = x1
    h = (x1 * lax.rsqrt(jnp.mean(x1 * x1, axis=-1, keepdims=True) + EPS)
         * n2_ref[...]).astype(BF16)
    h_ref[...] = h
    logits = _dot_nt(rw_ref[...], h) + rb_ref[...]
    ne = logits.shape[0]
    eidx = lax.broadcasted_iota(I32, logits.shape, 0)
    valid = lax.broadcasted_iota(I32, (1, tm), 1) < n_valid

    sels, vals = [], []
    l = logits
    for _ in range(TOP_K):
        m = jnp.max(l, axis=0, keepdims=True)
        first = jnp.min(jnp.where(l == m, eidx, ne), axis=0, keepdims=True)
        sel = (eidx == first) & valid
        l = jnp.where(eidx == first, -jnp.inf, l)
        sels.append(sel)
        vals.append(m)
    ex = [jnp.exp(v - vals[0]) for v in vals]
    den = ex[0] + ex[1] + ex[2] + ex[3]
    gates = [jnp.where(valid, e / den, 0.0) for e in ex]

    msel = jnp.zeros(logits.shape, F32)
    for sel in sels:
        msel = msel + jnp.where(sel, 1.0, 0.0)
    r0 = lax.broadcasted_iota(I32, (tm, tm), 0)
    r1 = lax.broadcasted_iota(I32, (tm, tm), 1)
    upper = jnp.where(r0 < r1, 1.0, 0.0).astype(BF16)
    rank = jnp.dot(msel.astype(BF16), upper, preferred_element_type=F32)
    cnt = jnp.sum(msel, axis=1, keepdims=True)
    pcnt = jnp.ceil(cnt * (1.0 / CHUNK)) * CHUNK
    e0 = lax.broadcasted_iota(I32, (ne, ne), 0)
    e1 = lax.broadcasted_iota(I32, (ne, ne), 1)
    lower = jnp.where(e1 < e0, 1.0, 0.0)
    off = jnp.dot(lower.astype(BF16), jnp.broadcast_to(pcnt, (ne, V7X_LANES)).astype(BF16),
                  preferred_element_type=F32)[:, 0:1]
    pos = off + rank
    rows = [jnp.where(valid, jnp.sum(jnp.where(sel, pos, 0.0), axis=0, keepdims=True), -1.0)
            for sel in sels]
    posg_ref[0] = jnp.concatenate(rows + gates, axis=0)
    cnt_ref[0] = jnp.broadcast_to(cnt, (ne, V7X_LANES)).astype(I32)


def _tail(attn, conv, x2d, w_out, norm2_w, router_w, router_b, *, tm, n_valid, total_tiles,
          first_tile=0, into=None):
    T, D = x2d.shape
    half = attn.shape[1]
    ne = router_w.shape[1]
    nt = T // tm
    wo = w_out.astype(BF16)
    rw = router_w.T.astype(BF16)
    row = lambda i: (i, 0)
    full = lambda i: (0, 0)
    orow = lambda i: (first_tile + i, 0)
    otile = lambda i: (first_tile + i, 0, 0)
    extra = list(into) if into is not None else []
    return pl.pallas_call(
        functools.partial(_tail_kernel, n_valid=n_valid),
        grid=(nt,),
        in_specs=[pl.BlockSpec((tm, half), row), pl.BlockSpec((tm, half), row),
                  pl.BlockSpec((tm, D), row), pl.BlockSpec((D, D), full),
                  pl.BlockSpec((1, D), full), pl.BlockSpec((ne, D), full),
                  pl.BlockSpec((ne, 1), full)]
                 + [pl.BlockSpec(memory_space=pl.ANY)] * len(extra),
        out_specs=[pl.BlockSpec((tm, D), orow), pl.BlockSpec((tm, D), orow),
                   pl.BlockSpec((1, 2 * TOP_K, tm), otile),
                   pl.BlockSpec((1, ne, V7X_LANES), otile)],
        out_shape=[jax.ShapeDtypeStruct((total_tiles * tm, D), F32),
                   jax.ShapeDtypeStruct((total_tiles * tm, D), BF16),
                   jax.ShapeDtypeStruct((total_tiles, 2 * TOP_K, tm), F32),
                   jax.ShapeDtypeStruct((total_tiles, ne, V7X_LANES), I32)],
        input_output_aliases={7 + j: j for j in range(len(extra))},
        compiler_params=_cparams(("arbitrary",)),
        name="tail",
    )(attn, conv, x2d, wo, norm2_w[None, :], rw, router_b[:, None], *extra)


def _slots(tm, ne):
    worst = TOP_K * tm + ne * (CHUNK - 1)
    return -(-worst // V7X_LANES) * V7X_LANES


def _prefix_sum(x, axis, exclusive):
    n = x.shape[axis]
    i = jnp.arange(n)
    tri = (i[:, None] < i[None, :]) if exclusive else (i[:, None] <= i[None, :])
    xm = jnp.moveaxis(x, axis, -1)
    out = jnp.sum(xm[..., :, None] * tri.astype(x.dtype), axis=-2)
    return jnp.moveaxis(out, -1, axis)


def _moe_tables(cnt, nb, slot_chunks):
    nch = (cnt + (CHUNK - 1)) // CHUNK
    tot = jnp.sum(nch, axis=0)
    nblk = (tot + (CHUNKS_PER_BLOCK - 1)) // CHUNKS_PER_BLOCK
    bend = _prefix_sum(nblk, 0, exclusive=False)
    gstart = (bend - nblk) * CHUNKS_PER_BLOCK
    rs = gstart[None, :] + _prefix_sum(nch, 0, exclusive=True)
    tail_start = gstart + tot
    tail_n = nblk * CHUNKS_PER_BLOCK - tot
    nused = bend[-1:]
    blk = jnp.minimum(jnp.arange(nb, dtype=I32), nused - 1)
    blk_e = jnp.sum((bend[None, :] <= blk[:, None]).astype(I32), axis=1)
    cend = _prefix_sum(nch, 1, exclusive=False)
    c = jnp.arange(slot_chunks, dtype=I32)
    run = jnp.minimum(jnp.sum((cend[:, None, :] <= c[None, :, None]).astype(I32), axis=2),
                      nch.shape[1] - 1)
    shift = rs - (cend - nch)
    dest = c[None, :] + jnp.sum(jnp.where(run[:, :, None] == jnp.arange(nch.shape[1]),
                                          shift[:, None, :], 0), axis=2)
    i32 = lambda a: a.astype(I32)
    return (i32(cend[:, -1]), i32(dest), i32(tail_start), i32(tail_n), i32(nused), i32(blk_e))


def _chunk_rows(c):
    return pl.ds(pl.multiple_of(c * CHUNK, CHUNK), CHUNK)


def _for_each_chunk(n, fn):
    def body(c, carry):
        fn(c)
        return carry
    lax.fori_loop(0, n, body, 0)


WAIT_GROUP = 8


def _wait_chunks(n, copy_of_rows):
    _for_each_chunk(n // WAIT_GROUP, lambda c: copy_of_rows(WAIT_GROUP * CHUNK).wait())
    _for_each_chunk(n % WAIT_GROUP, lambda c: copy_of_rows(CHUNK).wait())


def _one_hot_rows(pos, nrows):
    r = lax.broadcasted_iota(I32, (nrows, pos.shape[1]), 0)
    p = jnp.zeros(r.shape, F32)
    for k in range(TOP_K):
        p = p + jnp.where(r == pos[k:k + 1], 1.0, 0.0)
    return p.astype(BF16)


def _dispatch_kernel(n_ref, dest_ref, ts_ref, tn_ref, h_ref, posg_ref, xs_hbm,
                     buf, zbuf, sem, zsem, *, ne):
    i = pl.program_id(0)
    slot = i % 2
    pos = posg_ref[0][0:TOP_K].astype(I32)
    buf[slot] = jnp.dot(_one_hot_rows(pos, buf.shape[1]), h_ref[...],
                        preferred_element_type=F32).astype(BF16)

    def run_copy(tile, c):
        s = tile % 2
        return pltpu.make_async_copy(buf.at[s, _chunk_rows(c)],
                                     xs_hbm.at[_chunk_rows(dest_ref[tile, c])], sem.at[s])

    def zero_copy(g):
        return pltpu.make_async_copy(zbuf, xs_hbm.at[_chunk_rows(g)], zsem)

    @pl.when(i == 0)
    def _():
        zbuf[...] = jnp.zeros(zbuf.shape, BF16)
        for phase in ("start", "wait"):
            def per_expert(e, carry):
                def body(j, c):
                    cp = zero_copy(ts_ref[e] + j)
                    cp.start() if phase == "start" else cp.wait()
                    return c
                return lax.fori_loop(0, tn_ref[e], body, carry)
            lax.fori_loop(0, ne, per_expert, 0)

    _for_each_chunk(n_ref[i], lambda c: run_copy(i, c).start())

    def wait_tile(tile):
        s = tile % 2
        _wait_chunks(n_ref[tile], lambda r: pltpu.make_async_copy(
            buf.at[s, 0:r], xs_hbm.at[0:r], sem.at[s]))

    @pl.when(i > 0)
    def _():
        wait_tile(i - 1)

    @pl.when(i == pl.num_programs(0) - 1)
    def _():
        wait_tile(i)


def _dispatch(h, posg, nchunks, dest, tail_start, tail_n, *, nb):
    T, D = h.shape
    nt, _, tm = posg.shape
    ne = tail_n.shape[0]
    return pl.pallas_call(
        functools.partial(_dispatch_kernel, ne=ne),
        grid_spec=pltpu.PrefetchScalarGridSpec(
            num_scalar_prefetch=4, grid=(nt,),
            in_specs=[pl.BlockSpec((tm, D), lambda i, *_: (i, 0)),
                      pl.BlockSpec((1, 2 * TOP_K, tm), lambda i, *_: (i, 0, 0))],
            out_specs=pl.BlockSpec(memory_space=pl.ANY),
            scratch_shapes=[pltpu.VMEM((2, _slots(tm, ne), D), BF16),
                            pltpu.VMEM((CHUNK, D), BF16),
                            pltpu.SemaphoreType.DMA((2,)), pltpu.SemaphoreType.DMA(())]),
        out_shape=jax.ShapeDtypeStruct((nb * MOE_BLOCK, D), BF16),
        compiler_params=_cparams(("arbitrary",)),
        name="dispatch",
    )(nchunks, dest, tail_start, tail_n, h, posg)


def _experts_kernel(be_ref, nu_ref, xs_ref, wgu_ref, bgu_ref, wd_ref, bd_ref, ys_ref,
                    wgu_s, wd_s):
    b = pl.program_id(0)

    @pl.when(b < nu_ref[0])
    def _():
        @pl.when((b == 0) | (be_ref[b] != be_ref[jnp.maximum(b - 1, 0)]))
        def _():
            wgu_s[...] = wgu_ref[0].astype(BF16)
            wd_s[...] = wd_ref[0].astype(BF16)

        ff = wd_s.shape[0]
        gu = jnp.dot(xs_ref[...], wgu_s[...], preferred_element_type=F32) + bgu_ref[0]
        g = jnp.minimum(gu[:, :ff], SWIGLU_LIMIT)
        u = jnp.clip(gu[:, ff:], -SWIGLU_LIMIT, SWIGLU_LIMIT)
        act = (u + 1.0) * g * jax.nn.sigmoid(SWIGLU_ALPHA * g)
        ys = jnp.dot(act.astype(BF16), wd_s[...], preferred_element_type=F32) + bd_ref[0]
        ys_ref[...] = ys.astype(ys_ref.dtype)


def _experts(xs, blk_e, nused, w_gate_up, b_gate_up, w_down, b_down):
    rows, D = xs.shape
    nb = rows // MOE_BLOCK
    ne, _, ff2 = w_gate_up.shape
    ff = w_down.shape[1]
    blk = lambda b, be, nu: (jnp.minimum(b, nu[0] - 1), 0)
    exp3 = lambda b, be, nu: (be[b], 0, 0)
    return pl.pallas_call(
        _experts_kernel,
        grid_spec=pltpu.PrefetchScalarGridSpec(
            num_scalar_prefetch=2, grid=(nb,),
            in_specs=[pl.BlockSpec((MOE_BLOCK, D), blk),
                      pl.BlockSpec((1, D, ff2), exp3), pl.BlockSpec((1, 1, ff2), exp3),
                      pl.BlockSpec((1, ff, D), exp3), pl.BlockSpec((1, 1, D), exp3)],
            out_specs=pl.BlockSpec((MOE_BLOCK, D), blk),
            scratch_shapes=[pltpu.VMEM((D, ff2), BF16), pltpu.VMEM((ff, D), BF16)]),
        out_shape=jax.ShapeDtypeStruct((rows, D), BF16),
        compiler_params=_cparams(("arbitrary",)),
        name="experts",
    )(blk_e, nused, xs, w_gate_up, b_gate_up[:, None, :], w_down, b_down[:, None, :])


def _combine_kernel(n_ref, dest_ref, ys_hbm, posg_ref, x1_ref, y_ref, ylast_ref, buf, sem):
    i = pl.program_id(0)
    last = pl.num_programs(0) - 1
    tm = x1_ref.shape[0]
    nslots = buf.shape[1]

    def run_copy(tile, c):
        s = tile % 2
        return pltpu.make_async_copy(ys_hbm.at[_chunk_rows(dest_ref[tile, c])],
                                     buf.at[s, _chunk_rows(c)], sem.at[s])

    def fetch(tile):
        _for_each_chunk(n_ref[tile], lambda c: run_copy(tile, c).start())
        s = tile % 2

        def zero_chunk(c, carry):
            buf[s, _chunk_rows(c), :] = jnp.zeros((CHUNK, buf.shape[2]), BF16)
            return carry
        lax.fori_loop(n_ref[tile], nslots // CHUNK, zero_chunk, 0)

    @pl.when(i == 0)
    def _():
        fetch(i)

    @pl.when(i < last)
    def _():
        fetch(i + 1)

    r0 = lax.broadcasted_iota(I32, (tm, tm), 0)
    r1 = lax.broadcasted_iota(I32, (tm, tm), 1)
    posg_t = _dot_nt(jnp.where(r0 == r1, 1.0, 0.0), posg_ref[0], exact=True)
    slot = lax.broadcasted_iota(I32, (tm, nslots), 1)
    w = jnp.zeros(slot.shape, F32)
    for k in range(TOP_K):
        w = w + jnp.where(slot == posg_t[:, k:k + 1].astype(I32),
                          posg_t[:, TOP_K + k:TOP_K + k + 1], 0.0)

    _wait_chunks(n_ref[i], lambda r: pltpu.make_async_copy(
        ys_hbm.at[0:r], buf.at[i % 2, 0:r], sem.at[i % 2]))
    y = x1_ref[...] + jnp.dot(w.astype(BF16), buf[i % 2], preferred_element_type=F32)

    @pl.when(i < last)
    def _():
        y_ref[...] = y

    @pl.when(i == last)
    def _():
        ylast_ref[...] = y


def _combine(ys, posg, x1, nchunks, dest):
    T, D = x1.shape
    nt, _, tm = posg.shape
    return pl.pallas_call(
        _combine_kernel,
        grid_spec=pltpu.PrefetchScalarGridSpec(
            num_scalar_prefetch=2, grid=(nt,),
            in_specs=[pl.BlockSpec(memory_space=pl.ANY),
                      pl.BlockSpec((1, 2 * TOP_K, tm), lambda i, *_: (i, 0, 0)),
                      pl.BlockSpec((tm, D), lambda i, *_: (i, 0))],
            out_specs=[pl.BlockSpec((tm, D), lambda i, *_: (jnp.minimum(i, nt - 2), 0)),
                       pl.BlockSpec((tm, D), lambda i, *_: (0, 0))],
            scratch_shapes=[pltpu.VMEM((2, dest.shape[1] * CHUNK, D), BF16),
                            pltpu.SemaphoreType.DMA((2,))]),
        out_shape=[jax.ShapeDtypeStruct((T - tm, D), F32), jax.ShapeDtypeStruct((tm, D), F32)],
        compiler_params=_cparams(("arbitrary",)),
        name="combine",
    )(nchunks, dest, ys, posg, x1)


def _moe(h, posg, cnt, x1, w_gate_up, b_gate_up, w_down, b_down):
    nt, _, tm = posg.shape
    ne = cnt.shape[1]
    max_chunks = (nt * tm * TOP_K) // CHUNK + nt * ne
    nb = -(-max_chunks // CHUNKS_PER_BLOCK) + ne
    nchunks, dest, tail_start, tail_n, nused, blk_e = _moe_tables(cnt, nb, _slots(tm, ne) // CHUNK)
    xs = _dispatch(h, posg, nchunks, dest, tail_start, tail_n, nb=nb)
    ys = _experts(xs, blk_e, nused, w_gate_up, b_gate_up, w_down, b_down)
    return _combine(ys, posg, x1, nchunks, dest)


def kernel(x_prompt, x_sample, cache_k, cache_v, state_conv, page_table, norm1_w, w_in,
           q_norm_w, k_norm_w, lambda_q1, lambda_k1, lambda_q2, lambda_k2, subln_w,
           conv_w, conv_b, conv_ln_g, conv_ln_b, w_out, norm2_w, router_w, router_b,
           w_gate_up, b_gate_up, w_down, b_down):
    B, S, D = x_prompt.shape
    Bs, Ss, _ = x_sample.shape
    depth = norm1_w.shape[0]
    n_phys, page, heads, _, qk = cache_k.shape[1:]
    vdim = cache_v.shape[-1]
    qc, vc, cc = heads * 2 * qk, heads * vdim, conv_w.shape[2]
    taps = conv_w.shape[1]
    assert Ss == 1 and qk == QK_GROUP and vdim == V7X_LANES and (B * S) % MOE_TILE == 0
    assert Bs <= MOE_TILE and S >= taps - 1
    n_past = page_table.shape[1] * page
    T = B * S
    nt_p = T // MOE_TILE
    pos_p = jnp.arange(S, dtype=F32)
    pos_s = jnp.full((Bs,), n_past, F32)
    xp = x_prompt.reshape(T, D)
    xs = x_sample.reshape(Bs, D)
    pad_tile = lambda a: jnp.pad(a, ((0, MOE_TILE - Bs), (0, 0)))
    outs = [[] for _ in range(6)]
    for l in range(depth):
        lam_init = 0.8 - 0.6 * math.exp(-0.3 * l)
        lamv = jnp.stack([lambda_q1[l], lambda_k1[l], lambda_q2[l], lambda_k2[l]])
        conv_p = (conv_w[l], conv_b[l], conv_ln_g[l], conv_ln_b[l])
        proj_p = (norm1_w[l], w_in[l], q_norm_w[l], k_norm_w[l])
        tail_p = (w_out[l], norm2_w[l], router_w[l], router_b[l])

        q, kt, v, u, kbt, vb = _proj(xp, pos_p, S // PROJ_TILE, PROJ_TILE, *proj_p,
                                     qc=qc, vc=vc, cc=cc, exact_norm=False, attn_layout=True)
        qs, ks_, vs_, us = _proj(xs, pos_s, 1, Bs, *proj_p, qc=qc, vc=vc, cc=cc,
                                 exact_norm=True, attn_layout=False)
        cache_kt = jnp.transpose(cache_k[l], (0, 2, 3, 4, 1)).reshape(n_phys, qc, page)
        cache_vr = cache_v[l].reshape(n_phys, page * heads, vdim)
        attn, attn_s = _attention(q, kbt, vb, qs, ks_, vs_, cache_kt, cache_vr, page_table, lamv,
                                  subln_w[l], lam_init=lam_init)

        conv = _conv_prompt(u, *conv_p, batch=B, seq=S)
        bufs = _tail(attn, conv, xp, *tail_p, tm=MOE_TILE, n_valid=MOE_TILE,
                     total_tiles=nt_p + 1)
        outs[0].append(jnp.transpose(kt.reshape(B, heads, 2, qk, S), (0, 4, 1, 2, 3)))
        outs[1].append(v.reshape(B, S, heads, vdim))
        outs[2].append(u.reshape(B, S, cc)[:, S - (taps - 1):])

        conv_s = _conv_step(state_conv[l], us, *conv_p)
        x1, h, posg, cnt = _tail(pad_tile(attn_s), pad_tile(conv_s), pad_tile(xs), *tail_p,
                                 tm=MOE_TILE, n_valid=Bs, total_tiles=nt_p + 1,
                                 first_tile=nt_p, into=bufs)
        outs[3].append(ks_.reshape(Bs, Ss, heads, 2, qk))
        outs[4].append(vs_.reshape(Bs, Ss, heads, vdim))
        outs[5].append(jnp.concatenate([state_conv[l][:, Ss:], us[:, None, :]], axis=1))

        xp, y_last = _moe(h, posg, cnt[:, :, 0], x1, w_gate_up[l], b_gate_up[l], w_down[l],
                          b_down[l])
        xs = y_last[:Bs]
    return (xp.reshape(B, S, D), xs.reshape(Bs, Ss, D)) + tuple(jnp.stack(o) for o in outs)
```

```python
import functools
import math

import jax
import jax.numpy as jnp
from jax import lax
from jax.experimental import pallas as pl
from jax.experimental.pallas import tpu as pltpu

F32 = jnp.float32
BF16 = jnp.bfloat16
I32 = jnp.int32
HIGHEST = lax.Precision.HIGHEST

EPS = 1e-6
ROPE_THETA = 10000.0
SWIGLU_LIMIT = 7.0
SWIGLU_ALPHA = 1.702
TOP_K = 4
NEG = -1e30
QK_GROUP = 64

V7X_LANES = 128
V7X_SUBLANES = 8
VMEM_LIMIT = 56 * 1024 * 1024
BF16_ROWS = 16

MOE_TILE = 256
CHUNK = BF16_ROWS
MOE_BLOCK = 512
CHUNKS_PER_BLOCK = MOE_BLOCK // CHUNK
PROJ_TILE = 512


def _cparams(sem, vmem=VMEM_LIMIT):
    return pltpu.CompilerParams(dimension_semantics=sem, vmem_limit_bytes=vmem)


def _dot(a, b, exact=False):
    if exact:
        return jnp.dot(a.astype(F32), b.astype(F32), precision=HIGHEST,
                       preferred_element_type=F32)
    return jnp.dot(a.astype(BF16), b.astype(BF16), preferred_element_type=F32)


def _dot_nt(a, b, exact=False):
    dn = (((1,), (1,)), ((), ()))
    if exact:
        return lax.dot_general(a.astype(F32), b.astype(F32), dn, precision=HIGHEST,
                               preferred_element_type=F32)
    return lax.dot_general(a.astype(BF16), b.astype(BF16), dn, preferred_element_type=F32)


def _bf16_round(x):
    return x.astype(BF16).astype(F32)


def _rope_norm(p, gsum, w, cos, sin, first_half, exact_norm):
    ss = _dot(p * p, gsum, exact_norm)
    n = p * lax.rsqrt(ss * (1.0 / QK_GROUP) + EPS) * w
    outs = []
    for j in range(p.shape[1] // V7X_LANES):
        nj = n[:, j * V7X_LANES:(j + 1) * V7X_LANES]
        rot = jnp.where(first_half, pltpu.roll(nj, V7X_LANES - QK_GROUP // 2, 1),
                        pltpu.roll(nj, QK_GROUP // 2, 1))
        outs.append(nj * cos + rot * sin)
    return jnp.concatenate(outs, axis=1)


def _proj_kernel(x_ref, n1_ref, w_ref, qw_ref, kw_ref, cos_ref, sin_ref, gsum_ref,
                 q_ref, k_ref, v_ref, u_ref, *rest, qc, vc, cc, scale, exact_norm):
    x = x_ref[...]
    h = x * lax.rsqrt(jnp.mean(x * x, axis=-1, keepdims=True) + EPS) * n1_ref[...]
    hm = h.astype(BF16)
    cos = cos_ref[...]
    sin = sin_ref[...]
    lane = lax.broadcasted_iota(I32, cos.shape, 1)
    first_half = (lane % QK_GROUP) < QK_GROUP // 2
    gsum = gsum_ref[...]

    q = _rope_norm(_dot(hm, w_ref[:, 0:qc]), gsum, qw_ref[...], cos, sin, first_half, exact_norm)
    q_ref[...] = (q * scale).astype(q_ref.dtype)
    k = _rope_norm(_dot(hm, w_ref[:, qc:2 * qc]), gsum, kw_ref[...], cos, sin, first_half,
                   exact_norm)
    v = _dot(hm, w_ref[:, 2 * qc:2 * qc + vc])
    o = 2 * qc + vc
    ua = _dot(hm, w_ref[:, o:o + cc])
    ub = _dot(hm, w_ref[:, o + cc:o + 2 * cc])
    u_ref[...] = ua * jax.nn.sigmoid(ub)
    if rest:
        kb_ref, vb_ref = rest
        kt = k.T
        k_ref[0] = kt
        kb_ref[0, :, 0] = kt.astype(BF16).reshape(kb_ref.shape[1], kb_ref.shape[3], kt.shape[1])
        vb_ref[...] = v.astype(BF16)
        nh = vc // V7X_LANES
        for h in range(nh):
            v_ref[pl.ds(h, v.shape[0], stride=nh), :] = v[:, h * V7X_LANES:(h + 1) * V7X_LANES]
    else:
        k_ref[...] = k
        v_ref[...] = v


def _rope_tables(pos):
    half = QK_GROUP // 2
    inv = jnp.power(ROPE_THETA, -jnp.arange(half, dtype=F32) / half)
    ang = pos[:, None] * inv[None, :]
    reps = V7X_LANES // QK_GROUP
    cos = jnp.tile(jnp.cos(ang), (1, 2 * reps))
    s = jnp.sin(ang)
    sin = jnp.tile(jnp.concatenate([-s, s], axis=1), (1, reps))
    return cos, sin


def _proj(x2d, pos_rows, n_pos_blocks, tm, norm1_w, w_in, q_norm_w, k_norm_w, *, qc, vc, cc,
          exact_norm, attn_layout):
    T, D = x2d.shape
    cos, sin = _rope_tables(pos_rows)
    gi = jnp.arange(qc) // QK_GROUP
    gsum = (gi[:, None] == gi[None, :]).astype(F32 if exact_norm else BF16)
    qw = jnp.tile(q_norm_w, qc // QK_GROUP)[None, :]
    kw = jnp.tile(k_norm_w, qc // QK_GROUP)[None, :]
    w = w_in.astype(BF16)
    row = lambda i: (i, 0)
    full = lambda i: (0, 0)
    out_shape = [jax.ShapeDtypeStruct((T, qc), BF16),
                 jax.ShapeDtypeStruct((T, qc), F32),
                 jax.ShapeDtypeStruct((T, vc), F32),
                 jax.ShapeDtypeStruct((T, cc), F32)]
    out_specs = [pl.BlockSpec((tm, qc), row), pl.BlockSpec((tm, qc), row),
                 pl.BlockSpec((tm, vc), row), pl.BlockSpec((tm, cc), row)]
    if attn_layout:
        nseq = T // (n_pos_blocks * tm)
        heads = qc // (2 * QK_GROUP)
        seq_tile = lambda i: (i // n_pos_blocks, 0, i % n_pos_blocks)
        out_shape[1] = jax.ShapeDtypeStruct((nseq, qc, n_pos_blocks * tm), F32)
        out_specs[1] = pl.BlockSpec((1, qc, tm), seq_tile)
        out_shape[2] = jax.ShapeDtypeStruct((T * vc // V7X_LANES, V7X_LANES), F32)
        out_specs[2] = pl.BlockSpec((tm * vc // V7X_LANES, V7X_LANES), row)
        out_shape += [jax.ShapeDtypeStruct((nseq, heads, n_pos_blocks, 2 * QK_GROUP, tm), BF16),
                      jax.ShapeDtypeStruct((T, vc), BF16)]
        out_specs += [pl.BlockSpec((1, heads, 1, 2 * QK_GROUP, tm),
                                   lambda i: (i // n_pos_blocks, 0, i % n_pos_blocks, 0, 0)),
                      pl.BlockSpec((tm, vc), row)]
    return pl.pallas_call(
        functools.partial(_proj_kernel, qc=qc, vc=vc, cc=cc, scale=QK_GROUP ** -0.5,
                          exact_norm=exact_norm),
        grid=(T // tm,),
        in_specs=[pl.BlockSpec((tm, D), row),
                  pl.BlockSpec((1, D), full),
                  pl.BlockSpec(w.shape, full),
                  pl.BlockSpec((1, qc), full),
                  pl.BlockSpec((1, qc), full),
                  pl.BlockSpec((tm, V7X_LANES), lambda i: (i % n_pos_blocks, 0)),
                  pl.BlockSpec((tm, V7X_LANES), lambda i: (i % n_pos_blocks, 0)),
                  pl.BlockSpec((qc, qc), full)],
        out_specs=out_specs,
        out_shape=out_shape,
        compiler_params=_cparams(("arbitrary",)),
        name="proj" if attn_layout else "proj_step",
    )(x2d, norm1_w[None, :], w, qw, kw, cos, sin, gsum)


def _lambda_value(lv, lam_init):
    a = jnp.sum(lv[0:1] * lv[1:2], axis=-1, keepdims=True)
    b = jnp.sum(lv[2:3] * lv[3:4], axis=-1, keepdims=True)
    return jnp.exp(a) - jnp.exp(b) + lam_init


def _subln(o, w, lam_init):
    y = o * lax.rsqrt(jnp.mean(o * o, axis=-1, keepdims=True) + EPS)
    return y * w * (1.0 - lam_init)


def _attn_body(i, lam_ref, sw_ref, q_ref, k_ref, v_ref, o_ref, *, tq, lam_init):
    lam = _lambda_value(lam_ref[...], lam_init)
    q = q_ref[...]
    lane = lax.broadcasted_iota(I32, q.shape, 1)
    zero = jnp.zeros_like(q)
    qs = (jnp.where(lane < QK_GROUP, q, zero), jnp.where(lane >= QK_GROUP, q, zero))

    def chunk(j, carry, masked):
        kc = k_ref[0, 0, j]
        vc = v_ref[pl.ds(pl.multiple_of(j * tq, tq), tq), :]
        out = []
        for c in range(2):
            m, l, acc = carry[c]
            s = jnp.dot(qs[c], kc, preferred_element_type=F32)
            if masked:
                row = lax.broadcasted_iota(I32, s.shape, 0)
                col = lax.broadcasted_iota(I32, s.shape, 1)
                s = jnp.where(col <= row, s, NEG)
            m_new = jnp.maximum(m, jnp.max(s, axis=-1, keepdims=True))
            p = jnp.exp(s - m_new)
            alpha = jnp.exp(m - m_new)
            l = alpha * l + jnp.sum(p, axis=-1, keepdims=True)
            acc = alpha * acc + _dot(p, vc)
            out.append((m_new, l, acc))
        return tuple(out)

    init = tuple((jnp.full((tq, 1), NEG, F32), jnp.zeros((tq, 1), F32),
                  jnp.zeros((tq, V7X_LANES), F32)) for _ in range(2))
    carry = lax.fori_loop(0, i, lambda j, c: chunk(j, c, False), init)
    (_, l0, a0), (_, l1, a1) = chunk(i, carry, True)
    o = a0 / l0 - lam * (a1 / l1)
    o_ref[...] = _subln(o, sw_ref[...], lam_init).astype(o_ref.dtype)


DECODE_PAGES_PER_STEP = 32
SOFTMAX_PAGES = 16
PAGE_RING_DEPTH = 3


def _decode_body(s, half_steps, lam_ref, sw_ref, qm_ref, kn_ref, vn_ref, pages_ref,
                 o_ref, s_ref, m_ref, coef_ref, acc_ref, *, page, heads, lam_init):
    pps = pages_ref.shape[0]
    rows = s_ref.shape[1]
    qm = qm_ref[0]
    row = lax.broadcasted_iota(I32, (rows, V7X_LANES), 0)
    n_pages = half_steps * pps

    @pl.when(s == 0)
    def _():
        m_ref[...] = jnp.full(m_ref.shape, NEG, F32)

    @pl.when(s < half_steps)
    def _():
        m = m_ref[...]
        for j in range(pps):
            sc = jnp.dot(qm, pages_ref[j].astype(BF16), preferred_element_type=F32)
            s_ref[s * pps + j] = sc
            m = jnp.maximum(m, sc)
        m_ref[...] = m

    def head_weights(p):
        a = p * coef_ref[...]
        return _bf16_round(a + pltpu.roll(a, rows - heads, 0))

    @pl.when(s == half_steps)
    def _():
        s_new = jnp.sum(qm.astype(F32) * _bf16_round(kn_ref[0]), axis=-1, keepdims=True)
        m = jnp.maximum(jnp.max(m_ref[...], axis=-1, keepdims=True), s_new)

        def exp_pages(t, l):
            pages = pl.ds(pl.multiple_of(t * SOFTMAX_PAGES, SOFTMAX_PAGES), SOFTMAX_PAGES)
            p = jnp.exp(s_ref[pages] - m[None])
            s_ref[pages] = p
            return l + jnp.sum(p, axis=0)

        lsum = lax.fori_loop(0, n_pages // SOFTMAX_PAGES, exp_pages,
                             jnp.zeros((rows, V7X_LANES), F32))
        p_new = jnp.exp(s_new - m)
        l = jnp.sum(lsum, axis=-1, keepdims=True) + p_new
        lam = _lambda_value(lam_ref[...], lam_init)
        coef = jnp.where(row[:, 0:1] < heads, 1.0, -lam) / l
        coef_ref[...] = jnp.broadcast_to(coef, coef_ref.shape)
        acc_ref[...] = (head_weights(jnp.broadcast_to(p_new, (rows, V7X_LANES)))
                        * _bf16_round(vn_ref[0]))

    @pl.when(s >= half_steps)
    def _():
        lane = lax.broadcasted_iota(I32, (rows, V7X_LANES), 1)
        keep = (lane % heads == row) & (row < heads)
        acc = acc_ref[...]
        for j in range(pps):
            a = head_weights(s_ref[(s - half_steps) * pps + j])
            parts = []
            for c in range(heads):
                idx = (c * page + lane) // heads
                parts.append(jnp.where(keep, jnp.take_along_axis(a, idx, axis=1), 0.0))
            a_exp = jnp.concatenate(parts, axis=1).astype(BF16)
            acc = acc + jnp.dot(a_exp, pages_ref[j].astype(BF16), preferred_element_type=F32)
        acc_ref[...] = acc

    @pl.when(s == 2 * half_steps - 1)
    def _():
        o_ref[0] = _subln(acc_ref[...], sw_ref[...], lam_init)


def _attention_kernel(pt_ref, lam_ref, sw_ref, q_ref, k_ref, v_ref, qm_ref, kn_ref, vn_ref,
                      kt_hbm, vr_hbm, o_ref, od_ref, s_ref, m_ref, coef_ref, acc_ref, pages, sem,
                      *, ratio, nq, steps_per_seq, tq, page, heads, lam_init):
    t = pl.program_id(0)
    depth, pps = pages.shape[:2]
    half_steps = steps_per_seq // 2

    def fetch(step):
        seq = step // steps_per_seq
        s = step % steps_per_seq
        first = (s % half_steps) * pps
        slot = step % depth
        for src_hbm, cond in ((kt_hbm, s < half_steps), (vr_hbm, s >= half_steps)):
            @pl.when(cond)
            def _():
                def start(j, carry):
                    pltpu.make_async_copy(src_hbm.at[pt_ref[seq, first + j]],
                                          pages.at[slot, j], sem.at[slot]).start()
                    return carry
                lax.fori_loop(0, pps, start, 0)

    @pl.when(t == 0)
    def _():
        for ahead in range(depth - 1):
            fetch(t + ahead)

    @pl.when(t + depth - 1 < pl.num_programs(0))
    def _():
        fetch(t + depth - 1)

    slot = t % depth
    pltpu.make_async_copy(kt_hbm.at[0:pps], pages.at[slot], sem.at[slot]).wait()
    _decode_body(t % steps_per_seq, half_steps, lam_ref, sw_ref, qm_ref, kn_ref, vn_ref,
                 pages.at[slot], od_ref, s_ref, m_ref, coef_ref, acc_ref,
                 page=page, heads=heads, lam_init=lam_init)

    @pl.when(t % ratio == 0)
    def _():
        _attn_body((t // ratio) % nq, lam_ref, sw_ref, q_ref, k_ref, v_ref, o_ref,
                   tq=tq, lam_init=lam_init)


def _attention(q, kbt, vb, q_step, k_new, v_new, cache_kt, cache_vr, page_table, lamv, subln_w,
               *, lam_init):
    batch, heads, nq, _, tq = kbt.shape
    seq = nq * tq
    Bs, D = q_step.shape
    n_pages = page_table.shape[1]
    page = cache_kt.shape[2]
    vdim = cache_vr.shape[2]
    pps = DECODE_PAGES_PER_STEP
    half_steps = n_pages // pps
    steps_per_seq = 2 * half_steps
    rows = 2 * heads
    n_dec, n_att = Bs * steps_per_seq, batch * heads * nq
    assert rows == V7X_SUBLANES and page == V7X_LANES and vdim == V7X_LANES
    assert n_pages % pps == 0 and n_pages % SOFTMAX_PAGES == 0 and n_dec % n_att == 0
    assert cache_kt.shape[1:] == cache_vr.shape[1:]
    ratio = n_dec // n_att
    group = jnp.arange(D) // QK_GROUP
    rowmask = ((group % 2) * heads + group // 2)[None, :] == jnp.arange(rows)[:, None]
    qm = jnp.where(rowmask[None], q_step[:, None, :], jnp.zeros((), BF16))
    vn = jnp.pad(v_new.reshape(Bs, heads, vdim), ((0, 0), (0, rows - heads), (0, 0)))

    def att(t):
        a = t // ratio
        return a // (heads * nq), (a // nq) % heads, a % nq

    def q_map(t, pt):
        b, h, i = att(t)
        return b * nq + i, h

    seq3 = lambda t, pt: (t // steps_per_seq, 0, 0)
    const = lambda t, pt: (0, 0)
    out, out_step = pl.pallas_call(
        functools.partial(_attention_kernel, ratio=ratio, nq=nq,
                          steps_per_seq=steps_per_seq, tq=tq, page=page, heads=heads,
                          lam_init=lam_init),
        grid_spec=pltpu.PrefetchScalarGridSpec(
            num_scalar_prefetch=1, grid=(n_dec,),
            in_specs=[pl.BlockSpec(lamv.shape, const), pl.BlockSpec((1, vdim), const),
                      pl.BlockSpec((tq, V7X_LANES), q_map),
                      pl.BlockSpec((1, 1, nq, V7X_LANES, tq),
                                   lambda t, pt: att(t)[:2] + (0, 0, 0)),
                      pl.BlockSpec((seq, V7X_LANES), lambda t, pt: att(t)[:2]),
                      pl.BlockSpec((1, rows, D), seq3), pl.BlockSpec((1, 1, D), seq3),
                      pl.BlockSpec((1, rows, vdim), seq3),
                      pl.BlockSpec(memory_space=pl.ANY), pl.BlockSpec(memory_space=pl.ANY)],
            out_specs=[pl.BlockSpec((tq, V7X_LANES), q_map),
                       pl.BlockSpec((1, rows, vdim), seq3)],
            scratch_shapes=[pltpu.VMEM((n_pages, rows, page), F32),
                            pltpu.VMEM((rows, V7X_LANES), F32),
                            pltpu.VMEM((rows, V7X_LANES), F32),
                            pltpu.VMEM((rows, vdim), F32),
                            pltpu.VMEM((PAGE_RING_DEPTH, pps) + cache_kt.shape[1:], F32),
                            pltpu.SemaphoreType.DMA((PAGE_RING_DEPTH,))]),
        out_shape=[jax.ShapeDtypeStruct((batch * seq, heads * V7X_LANES), BF16),
                   jax.ShapeDtypeStruct((Bs, rows, vdim), F32)],
        compiler_params=_cparams(("arbitrary",)),
        name="attention",
    )(page_table, lamv, subln_w[None, :], q, kbt, vb, qm, k_new[:, None, :], vn,
      cache_kt, cache_vr)
    return out, out_step[:, :heads, :].reshape(Bs, heads * vdim)


CONV_HALO = 32


def _ln_swish(y, b_ref, g_ref, be_ref):
    y = y + b_ref[...]
    mu = jnp.mean(y, axis=-1, keepdims=True)
    yc = y - mu
    z = yc * lax.rsqrt(jnp.mean(yc * yc, axis=-1, keepdims=True) + EPS) * g_ref[...] + be_ref[...]
    return z * jax.nn.sigmoid(z)


def _conv_kernel(u_ref, w_ref, b_ref, g_ref, be_ref, o_ref, buf_ref, part_ref, *, tc, taps):
    sub = V7X_SUBLANES
    first = CONV_HALO - (taps - 1)

    @pl.when(pl.program_id(1) == 0)
    def _():
        buf_ref[0:CONV_HALO, :] = jnp.zeros((CONV_HALO, buf_ref.shape[1]), F32)
        buf_ref[CONV_HALO + tc:, :] = jnp.zeros((sub, buf_ref.shape[1]), F32)

    buf_ref[CONV_HALO:CONV_HALO + tc, :] = _bf16_round(u_ref[...])
    acc = None
    for r in range(sub):
        part = None
        for a in range(-(-(first + taps) // sub)):
            k = sub * a + r - first
            if 0 <= k < taps:
                term = buf_ref[sub * a:sub * a + tc + sub, :] * _bf16_round(w_ref[k:k + 1, :])
                part = term if part is None else part + term
        if r == 0:
            acc = part[0:tc]
        else:
            part_ref[...] = part
            acc = acc + part_ref[r:r + tc, :]
    o_ref[...] = _ln_swish(acc, b_ref, g_ref, be_ref).astype(o_ref.dtype)
    buf_ref[0:CONV_HALO, :] = buf_ref[tc:tc + CONV_HALO, :]


def _conv_prompt(u2d, conv_w, conv_b, ln_g, ln_b, *, batch, seq, tc=512):
    taps, C = conv_w.shape
    nt = seq // tc
    vec = lambda b, i: (0, 0)
    return pl.pallas_call(
        functools.partial(_conv_kernel, tc=tc, taps=taps),
        grid=(batch, nt),
        in_specs=[pl.BlockSpec((tc, C), lambda b, i: (b * nt + i, 0)),
                  pl.BlockSpec((taps, C), vec), pl.BlockSpec((1, C), vec),
                  pl.BlockSpec((1, C), vec), pl.BlockSpec((1, C), vec)],
        out_specs=pl.BlockSpec((tc, C), lambda b, i: (b * nt + i, 0)),
        out_shape=jax.ShapeDtypeStruct((batch * seq, C), BF16),
        scratch_shapes=[pltpu.VMEM((tc + CONV_HALO + V7X_SUBLANES, C), F32),
                        pltpu.VMEM((tc + V7X_SUBLANES, C), F32)],
        compiler_params=_cparams(("arbitrary", "arbitrary")),
        name="conv",
    )(u2d, conv_w, conv_b[None, :], ln_g[None, :], ln_b[None, :])


def _conv_step_kernel(st_ref, u_ref, w_ref, b_ref, g_ref, be_ref, o_ref, *, taps):
    acc = u_ref[...] * w_ref[taps - 1:taps, :]
    for k in range(taps - 1):
        acc = acc + st_ref[:, k, :] * w_ref[k:k + 1, :]
    o_ref[...] = _ln_swish(acc, b_ref, g_ref, be_ref)


def _conv_step(state, u, conv_w, conv_b, ln_g, ln_b):
    taps, C = conv_w.shape
    return pl.pallas_call(
        functools.partial(_conv_step_kernel, taps=taps),
        out_shape=jax.ShapeDtypeStruct(u.shape, F32),
        compiler_params=_cparams(None),
        name="conv_step",
    )(state, u, conv_w, conv_b[None, :], ln_g[None, :], ln_b[None, :])


def _tail_kernel(a_ref, c_ref, x_ref, wo_ref, n2_ref, rw_ref, rb_ref, *rest, n_valid):
    x1_ref, h_ref, posg_ref, cnt_ref = rest[-4:]
    tm = x_ref.shape[0]
    half = a_ref.shape[1]
    x1 = (x_ref[...] + _dot(a_ref[...], wo_ref[0:half, :])
          + _dot(c_ref[...], wo_ref[half:, :]))
    x1_ref[...] = x1
    h = (x1 * lax.rsqrt(jnp.mean(x1 * x1, axis=-1, keepdims=True) + EPS)
         * n2_ref[...]).astype(BF16)
    h_ref[...] = h
    logits = _dot_nt(rw_ref[...], h) + rb_ref[...]
    ne = logits.shape[0]
    eidx = lax.broadcasted_iota(I32, logits.shape, 0)
    valid = lax.broadcasted_iota(I32, (1, tm), 1) < n_valid

    sels, vals = [], []
    l = logits
    for _ in range(TOP_K):
        m = jnp.max(l, axis=0, keepdims=True)
        first = jnp.min(jnp.where(l == m, eidx, ne), axis=0, keepdims=True)
        sel = (eidx == first) & valid
        l = jnp.where(eidx == first, -jnp.inf, l)
        sels.append(sel)
        vals.append(m)
    ex = [jnp.exp(v - vals[0]) for v in vals]
    den = ex[0] + ex[1] + ex[2] + ex[3]
    gates = [jnp.where(valid, e / den, 0.0) for e in ex]

    msel = jnp.zeros(logits.shape, F32)
    for sel in sels:
        msel = msel + jnp.where(sel, 1.0, 0.0)
    r0 = lax.broadcasted_iota(I32, (tm, tm), 0)
    r1 = lax.broadcasted_iota(I32, (tm, tm), 1)
    upper = jnp.where(r0 < r1, 1.0, 0.0).astype(BF16)
    rank = jnp.dot(msel.astype(BF16), upper, preferred_element_type=F32)
    cnt = jnp.sum(msel, axis=1, keepdims=True)
    pcnt = jnp.ceil(cnt * (1.0 / CHUNK)) * CHUNK
    e0 = lax.broadcasted_iota(I32, (ne, ne), 0)
    e1 = lax.broadcasted_iota(I32, (ne, ne), 1)
    lower = jnp.where(e1 < e0, 1.0, 0.0)
    off = jnp.dot(lower.astype(BF16), jnp.broadcast_to(pcnt, (ne, V7X_LANES)).astype(BF16),
                  preferred_element_type=F32)[:, 0:1]
    pos = off + rank
    rows = [jnp.where(valid, jnp.sum(jnp.where(sel, pos, 0.0), axis=0, keepdims=True), -1.0)
            for sel in sels]
    posg_ref[0] = jnp.concatenate(rows + gates, axis=0)
    cnt_ref[0] = jnp.broadcast_to(cnt, (ne, V7X_LANES)).astype(I32)


def _tail(attn, conv, x2d, w_out, norm2_w, router_w, router_b, *, tm, n_valid, total_tiles,
          first_tile=0, into=None):
    T, D = x2d.shape
    half = attn.shape[1]
    ne = router_w.shape[1]
    nt = T // tm
    wo = w_out.astype(BF16)
    rw = router_w.T.astype(BF16)
    row = lambda i: (i, 0)
    full = lambda i: (0, 0)
    orow = lambda i: (first_tile + i, 0)
    otile = lambda i: (first_tile + i, 0, 0)
    extra = list(into) if into is not None else []
    return pl.pallas_call(
        functools.partial(_tail_kernel, n_valid=n_valid),
        grid=(nt,),
        in_specs=[pl.BlockSpec((tm, half), row), pl.BlockSpec((tm, half), row),
                  pl.BlockSpec((tm, D), row), pl.BlockSpec((D, D), full),
                  pl.BlockSpec((1, D), full), pl.BlockSpec((ne, D), full),
                  pl.BlockSpec((ne, 1), full)]
                 + [pl.BlockSpec(memory_space=pl.ANY)] * len(extra),
        out_specs=[pl.BlockSpec((tm, D), orow), pl.BlockSpec((tm, D), orow),
                   pl.BlockSpec((1, 2 * TOP_K, tm), otile),
                   pl.BlockSpec((1, ne, V7X_LANES), otile)],
        out_shape=[jax.ShapeDtypeStruct((total_tiles * tm, D), F32),
                   jax.ShapeDtypeStruct((total_tiles * tm, D), BF16),
                   jax.ShapeDtypeStruct((total_tiles, 2 * TOP_K, tm), F32),
                   jax.ShapeDtypeStruct((total_tiles, ne, V7X_LANES), I32)],
        input_output_aliases={7 + j: j for j in range(len(extra))},
        compiler_params=_cparams(("arbitrary",)),
        name="tail",
    )(attn, conv, x2d, wo, norm2_w[None, :], rw, router_b[:, None], *extra)


def _slots(tm, ne):
    worst = TOP_K * tm + ne * (CHUNK - 1)
    return -(-worst // V7X_LANES) * V7X_LANES


def _prefix_sum(x, axis, exclusive):
    n = x.shape[axis]
    i = jnp.arange(n)
    tri = (i[:, None] < i[None, :]) if exclusive else (i[:, None] <= i[None, :])
    xm = jnp.moveaxis(x, axis, -1)
    out = jnp.sum(xm[..., :, None] * tri.astype(x.dtype), axis=-2)
    return jnp.moveaxis(out, -1, axis)


def _moe_tables(cnt, nb, slot_chunks):
    nch = (cnt + (CHUNK - 1)) // CHUNK
    tot = jnp.sum(nch, axis=0)
    nblk = (tot + (CHUNKS_PER_BLOCK - 1)) // CHUNKS_PER_BLOCK
    bend = _prefix_sum(nblk, 0, exclusive=False)
    gstart = (bend - nblk) * CHUNKS_PER_BLOCK
    rs = gstart[None, :] + _prefix_sum(nch, 0, exclusive=True)
    tail_start = gstart + tot
    tail_n = nblk * CHUNKS_PER_BLOCK - tot
    nused = bend[-1:]
    blk = jnp.minimum(jnp.arange(nb, dtype=I32), nused - 1)
    blk_e = jnp.sum((bend[None, :] <= blk[:, None]).astype(I32), axis=1)
    cend = _prefix_sum(nch, 1, exclusive=False)
    c = jnp.arange(slot_chunks, dtype=I32)
    run = jnp.minimum(jnp.sum((cend[:, None, :] <= c[None, :, None]).astype(I32), axis=2),
                      nch.shape[1] - 1)
    shift = rs - (cend - nch)
    dest = c[None, :] + jnp.sum(jnp.where(run[:, :, None] == jnp.arange(nch.shape[1]),
                                          shift[:, None, :], 0), axis=2)
    ne = nblk.shape[0]
    eid = jnp.arange(ne, dtype=I32)
    later = jnp.where((eid[None, :] > eid[:, None]) & (nblk[None, :] > 0), eid[None, :], ne)
    next_e = jnp.min(later, axis=1)
    next_e = jnp.where(next_e == ne, -1, next_e)
    rank_e = _prefix_sum((nblk > 0).astype(I32), 0, exclusive=True)
    bidx = jnp.arange(nb, dtype=I32)
    bstart = bend - nblk
    first = (bidx == bstart[blk_e]) & (bidx < nused)
    last = (bidx == bend[blk_e] - 1) & (bidx < nused)
    edge = first.astype(I32) + 2 * last.astype(I32) + 4 * (rank_e[blk_e] % 2)
    i32 = lambda a: a.astype(I32)
    return (i32(cend[:, -1]), i32(dest), i32(tail_start), i32(tail_n), i32(nused), i32(blk_e),
            i32(next_e[blk_e]), i32(edge))


def _chunk_rows(c):
    return pl.ds(pl.multiple_of(c * CHUNK, CHUNK), CHUNK)


def _for_each_chunk(n, fn):
    def body(c, carry):
        fn(c)
        return carry
    lax.fori_loop(0, n, body, 0)


WAIT_GROUP = 8


def _wait_chunks(n, copy_of_rows):
    _for_each_chunk(n // WAIT_GROUP, lambda c: copy_of_rows(WAIT_GROUP * CHUNK).wait())
    _for_each_chunk(n % WAIT_GROUP, lambda c: copy_of_rows(CHUNK).wait())


def _one_hot_rows(pos, nrows):
    r = lax.broadcasted_iota(I32, (nrows, pos.shape[1]), 0)
    p = jnp.zeros(r.shape, F32)
    for k in range(TOP_K):
        p = p + jnp.where(r == pos[k:k + 1], 1.0, 0.0)
    return p.astype(BF16)


def _dispatch_kernel(n_ref, dest_ref, ts_ref, tn_ref, h_ref, posg_ref, xs_hbm,
                     buf, zbuf, sem, zsem, *, ne):
    i = pl.program_id(0)
    slot = i % 2
    pos = posg_ref[0][0:TOP_K].astype(I32)
    buf[slot] = jnp.dot(_one_hot_rows(pos, buf.shape[1]), h_ref[...],
                        preferred_element_type=F32).astype(BF16)

    def run_copy(tile, c):
        s = tile % 2
        return pltpu.make_async_copy(buf.at[s, _chunk_rows(c)],
                                     xs_hbm.at[_chunk_rows(dest_ref[tile, c])], sem.at[s])

    def zero_copy(g):
        return pltpu.make_async_copy(zbuf, xs_hbm.at[_chunk_rows(g)], zsem)

    @pl.when(i == 0)
    def _():
        zbuf[...] = jnp.zeros(zbuf.shape, BF16)
        for phase in ("start", "wait"):
            def per_expert(e, carry):
                def body(j, c):
                    cp = zero_copy(ts_ref[e] + j)
                    cp.start() if phase == "start" else cp.wait()
                    return c
                return lax.fori_loop(0, tn_ref[e], body, carry)
            lax.fori_loop(0, ne, per_expert, 0)

    _for_each_chunk(n_ref[i], lambda c: run_copy(i, c).start())

    def wait_tile(tile):
        s = tile % 2
        _wait_chunks(n_ref[tile], lambda r: pltpu.make_async_copy(
            buf.at[s, 0:r], xs_hbm.at[0:r], sem.at[s]))

    @pl.when(i > 0)
    def _():
        wait_tile(i - 1)

    @pl.when(i == pl.num_programs(0) - 1)
    def _():
        wait_tile(i)


def _dispatch(h, posg, nchunks, dest, tail_start, tail_n, *, nb):
    T, D = h.shape
    nt, _, tm = posg.shape
    ne = tail_n.shape[0]
    return pl.pallas_call(
        functools.partial(_dispatch_kernel, ne=ne),
        grid_spec=pltpu.PrefetchScalarGridSpec(
            num_scalar_prefetch=4, grid=(nt,),
            in_specs=[pl.BlockSpec((tm, D), lambda i, *_: (i, 0)),
                      pl.BlockSpec((1, 2 * TOP_K, tm), lambda i, *_: (i, 0, 0))],
            out_specs=pl.BlockSpec(memory_space=pl.ANY),
            scratch_shapes=[pltpu.VMEM((2, _slots(tm, ne), D), BF16),
                            pltpu.VMEM((CHUNK, D), BF16),
                            pltpu.SemaphoreType.DMA((2,)), pltpu.SemaphoreType.DMA(())]),
        out_shape=jax.ShapeDtypeStruct((nb * MOE_BLOCK, D), BF16),
        compiler_params=_cparams(("arbitrary",)),
        name="dispatch",
    )(nchunks, dest, tail_start, tail_n, h, posg)


def _experts_kernel(be_ref, nu_ref, nxt_ref, edge_ref, xs_ref, wgu_hbm, bgu_ref, wd_hbm, bd_ref,
                    ys_ref, stage_gu, stage_d, wgu_s, wd_s, sem):
    b = pl.program_id(0)

    @pl.when(b < nu_ref[0])
    def _():
        edge = edge_ref[b]
        half = edge // 4
        nxt = nxt_ref[b]

        def weight_copies(e):
            return (pltpu.make_async_copy(wgu_hbm.at[e], stage_gu, sem.at[0]),
                    pltpu.make_async_copy(wd_hbm.at[e], stage_d, sem.at[1]))

        def cast_into(h):
            wgu_s[h] = stage_gu[...].astype(BF16)
            wd_s[h] = stage_d[...].astype(BF16)

        @pl.when(b == 0)
        def _():
            for cp in weight_copies(be_ref[b]):
                cp.start()
            for cp in weight_copies(be_ref[b]):
                cp.wait()
            cast_into(half)

        @pl.when((edge % 2 == 1) & (nxt >= 0))
        def _():
            for cp in weight_copies(nxt):
                cp.start()

        ff = wd_s.shape[1]
        gu = jnp.dot(xs_ref[...], wgu_s[half], preferred_element_type=F32) + bgu_ref[0]
        g = jnp.minimum(gu[:, :ff], SWIGLU_LIMIT)
        u = jnp.clip(gu[:, ff:], -SWIGLU_LIMIT, SWIGLU_LIMIT)
        act = (u + 1.0) * g * jax.nn.sigmoid(SWIGLU_ALPHA * g)
        ys = jnp.dot(act.astype(BF16), wd_s[half], preferred_element_type=F32) + bd_ref[0]
        ys_ref[...] = ys.astype(ys_ref.dtype)

        @pl.when(((edge // 2) % 2 == 1) & (nxt >= 0))
        def _():
            for cp in weight_copies(nxt):
                cp.wait()
            cast_into(1 - half)


def _experts(xs, blk_e, nused, nxt, edge, w_gate_up, b_gate_up, w_down, b_down):
    rows, D = xs.shape
    nb = rows // MOE_BLOCK
    ne, _, ff2 = w_gate_up.shape
    ff = w_down.shape[1]
    blk = lambda b, be, nu, *_: (jnp.minimum(b, nu[0] - 1), 0)
    exp3 = lambda b, be, *_: (be[b], 0, 0)
    return pl.pallas_call(
        _experts_kernel,
        grid_spec=pltpu.PrefetchScalarGridSpec(
            num_scalar_prefetch=4, grid=(nb,),
            in_specs=[pl.BlockSpec((MOE_BLOCK, D), blk),
                      pl.BlockSpec(memory_space=pl.ANY), pl.BlockSpec((1, 1, ff2), exp3),
                      pl.BlockSpec(memory_space=pl.ANY), pl.BlockSpec((1, 1, D), exp3)],
            out_specs=pl.BlockSpec((MOE_BLOCK, D), blk),
            scratch_shapes=[pltpu.VMEM((D, ff2), F32), pltpu.VMEM((ff, D), F32),
                            pltpu.VMEM((2, D, ff2), BF16), pltpu.VMEM((2, ff, D), BF16),
                            pltpu.SemaphoreType.DMA((2,))]),
        out_shape=jax.ShapeDtypeStruct((rows, D), BF16),
        compiler_params=_cparams(("arbitrary",)),
        name="experts",
    )(blk_e, nused, nxt, edge, xs, w_gate_up, b_gate_up[:, None, :], w_down,
      b_down[:, None, :])


def _combine_kernel(n_ref, dest_ref, ys_hbm, posg_ref, x1_ref, y_ref, ylast_ref, buf, sem):
    i = pl.program_id(0)
    last = pl.num_programs(0) - 1
    tm = x1_ref.shape[0]
    nslots = buf.shape[1]

    def run_copy(tile, c):
        s = tile % 2
        return pltpu.make_async_copy(ys_hbm.at[_chunk_rows(dest_ref[tile, c])],
                                     buf.at[s, _chunk_rows(c)], sem.at[s])

    def fetch(tile):
        _for_each_chunk(n_ref[tile], lambda c: run_copy(tile, c).start())
        s = tile % 2

        def zero_chunk(c, carry):
            buf[s, _chunk_rows(c), :] = jnp.zeros((CHUNK, buf.shape[2]), BF16)
            return carry
        lax.fori_loop(n_ref[tile], nslots // CHUNK, zero_chunk, 0)

    @pl.when(i == 0)
    def _():
        fetch(i)

    @pl.when(i < last)
    def _():
        fetch(i + 1)

    r0 = lax.broadcasted_iota(I32, (tm, tm), 0)
    r1 = lax.broadcasted_iota(I32, (tm, tm), 1)
    posg_t = _dot_nt(jnp.where(r0 == r1, 1.0, 0.0), posg_ref[0], exact=True)
    slot = lax.broadcasted_iota(I32, (tm, nslots), 1)
    w = jnp.zeros(slot.shape, F32)
    for k in range(TOP_K):
        w = w + jnp.where(slot == posg_t[:, k:k + 1].astype(I32),
                          posg_t[:, TOP_K + k:TOP_K + k + 1], 0.0)

    _wait_chunks(n_ref[i], lambda r: pltpu.make_async_copy(
        ys_hbm.at[0:r], buf.at[i % 2, 0:r], sem.at[i % 2]))
    y = x1_ref[...] + jnp.dot(w.astype(BF16), buf[i % 2], preferred_element_type=F32)

    @pl.when(i < last)
    def _():
        y_ref[...] = y

    @pl.when(i == last)
    def _():
        ylast_ref[...] = y


def _combine(ys, posg, x1, nchunks, dest):
    T, D = x1.shape
    nt, _, tm = posg.shape
    return pl.pallas_call(
        _combine_kernel,
        grid_spec=pltpu.PrefetchScalarGridSpec(
            num_scalar_prefetch=2, grid=(nt,),
            in_specs=[pl.BlockSpec(memory_space=pl.ANY),
                      pl.BlockSpec((1, 2 * TOP_K, tm), lambda i, *_: (i, 0, 0)),
                      pl.BlockSpec((tm, D), lambda i, *_: (i, 0))],
            out_specs=[pl.BlockSpec((tm, D), lambda i, *_: (jnp.minimum(i, nt - 2), 0)),
                       pl.BlockSpec((tm, D), lambda i, *_: (0, 0))],
            scratch_shapes=[pltpu.VMEM((2, dest.shape[1] * CHUNK, D), BF16),
                            pltpu.SemaphoreType.DMA((2,))]),
        out_shape=[jax.ShapeDtypeStruct((T - tm, D), F32), jax.ShapeDtypeStruct((tm, D), F32)],
        compiler_params=_cparams(("arbitrary",)),
        name="combine",
    )(nchunks, dest, ys, posg, x1)


def _moe(h, posg, cnt, x1, w_gate_up, b_gate_up, w_down, b_down):
    nt, _, tm = posg.shape
    ne = cnt.shape[1]
    max_chunks = (nt * tm * TOP_K) // CHUNK + nt * ne
    nb = -(-max_chunks // CHUNKS_PER_BLOCK) + ne
    nchunks, dest, tail_start, tail_n, nused, blk_e, nxt, edge = _moe_tables(
        cnt, nb, _slots(tm, ne) // CHUNK)
    xs = _dispatch(h, posg, nchunks, dest, tail_start, tail_n, nb=nb)
    ys = _experts(xs, blk_e, nused, nxt, edge, w_gate_up, b_gate_up, w_down, b_down)
    return _combine(ys, posg, x1, nchunks, dest)


def kernel(x_prompt, x_sample, cache_k, cache_v, state_conv, page_table, norm1_w, w_in,
           q_norm_w, k_norm_w, lambda_q1, lambda_k1, lambda_q2, lambda_k2, subln_w,
           conv_w, conv_b, conv_ln_g, conv_ln_b, w_out, norm2_w, router_w, router_b,
           w_gate_up, b_gate_up, w_down, b_down):
    B, S, D = x_prompt.shape
    Bs, Ss, _ = x_sample.shape
    depth = norm1_w.shape[0]
    n_phys, page, heads, _, qk = cache_k.shape[1:]
    vdim = cache_v.shape[-1]
    qc, vc, cc = heads * 2 * qk, heads * vdim, conv_w.shape[2]
    taps = conv_w.shape[1]
    assert Ss == 1 and qk == QK_GROUP and vdim == V7X_LANES and (B * S) % MOE_TILE == 0
    assert Bs <= MOE_TILE and S >= taps - 1
    n_past = page_table.shape[1] * page
    T = B * S
    nt_p = T // MOE_TILE
    pos_p = jnp.arange(S, dtype=F32)
    pos_s = jnp.full((Bs,), n_past, F32)
    xp = x_prompt.reshape(T, D)
    xs = x_sample.reshape(Bs, D)
    pad_tile = lambda a: jnp.pad(a, ((0, MOE_TILE - Bs), (0, 0)))
    outs = [[] for _ in range(6)]
    for l in range(depth):
        lam_init = 0.8 - 0.6 * math.exp(-0.3 * l)
        lamv = jnp.stack([lambda_q1[l], lambda_k1[l], lambda_q2[l], lambda_k2[l]])
        conv_p = (conv_w[l], conv_b[l], conv_ln_g[l], conv_ln_b[l])
        proj_p = (norm1_w[l], w_in[l], q_norm_w[l], k_norm_w[l])
        tail_p = (w_out[l], norm2_w[l], router_w[l], router_b[l])

        q, kt, v, u, kbt, vb = _proj(xp, pos_p, S // PROJ_TILE, PROJ_TILE, *proj_p,
                                     qc=qc, vc=vc, cc=cc, exact_norm=False, attn_layout=True)
        qs, ks_, vs_, us = _proj(xs, pos_s, 1, Bs, *proj_p, qc=qc, vc=vc, cc=cc,
                                 exact_norm=True, attn_layout=False)
        cache_kt = jnp.transpose(cache_k[l], (0, 2, 3, 4, 1)).reshape(n_phys, qc, page)
        cache_vr = cache_v[l].reshape(n_phys, page * heads, vdim)
        attn, attn_s = _attention(q, kbt, vb, qs, ks_, vs_, cache_kt, cache_vr, page_table, lamv,
                                  subln_w[l], lam_init=lam_init)

        conv = _conv_prompt(u, *conv_p, batch=B, seq=S)
        bufs = _tail(attn, conv, xp, *tail_p, tm=MOE_TILE, n_valid=MOE_TILE,
                     total_tiles=nt_p + 1)
        outs[0].append(jnp.transpose(kt.reshape(B, heads, 2, qk, S), (0, 4, 1, 2, 3)))
        outs[1].append(v.reshape(B, S, heads, vdim))
        outs[2].append(u.reshape(B, S, cc)[:, S - (taps - 1):])

        conv_s = _conv_step(state_conv[l], us, *conv_p)
        x1, h, posg, cnt = _tail(pad_tile(attn_s), pad_tile(conv_s), pad_tile(xs), *tail_p,
                                 tm=MOE_TILE, n_valid=Bs, total_tiles=nt_p + 1,
                                 first_tile=nt_p, into=bufs)
        outs[3].append(ks_.reshape(Bs, Ss, heads, 2, qk))
        outs[4].append(vs_.reshape(Bs, Ss, heads, vdim))
        outs[5].append(jnp.concatenate([state_conv[l][:, Ss:], us[:, None, :]], axis=1))

        xp, y_last = _moe(h, posg, cnt[:, :, 0], x1, w_gate_up[l], b_gate_up[l], w_down[l],
                          b_down[l])
        xs = y_last[:Bs]
    return (xp.reshape(B, S, D), xs.reshape(Bs, Ss, D)) + tuple(jnp.stack(o) for o in outs)
```

```python
import functools
import math

import jax
import jax.numpy as jnp
from jax import lax
from jax.experimental import pallas as pl
from jax.experimental.pallas import tpu as pltpu

F32 = jnp.float32
BF16 = jnp.bfloat16
I32 = jnp.int32
HIGHEST = lax.Precision.HIGHEST

EPS = 1e-6
ROPE_THETA = 10000.0
SWIGLU_LIMIT = 7.0
SWIGLU_ALPHA = 1.702
TOP_K = 4
NEG = -1e30
QK_GROUP = 64

V7X_LANES = 128
V7X_SUBLANES = 8
VMEM_LIMIT = 56 * 1024 * 1024
BF16_ROWS = 16

MOE_TILE = 256
CHUNK = BF16_ROWS
MOE_BLOCK = 512
CHUNKS_PER_BLOCK = MOE_BLOCK // CHUNK
PROJ_TILE = 512


def _cparams(sem, vmem=VMEM_LIMIT):
    return pltpu.CompilerParams(dimension_semantics=sem, vmem_limit_bytes=vmem)


def _dot(a, b, exact=False):
    if exact:
        return jnp.dot(a.astype(F32), b.astype(F32), precision=HIGHEST,
                       preferred_element_type=F32)
    return jnp.dot(a.astype(BF16), b.astype(BF16), preferred_element_type=F32)


def _dot_nt(a, b, exact=False):
    dn = (((1,), (1,)), ((), ()))
    if exact:
        return lax.dot_general(a.astype(F32), b.astype(F32), dn, precision=HIGHEST,
                               preferred_element_type=F32)
    return lax.dot_general(a.astype(BF16), b.astype(BF16), dn, preferred_element_type=F32)


def _bf16_round(x):
    return x.astype(BF16).astype(F32)


def _rope_norm(p, gsum, w, cos, sin, first_half, exact_norm):
    ss = _dot(p * p, gsum, exact_norm)
    n = p * lax.rsqrt(ss * (1.0 / QK_GROUP) + EPS) * w
    outs = []
    for j in range(p.shape[1] // V7X_LANES):
        nj = n[:, j * V7X_LANES:(j + 1) * V7X_LANES]
        rot = jnp.where(first_half, pltpu.roll(nj, V7X_LANES - QK_GROUP // 2, 1),
                        pltpu.roll(nj, QK_GROUP // 2, 1))
        outs.append(nj * cos + rot * sin)
    return jnp.concatenate(outs, axis=1)


def _proj_kernel(x_ref, n1_ref, w_ref, qw_ref, kw_ref, cos_ref, sin_ref, gsum_ref,
                 q_ref, k_ref, v_ref, u_ref, *rest, qc, vc, cc, scale, exact_norm):
    x = x_ref[...]
    h = x * lax.rsqrt(jnp.mean(x * x, axis=-1, keepdims=True) + EPS) * n1_ref[...]
    hm = h.astype(BF16)
    cos = cos_ref[...]
    sin = sin_ref[...]
    lane = lax.broadcasted_iota(I32, cos.shape, 1)
    first_half = (lane % QK_GROUP) < QK_GROUP // 2
    gsum = gsum_ref[...]

    q = _rope_norm(_dot(hm, w_ref[:, 0:qc]), gsum, qw_ref[...], cos, sin, first_half, exact_norm)
    q_ref[...] = (q * scale).astype(q_ref.dtype)
    k = _rope_norm(_dot(hm, w_ref[:, qc:2 * qc]), gsum, kw_ref[...], cos, sin, first_half,
                   exact_norm)
    v = _dot(hm, w_ref[:, 2 * qc:2 * qc + vc])
    o = 2 * qc + vc
    ua = _dot(hm, w_ref[:, o:o + cc])
    ub = _dot(hm, w_ref[:, o + cc:o + 2 * cc])
    u_ref[...] = ua * jax.nn.sigmoid(ub)
    if rest:
        kb_ref, vb_ref = rest
        kt = k.T
        k_ref[0] = kt
        kb_ref[0, :, 0] = kt.astype(BF16).reshape(kb_ref.shape[1], kb_ref.shape[3], kt.shape[1])
        vb_ref[...] = v.astype(BF16)
        nh = vc // V7X_LANES
        for h in range(nh):
            v_ref[pl.ds(h, v.shape[0], stride=nh), :] = v[:, h * V7X_LANES:(h + 1) * V7X_LANES]
    else:
        k_ref[...] = k
        v_ref[...] = v


def _rope_tables(pos):
    half = QK_GROUP // 2
    inv = jnp.power(ROPE_THETA, -jnp.arange(half, dtype=F32) / half)
    ang = pos[:, None] * inv[None, :]
    reps = V7X_LANES // QK_GROUP
    cos = jnp.tile(jnp.cos(ang), (1, 2 * reps))
    s = jnp.sin(ang)
    sin = jnp.tile(jnp.concatenate([-s, s], axis=1), (1, reps))
    return cos, sin


def _proj(x2d, pos_rows, n_pos_blocks, tm, norm1_w, w_in, q_norm_w, k_norm_w, *, qc, vc, cc,
          exact_norm, attn_layout):
    T, D = x2d.shape
    cos, sin = _rope_tables(pos_rows)
    gi = jnp.arange(qc) // QK_GROUP
    gsum = (gi[:, None] == gi[None, :]).astype(F32 if exact_norm else BF16)
    qw = jnp.tile(q_norm_w, qc // QK_GROUP)[None, :]
    kw = jnp.tile(k_norm_w, qc // QK_GROUP)[None, :]
    w = w_in.astype(BF16)
    row = lambda i: (i, 0)
    full = lambda i: (0, 0)
    out_shape = [jax.ShapeDtypeStruct((T, qc), BF16),
                 jax.ShapeDtypeStruct((T, qc), F32),
                 jax.ShapeDtypeStruct((T, vc), F32),
                 jax.ShapeDtypeStruct((T, cc), F32)]
    out_specs = [pl.BlockSpec((tm, qc), row), pl.BlockSpec((tm, qc), row),
                 pl.BlockSpec((tm, vc), row), pl.BlockSpec((tm, cc), row)]
    if attn_layout:
        nseq = T // (n_pos_blocks * tm)
        heads = qc // (2 * QK_GROUP)
        seq_tile = lambda i: (i // n_pos_blocks, 0, i % n_pos_blocks)
        out_shape[1] = jax.ShapeDtypeStruct((nseq, qc, n_pos_blocks * tm), F32)
        out_specs[1] = pl.BlockSpec((1, qc, tm), seq_tile)
        out_shape[2] = jax.ShapeDtypeStruct((T * vc // V7X_LANES, V7X_LANES), F32)
        out_specs[2] = pl.BlockSpec((tm * vc // V7X_LANES, V7X_LANES), row)
        out_shape += [jax.ShapeDtypeStruct((nseq, heads, n_pos_blocks, 2 * QK_GROUP, tm), BF16),
                      jax.ShapeDtypeStruct((T, vc), BF16)]
        out_specs += [pl.BlockSpec((1, heads, 1, 2 * QK_GROUP, tm),
                                   lambda i: (i // n_pos_blocks, 0, i % n_pos_blocks, 0, 0)),
                      pl.BlockSpec((tm, vc), row)]
    return pl.pallas_call(
        functools.partial(_proj_kernel, qc=qc, vc=vc, cc=cc, scale=QK_GROUP ** -0.5,
                          exact_norm=exact_norm),
        grid=(T // tm,),
        in_specs=[pl.BlockSpec((tm, D), row),
                  pl.BlockSpec((1, D), full),
                  pl.BlockSpec(w.shape, full),
                  pl.BlockSpec((1, qc), full),
                  pl.BlockSpec((1, qc), full),
                  pl.BlockSpec((tm, V7X_LANES), lambda i: (i % n_pos_blocks, 0)),
                  pl.BlockSpec((tm, V7X_LANES), lambda i: (i % n_pos_blocks, 0)),
                  pl.BlockSpec((qc, qc), full)],
        out_specs=out_specs,
        out_shape=out_shape,
        compiler_params=_cparams(("arbitrary",)),
        name="proj" if attn_layout else "proj_step",
    )(x2d, norm1_w[None, :], w, qw, kw, cos, sin, gsum)


def _lambda_value(lv, lam_init):
    a = jnp.sum(lv[0:1] * lv[1:2], axis=-1, keepdims=True)
    b = jnp.sum(lv[2:3] * lv[3:4], axis=-1, keepdims=True)
    return jnp.exp(a) - jnp.exp(b) + lam_init


def _subln(o, w, lam_init):
    y = o * lax.rsqrt(jnp.mean(o * o, axis=-1, keepdims=True) + EPS)
    return y * w * (1.0 - lam_init)


def _attn_body(i, lam_ref, sw_ref, q_ref, k_ref, v_ref, o_ref, *, tq, lam_init):
    lam = _lambda_value(lam_ref[...], lam_init)
    q = q_ref[...]
    lane = lax.broadcasted_iota(I32, q.shape, 1)
    zero = jnp.zeros_like(q)
    qs = (jnp.where(lane < QK_GROUP, q, zero), jnp.where(lane >= QK_GROUP, q, zero))

    def chunk(j, carry, masked):
        kc = k_ref[0, 0, j]
        vc = v_ref[pl.ds(pl.multiple_of(j * tq, tq), tq), :]
        out = []
        for c in range(2):
            m, l, acc = carry[c]
            s = jnp.dot(qs[c], kc, preferred_element_type=F32)
            if masked:
                row = lax.broadcasted_iota(I32, s.shape, 0)
                col = lax.broadcasted_iota(I32, s.shape, 1)
                s = jnp.where(col <= row, s, NEG)
            m_new = jnp.maximum(m, jnp.max(s, axis=-1, keepdims=True))
            p = jnp.exp(s - m_new)
            alpha = jnp.exp(m - m_new)
            l = alpha * l + jnp.sum(p, axis=-1, keepdims=True)
            acc = alpha * acc + _dot(p, vc)
            out.append((m_new, l, acc))
        return tuple(out)

    init = tuple((jnp.full((tq, 1), NEG, F32), jnp.zeros((tq, 1), F32),
                  jnp.zeros((tq, V7X_LANES), F32)) for _ in range(2))
    carry = lax.fori_loop(0, i, lambda j, c: chunk(j, c, False), init)
    (_, l0, a0), (_, l1, a1) = chunk(i, carry, True)
    o = a0 / l0 - lam * (a1 / l1)
    o_ref[...] = _subln(o, sw_ref[...], lam_init).astype(o_ref.dtype)


DECODE_PAGES_PER_STEP = 32
SOFTMAX_PAGES = 16
PAGE_RING_DEPTH = 3


def _decode_body(s, half_steps, lam_ref, sw_ref, qm_ref, kn_ref, vn_ref, pages_ref,
                 o_ref, s_ref, m_ref, coef_ref, acc_ref, *, page, heads, lam_init):
    pps = pages_ref.shape[0]
    rows = s_ref.shape[1]
    qm = qm_ref[0]
    row = lax.broadcasted_iota(I32, (rows, V7X_LANES), 0)
    n_pages = half_steps * pps

    @pl.when(s == 0)
    def _():
        m_ref[...] = jnp.full(m_ref.shape, NEG, F32)

    @pl.when(s < half_steps)
    def _():
        m = m_ref[...]
        for j in range(pps):
            sc = jnp.dot(qm, pages_ref[j].astype(BF16), preferred_element_type=F32)
            s_ref[s * pps + j] = sc
            m = jnp.maximum(m, sc)
        m_ref[...] = m

    def head_weights(p):
        a = p * coef_ref[...]
        return _bf16_round(a + pltpu.roll(a, rows - heads, 0))

    @pl.when(s == half_steps)
    def _():
        s_new = jnp.sum(qm.astype(F32) * _bf16_round(kn_ref[0]), axis=-1, keepdims=True)
        m = jnp.maximum(jnp.max(m_ref[...], axis=-1, keepdims=True), s_new)

        def exp_pages(t, l):
            pages = pl.ds(pl.multiple_of(t * SOFTMAX_PAGES, SOFTMAX_PAGES), SOFTMAX_PAGES)
            p = jnp.exp(s_ref[pages] - m[None])
            s_ref[pages] = p
            return l + jnp.sum(p, axis=0)

        lsum = lax.fori_loop(0, n_pages // SOFTMAX_PAGES, exp_pages,
                             jnp.zeros((rows, V7X_LANES), F32))
        p_new = jnp.exp(s_new - m)
        l = jnp.sum(lsum, axis=-1, keepdims=True) + p_new
        lam = _lambda_value(lam_ref[...], lam_init)
        coef = jnp.where(row[:, 0:1] < heads, 1.0, -lam) / l
        coef_ref[...] = jnp.broadcast_to(coef, coef_ref.shape)
        acc_ref[...] = (head_weights(jnp.broadcast_to(p_new, (rows, V7X_LANES)))
                        * _bf16_round(vn_ref[0]))

    @pl.when(s >= half_steps)
    def _():
        lane = lax.broadcasted_iota(I32, (rows, V7X_LANES), 1)
        keep = (lane % heads == row) & (row < heads)
        acc = acc_ref[...]
        for j in range(pps):
            a = head_weights(s_ref[(s - half_steps) * pps + j])
            parts = []
            for c in range(heads):
                idx = (c * page + lane) // heads
                parts.append(jnp.where(keep, jnp.take_along_axis(a, idx, axis=1), 0.0))
            a_exp = jnp.concatenate(parts, axis=1).astype(BF16)
            acc = acc + jnp.dot(a_exp, pages_ref[j].astype(BF16), preferred_element_type=F32)
        acc_ref[...] = acc

    @pl.when(s == 2 * half_steps - 1)
    def _():
        o_ref[0] = _subln(acc_ref[...], sw_ref[...], lam_init)


def _attention_kernel(pt_ref, lam_ref, sw_ref, q_ref, k_ref, v_ref, qm_ref, kn_ref, vn_ref,
                      kt_hbm, vr_hbm, o_ref, od_ref, s_ref, m_ref, coef_ref, acc_ref, pages, sem,
                      *, ratio, nq, steps_per_seq, tq, page, heads, lam_init):
    t = pl.program_id(0)
    depth, pps = pages.shape[:2]
    half_steps = steps_per_seq // 2

    def fetch(step):
        seq = step // steps_per_seq
        s = step % steps_per_seq
        first = (s % half_steps) * pps
        slot = step % depth
        for src_hbm, cond in ((kt_hbm, s < half_steps), (vr_hbm, s >= half_steps)):
            @pl.when(cond)
            def _():
                def start(j, carry):
                    pltpu.make_async_copy(src_hbm.at[pt_ref[seq, first + j]],
                                          pages.at[slot, j], sem.at[slot]).start()
                    return carry
                lax.fori_loop(0, pps, start, 0)

    @pl.when(t == 0)
    def _():
        for ahead in range(depth - 1):
            fetch(t + ahead)

    @pl.when(t + depth - 1 < pl.num_programs(0))
    def _():
        fetch(t + depth - 1)

    slot = t % depth
    pltpu.make_async_copy(kt_hbm.at[0:pps], pages.at[slot], sem.at[slot]).wait()
    _decode_body(t % steps_per_seq, half_steps, lam_ref, sw_ref, qm_ref, kn_ref, vn_ref,
                 pages.at[slot], od_ref, s_ref, m_ref, coef_ref, acc_ref,
                 page=page, heads=heads, lam_init=lam_init)

    @pl.when(t % ratio == 0)
    def _():
        _attn_body((t // ratio) % nq, lam_ref, sw_ref, q_ref, k_ref, v_ref, o_ref,
                   tq=tq, lam_init=lam_init)


def _attention(q, kbt, vb, q_step, k_new, v_new, cache_kt, cache_vr, page_table, lamv, subln_w,
               *, lam_init):
    batch, heads, nq, _, tq = kbt.shape
    seq = nq * tq
    Bs, D = q_step.shape
    n_pages = page_table.shape[1]
    page = cache_kt.shape[2]
    vdim = cache_vr.shape[2]
    pps = DECODE_PAGES_PER_STEP
    half_steps = n_pages // pps
    steps_per_seq = 2 * half_steps
    rows = 2 * heads
    n_dec, n_att = Bs * steps_per_seq, batch * heads * nq
    assert rows == V7X_SUBLANES and page == V7X_LANES and vdim == V7X_LANES
    assert n_pages % pps == 0 and n_pages % SOFTMAX_PAGES == 0 and n_dec % n_att == 0
    assert cache_kt.shape[1:] == cache_vr.shape[1:]
    ratio = n_dec // n_att
    group = jnp.arange(D) // QK_GROUP
    rowmask = ((group % 2) * heads + group // 2)[None, :] == jnp.arange(rows)[:, None]
    qm = jnp.where(rowmask[None], q_step[:, None, :], jnp.zeros((), BF16))
    vn = jnp.pad(v_new.reshape(Bs, heads, vdim), ((0, 0), (0, rows - heads), (0, 0)))

    def att(t):
        a = t // ratio
        return a // (heads * nq), (a // nq) % heads, a % nq

    def q_map(t, pt):
        b, h, i = att(t)
        return b * nq + i, h

    seq3 = lambda t, pt: (t // steps_per_seq, 0, 0)
    const = lambda t, pt: (0, 0)
    out, out_step = pl.pallas_call(
        functools.partial(_attention_kernel, ratio=ratio, nq=nq,
                          steps_per_seq=steps_per_seq, tq=tq, page=page, heads=heads,
                          lam_init=lam_init),
        grid_spec=pltpu.PrefetchScalarGridSpec(
            num_scalar_prefetch=1, grid=(n_dec,),
            in_specs=[pl.BlockSpec(lamv.shape, const), pl.BlockSpec((1, vdim), const),
                      pl.BlockSpec((tq, V7X_LANES), q_map),
                      pl.BlockSpec((1, 1, nq, V7X_LANES, tq),
                                   lambda t, pt: att(t)[:2] + (0, 0, 0)),
                      pl.BlockSpec((seq, V7X_LANES), lambda t, pt: att(t)[:2]),
                      pl.BlockSpec((1, rows, D), seq3), pl.BlockSpec((1, 1, D), seq3),
                      pl.BlockSpec((1, rows, vdim), seq3),
                      pl.BlockSpec(memory_space=pl.ANY), pl.BlockSpec(memory_space=pl.ANY)],
            out_specs=[pl.BlockSpec((tq, V7X_LANES), q_map),
                       pl.BlockSpec((1, rows, vdim), seq3)],
            scratch_shapes=[pltpu.VMEM((n_pages, rows, page), F32),
                            pltpu.VMEM((rows, V7X_LANES), F32),
                            pltpu.VMEM((rows, V7X_LANES), F32),
                            pltpu.VMEM((rows, vdim), F32),
                            pltpu.VMEM((PAGE_RING_DEPTH, pps) + cache_kt.shape[1:], F32),
                            pltpu.SemaphoreType.DMA((PAGE_RING_DEPTH,))]),
        out_shape=[jax.ShapeDtypeStruct((batch * seq, heads * V7X_LANES), BF16),
                   jax.ShapeDtypeStruct((Bs, rows, vdim), F32)],
        compiler_params=_cparams(("arbitrary",)),
        name="attention",
    )(page_table, lamv, subln_w[None, :], q, kbt, vb, qm, k_new[:, None, :], vn,
      cache_kt, cache_vr)
    return out, out_step[:, :heads, :].reshape(Bs, heads * vdim)


CONV_HALO = 32


def _ln_swish(y, b_ref, g_ref, be_ref):
    y = y + b_ref[...]
    mu = jnp.mean(y, axis=-1, keepdims=True)
    yc = y - mu
    z = yc * lax.rsqrt(jnp.mean(yc * yc, axis=-1, keepdims=True) + EPS) * g_ref[...] + be_ref[...]
    return z * jax.nn.sigmoid(z)


def _conv_tile(u_ref, starts_sequence, w_ref, b_ref, g_ref, be_ref, buf_ref, part_ref):
    sub = V7X_SUBLANES
    tc = u_ref.shape[0]
    taps = w_ref.shape[0]
    first = CONV_HALO - (taps - 1)

    @pl.when(starts_sequence)
    def _():
        buf_ref[0:CONV_HALO, :] = jnp.zeros((CONV_HALO, buf_ref.shape[1]), F32)
        buf_ref[CONV_HALO + tc:, :] = jnp.zeros((sub, buf_ref.shape[1]), F32)

    buf_ref[CONV_HALO:CONV_HALO + tc, :] = _bf16_round(u_ref[...])
    acc = None
    for r in range(sub):
        part = None
        for a in range(-(-(first + taps) // sub)):
            k = sub * a + r - first
            if 0 <= k < taps:
                term = buf_ref[sub * a:sub * a + tc + sub, :] * _bf16_round(w_ref[k:k + 1, :])
                part = term if part is None else part + term
        if r == 0:
            acc = part[0:tc]
        else:
            part_ref[...] = part
            acc = acc + part_ref[r:r + tc, :]
    out = _ln_swish(acc, b_ref, g_ref, be_ref)
    buf_ref[0:CONV_HALO, :] = buf_ref[tc:tc + CONV_HALO, :]
    return out


def _conv_step_kernel(st_ref, u_ref, w_ref, b_ref, g_ref, be_ref, o_ref, *, taps):
    acc = u_ref[...] * w_ref[taps - 1:taps, :]
    for k in range(taps - 1):
        acc = acc + st_ref[:, k, :] * w_ref[k:k + 1, :]
    o_ref[...] = _ln_swish(acc, b_ref, g_ref, be_ref)


def _conv_step(state, u, conv_w, conv_b, ln_g, ln_b):
    taps, C = conv_w.shape
    return pl.pallas_call(
        functools.partial(_conv_step_kernel, taps=taps),
        out_shape=jax.ShapeDtypeStruct(u.shape, F32),
        compiler_params=_cparams(None),
        name="conv_step",
    )(state, u, conv_w, conv_b[None, :], ln_g[None, :], ln_b[None, :])


def _tail_kernel(a_ref, c_ref, x_ref, wo_ref, n2_ref, rw_ref, rb_ref, *rest, n_valid,
                 tiles_per_seq):
    tm = x_ref.shape[0]
    half = a_ref.shape[1]
    if tiles_per_seq:
        cw_ref, cb_ref, cg_ref, cbe_ref = rest[:4]
        buf_ref, part_ref = rest[-2:]
        x1_ref, h_ref, posg_ref, cnt_ref = rest[-6:-2]
        conv = _conv_tile(c_ref, pl.program_id(0) % tiles_per_seq == 0, cw_ref, cb_ref, cg_ref,
                          cbe_ref, buf_ref, part_ref)
    else:
        x1_ref, h_ref, posg_ref, cnt_ref = rest[-4:]
        conv = c_ref[...]
    x1 = (x_ref[...] + _dot(a_ref[...], wo_ref[0:half, :]) + _dot(conv, wo_ref[half:, :]))
    x1_ref[...] = x1
    h = (x1 * lax.rsqrt(jnp.mean(x1 * x1, axis=-1, keepdims=True) + EPS)
         * n2_ref[...]).astype(BF16)
    h_ref[...] = h
    logits = _dot_nt(rw_ref[...], h) + rb_ref[...]
    ne = logits.shape[0]
    eidx = lax.broadcasted_iota(I32, logits.shape, 0)
    valid = lax.broadcasted_iota(I32, (1, tm), 1) < n_valid

    sels, vals = [], []
    l = logits
    for _ in range(TOP_K):
        m = jnp.max(l, axis=0, keepdims=True)
        first = jnp.min(jnp.where(l == m, eidx, ne), axis=0, keepdims=True)
        sel = (eidx == first) & valid
        l = jnp.where(eidx == first, -jnp.inf, l)
        sels.append(sel)
        vals.append(m)
    ex = [jnp.exp(v - vals[0]) for v in vals]
    den = ex[0] + ex[1] + ex[2] + ex[3]
    gates = [jnp.where(valid, e / den, 0.0) for e in ex]

    msel = jnp.zeros(logits.shape, F32)
    for sel in sels:
        msel = msel + jnp.where(sel, 1.0, 0.0)
    r0 = lax.broadcasted_iota(I32, (tm, tm), 0)
    r1 = lax.broadcasted_iota(I32, (tm, tm), 1)
    upper = jnp.where(r0 < r1, 1.0, 0.0).astype(BF16)
    rank = jnp.dot(msel.astype(BF16), upper, preferred_element_type=F32)
    cnt = jnp.sum(msel, axis=1, keepdims=True)
    pcnt = jnp.ceil(cnt * (1.0 / CHUNK)) * CHUNK
    e0 = lax.broadcasted_iota(I32, (ne, ne), 0)
    e1 = lax.broadcasted_iota(I32, (ne, ne), 1)
    lower = jnp.where(e1 < e0, 1.0, 0.0)
    off = jnp.dot(lower.astype(BF16), jnp.broadcast_to(pcnt, (ne, V7X_LANES)).astype(BF16),
                  preferred_element_type=F32)[:, 0:1]
    pos = off + rank
    rows = [jnp.where(valid, jnp.sum(jnp.where(sel, pos, 0.0), axis=0, keepdims=True), -1.0)
            for sel in sels]
    posg_ref[0] = jnp.concatenate(rows + gates, axis=0)
    cnt_ref[0] = jnp.broadcast_to(cnt, (ne, V7X_LANES)).astype(I32)


def _tail(attn, conv, x2d, w_out, norm2_w, router_w, router_b, *, tm, n_valid, total_tiles,
          first_tile=0, into=None, conv_params=None, seq=None):
    T, D = x2d.shape
    half = attn.shape[1]
    ne = router_w.shape[1]
    nt = T // tm
    wo = w_out.astype(BF16)
    rw = router_w.T.astype(BF16)
    row = lambda i: (i, 0)
    full = lambda i: (0, 0)
    orow = lambda i: (first_tile + i, 0)
    otile = lambda i: (first_tile + i, 0, 0)
    extra = list(into) if into is not None else []
    conv_in, conv_specs, scratch, tiles_per_seq = [], [], [], 0
    if conv_params is not None:
        conv_w, conv_b, ln_g, ln_b = conv_params
        taps, C = conv_w.shape
        conv_in = [conv_w, conv_b[None, :], ln_g[None, :], ln_b[None, :]]
        conv_specs = [pl.BlockSpec((taps, C), full)] + [pl.BlockSpec((1, C), full)] * 3
        scratch = [pltpu.VMEM((tm + CONV_HALO + V7X_SUBLANES, C), F32),
                   pltpu.VMEM((tm + V7X_SUBLANES, C), F32)]
        tiles_per_seq = seq // tm
    return pl.pallas_call(
        functools.partial(_tail_kernel, n_valid=n_valid, tiles_per_seq=tiles_per_seq),
        grid=(nt,),
        in_specs=[pl.BlockSpec((tm, half), row), pl.BlockSpec((tm, half), row),
                  pl.BlockSpec((tm, D), row), pl.BlockSpec((D, D), full),
                  pl.BlockSpec((1, D), full), pl.BlockSpec((ne, D), full),
                  pl.BlockSpec((ne, 1), full)] + conv_specs
                 + [pl.BlockSpec(memory_space=pl.ANY)] * len(extra),
        out_specs=[pl.BlockSpec((tm, D), orow), pl.BlockSpec((tm, D), orow),
                   pl.BlockSpec((1, 2 * TOP_K, tm), otile),
                   pl.BlockSpec((1, ne, V7X_LANES), otile)],
        out_shape=[jax.ShapeDtypeStruct((total_tiles * tm, D), F32),
                   jax.ShapeDtypeStruct((total_tiles * tm, D), BF16),
                   jax.ShapeDtypeStruct((total_tiles, 2 * TOP_K, tm), F32),
                   jax.ShapeDtypeStruct((total_tiles, ne, V7X_LANES), I32)],
        scratch_shapes=scratch,
        input_output_aliases={7 + len(conv_in) + j: j for j in range(len(extra))},
        compiler_params=_cparams(("arbitrary",)),
        name="tail",
    )(attn, conv, x2d, wo, norm2_w[None, :], rw, router_b[:, None], *conv_in, *extra)


def _slots(tm, ne):
    worst = TOP_K * tm + ne * (CHUNK - 1)
    return -(-worst // V7X_LANES) * V7X_LANES


def _prefix_sum(x, axis, exclusive):
    n = x.shape[axis]
    i = jnp.arange(n)
    tri = (i[:, None] < i[None, :]) if exclusive else (i[:, None] <= i[None, :])
    xm = jnp.moveaxis(x, axis, -1)
    out = jnp.sum(xm[..., :, None] * tri.astype(x.dtype), axis=-2)
    return jnp.moveaxis(out, -1, axis)


def _moe_tables(cnt, nb, slot_chunks):
    nch = (cnt + (CHUNK - 1)) // CHUNK
    tot = jnp.sum(nch, axis=0)
    nblk = (tot + (CHUNKS_PER_BLOCK - 1)) // CHUNKS_PER_BLOCK
    bend = _prefix_sum(nblk, 0, exclusive=False)
    gstart = (bend - nblk) * CHUNKS_PER_BLOCK
    rs = gstart[None, :] + _prefix_sum(nch, 0, exclusive=True)
    tail_start = gstart + tot
    tail_n = nblk * CHUNKS_PER_BLOCK - tot
    nused = bend[-1:]
    blk = jnp.minimum(jnp.arange(nb, dtype=I32), nused - 1)
    blk_e = jnp.sum((bend[None, :] <= blk[:, None]).astype(I32), axis=1)
    cend = _prefix_sum(nch, 1, exclusive=False)
    c = jnp.arange(slot_chunks, dtype=I32)
    run = jnp.minimum(jnp.sum((cend[:, None, :] <= c[None, :, None]).astype(I32), axis=2),
                      nch.shape[1] - 1)
    shift = rs - (cend - nch)
    dest = c[None, :] + jnp.sum(jnp.where(run[:, :, None] == jnp.arange(nch.shape[1]),
                                          shift[:, None, :], 0), axis=2)
    ne = nblk.shape[0]
    eid = jnp.arange(ne, dtype=I32)
    later = jnp.where((eid[None, :] > eid[:, None]) & (nblk[None, :] > 0), eid[None, :], ne)
    next_e = jnp.min(later, axis=1)
    next_e = jnp.where(next_e == ne, -1, next_e)
    rank_e = _prefix_sum((nblk > 0).astype(I32), 0, exclusive=True)
    bidx = jnp.arange(nb, dtype=I32)
    mine = (blk_e[:, None] == eid[None, :]).astype(I32)
    of_block = lambda per_expert: jnp.sum(mine * per_expert[None, :], axis=1)
    first = (bidx == of_block(bend - nblk)) & (bidx < nused)
    last = (bidx == of_block(bend) - 1) & (bidx < nused)
    edge = first.astype(I32) + 2 * last.astype(I32) + 4 * (of_block(rank_e) % 2)
    i32 = lambda a: a.astype(I32)
    return (i32(cend[:, -1]), i32(dest), i32(tail_start), i32(tail_n), i32(nused), i32(blk_e),
            i32(of_block(next_e)), i32(edge))


def _chunk_rows(c):
    return pl.ds(pl.multiple_of(c * CHUNK, CHUNK), CHUNK)


def _for_each_chunk(n, fn):
    def body(c, carry):
        fn(c)
        return carry
    lax.fori_loop(0, n, body, 0)


WAIT_GROUP = 8


def _wait_chunks(n, copy_of_rows):
    _for_each_chunk(n // WAIT_GROUP, lambda c: copy_of_rows(WAIT_GROUP * CHUNK).wait())
    _for_each_chunk(n % WAIT_GROUP, lambda c: copy_of_rows(CHUNK).wait())


def _one_hot_rows(pos, nrows):
    r = lax.broadcasted_iota(I32, (nrows, pos.shape[1]), 0)
    p = jnp.zeros(r.shape, F32)
    for k in range(TOP_K):
        p = jnp.where(r == pos[k:k + 1], 1.0, p)
    return p.astype(BF16)


def _dispatch_kernel(n_ref, dest_ref, ts_ref, tn_ref, h_ref, posg_ref, xs_hbm,
                     buf, zbuf, sem, zsem, *, ne):
    i = pl.program_id(0)
    slot = i % 2
    pos = posg_ref[0][0:TOP_K].astype(I32)
    buf[slot] = jnp.dot(_one_hot_rows(pos, buf.shape[1]), h_ref[...],
                        preferred_element_type=F32).astype(BF16)

    def run_copy(tile, c):
        s = tile % 2
        return pltpu.make_async_copy(buf.at[s, _chunk_rows(c)],
                                     xs_hbm.at[_chunk_rows(dest_ref[tile, c])], sem.at[s])

    def zero_copy(g):
        return pltpu.make_async_copy(zbuf, xs_hbm.at[_chunk_rows(g)], zsem)

    @pl.when(i == 0)
    def _():
        zbuf[...] = jnp.zeros(zbuf.shape, BF16)
        for phase in ("start", "wait"):
            def per_expert(e, carry):
                def body(j, c):
                    cp = zero_copy(ts_ref[e] + j)
                    cp.start() if phase == "start" else cp.wait()
                    return c
                return lax.fori_loop(0, tn_ref[e], body, carry)
            lax.fori_loop(0, ne, per_expert, 0)

    _for_each_chunk(n_ref[i], lambda c: run_copy(i, c).start())

    def wait_tile(tile):
        s = tile % 2
        _wait_chunks(n_ref[tile], lambda r: pltpu.make_async_copy(
            buf.at[s, 0:r], xs_hbm.at[0:r], sem.at[s]))

    @pl.when(i > 0)
    def _():
        wait_tile(i - 1)

    @pl.when(i == pl.num_programs(0) - 1)
    def _():
        wait_tile(i)


def _dispatch(h, posg, nchunks, dest, tail_start, tail_n, *, nb):
    T, D = h.shape
    nt, _, tm = posg.shape
    ne = tail_n.shape[0]
    return pl.pallas_call(
        functools.partial(_dispatch_kernel, ne=ne),
        grid_spec=pltpu.PrefetchScalarGridSpec(
            num_scalar_prefetch=4, grid=(nt,),
            in_specs=[pl.BlockSpec((tm, D), lambda i, *_: (i, 0)),
                      pl.BlockSpec((1, 2 * TOP_K, tm), lambda i, *_: (i, 0, 0))],
            out_specs=pl.BlockSpec(memory_space=pl.ANY),
            scratch_shapes=[pltpu.VMEM((2, _slots(tm, ne), D), BF16),
                            pltpu.VMEM((CHUNK, D), BF16),
                            pltpu.SemaphoreType.DMA((2,)), pltpu.SemaphoreType.DMA(())]),
        out_shape=jax.ShapeDtypeStruct((nb * MOE_BLOCK, D), BF16),
        compiler_params=_cparams(("arbitrary",)),
        name="dispatch",
    )(nchunks, dest, tail_start, tail_n, h, posg)


def _experts_kernel(be_ref, nu_ref, nxt_ref, edge_ref, xs_ref, wgu_hbm, bgu_ref, wd_hbm, bd_ref,
                    ys_ref, stage_gu, stage_d, wgu_s, wd_s, sem):
    b = pl.program_id(0)

    @pl.when(b < nu_ref[0])
    def _():
        edge = edge_ref[b]
        half = edge // 4
        nxt = nxt_ref[b]

        def weight_copies(e):
            return (pltpu.make_async_copy(wgu_hbm.at[e], stage_gu, sem.at[0]),
                    pltpu.make_async_copy(wd_hbm.at[e], stage_d, sem.at[1]))

        def cast_into(h):
            wgu_s[h] = stage_gu[...].astype(BF16)
            wd_s[h] = stage_d[...].astype(BF16)

        @pl.when(b == 0)
        def _():
            for cp in weight_copies(be_ref[b]):
                cp.start()
            for cp in weight_copies(be_ref[b]):
                cp.wait()
            cast_into(half)

        @pl.when((edge % 2 == 1) & (nxt >= 0))
        def _():
            for cp in weight_copies(nxt):
                cp.start()

        ff = wd_s.shape[1]
        gu = jnp.dot(xs_ref[...], wgu_s[half], preferred_element_type=F32) + bgu_ref[0]
        g = jnp.minimum(gu[:, :ff], SWIGLU_LIMIT)
        u = jnp.clip(gu[:, ff:], -SWIGLU_LIMIT, SWIGLU_LIMIT)
        act = (u + 1.0) * g * jax.nn.sigmoid(SWIGLU_ALPHA * g)
        ys = jnp.dot(act.astype(BF16), wd_s[half], preferred_element_type=F32) + bd_ref[0]
        ys_ref[...] = ys.astype(ys_ref.dtype)

        @pl.when(((edge // 2) % 2 == 1) & (nxt >= 0))
        def _():
            for cp in weight_copies(nxt):
                cp.wait()
            cast_into(1 - half)


def _experts(xs, blk_e, nused, nxt, edge, w_gate_up, b_gate_up, w_down, b_down):
    rows, D = xs.shape
    nb = rows // MOE_BLOCK
    ne, _, ff2 = w_gate_up.shape
    ff = w_down.shape[1]
    blk = lambda b, be, nu, *_: (jnp.minimum(b, nu[0] - 1), 0)
    exp3 = lambda b, be, *_: (be[b], 0, 0)
    return pl.pallas_call(
        _experts_kernel,
        grid_spec=pltpu.PrefetchScalarGridSpec(
            num_scalar_prefetch=4, grid=(nb,),
            in_specs=[pl.BlockSpec((MOE_BLOCK, D), blk),
                      pl.BlockSpec(memory_space=pl.ANY), pl.BlockSpec((1, 1, ff2), exp3),
                      pl.BlockSpec(memory_space=pl.ANY), pl.BlockSpec((1, 1, D), exp3)],
            out_specs=pl.BlockSpec((MOE_BLOCK, D), blk),
            scratch_shapes=[pltpu.VMEM((D, ff2), F32), pltpu.VMEM((ff, D), F32),
                            pltpu.VMEM((2, D, ff2), BF16), pltpu.VMEM((2, ff, D), BF16),
                            pltpu.SemaphoreType.DMA((2,))]),
        out_shape=jax.ShapeDtypeStruct((rows, D), BF16),
        compiler_params=_cparams(("arbitrary",)),
        name="experts",
    )(blk_e, nused, nxt, edge, xs, w_gate_up, b_gate_up[:, None, :], w_down,
      b_down[:, None, :])


def _combine_kernel(n_ref, dest_ref, ys_hbm, posg_ref, x1_ref, y_ref, ylast_ref, buf, sem):
    i = pl.program_id(0)
    last = pl.num_programs(0) - 1
    tm = x1_ref.shape[0]
    nslots = buf.shape[1]

    def run_copy(tile, c):
        s = tile % 2
        return pltpu.make_async_copy(ys_hbm.at[_chunk_rows(dest_ref[tile, c])],
                                     buf.at[s, _chunk_rows(c)], sem.at[s])

    def fetch(tile):
        _for_each_chunk(n_ref[tile], lambda c: run_copy(tile, c).start())
        s = tile % 2

        def zero_chunk(c, carry):
            buf[s, _chunk_rows(c), :] = jnp.zeros((CHUNK, buf.shape[2]), BF16)
            return carry
        lax.fori_loop(n_ref[tile], nslots // CHUNK, zero_chunk, 0)

    @pl.when(i == 0)
    def _():
        fetch(i)

    @pl.when(i < last)
    def _():
        fetch(i + 1)

    r0 = lax.broadcasted_iota(I32, (tm, tm), 0)
    r1 = lax.broadcasted_iota(I32, (tm, tm), 1)
    posg_t = _dot_nt(jnp.where(r0 == r1, 1.0, 0.0), posg_ref[0], exact=True)
    slot = lax.broadcasted_iota(I32, (tm, nslots), 1)
    w = jnp.zeros(slot.shape, F32)
    for k in range(TOP_K):
        w = jnp.where(slot == posg_t[:, k:k + 1].astype(I32),
                      posg_t[:, TOP_K + k:TOP_K + k + 1], w)

    _wait_chunks(n_ref[i], lambda r: pltpu.make_async_copy(
        ys_hbm.at[0:r], buf.at[i % 2, 0:r], sem.at[i % 2]))
    y = x1_ref[...] + jnp.dot(w.astype(BF16), buf[i % 2], preferred_element_type=F32)

    @pl.when(i < last)
    def _():
        y_ref[...] = y

    @pl.when(i == last)
    def _():
        ylast_ref[...] = y


def _combine(ys, posg, x1, nchunks, dest):
    T, D = x1.shape
    nt, _, tm = posg.shape
    return pl.pallas_call(
        _combine_kernel,
        grid_spec=pltpu.PrefetchScalarGridSpec(
            num_scalar_prefetch=2, grid=(nt,),
            in_specs=[pl.BlockSpec(memory_space=pl.ANY),
                      pl.BlockSpec((1, 2 * TOP_K, tm), lambda i, *_: (i, 0, 0)),
                      pl.BlockSpec((tm, D), lambda i, *_: (i, 0))],
            out_specs=[pl.BlockSpec((tm, D), lambda i, *_: (jnp.minimum(i, nt - 2), 0)),
                       pl.BlockSpec((tm, D), lambda i, *_: (0, 0))],
            scratch_shapes=[pltpu.VMEM((2, dest.shape[1] * CHUNK, D), BF16),
                            pltpu.SemaphoreType.DMA((2,))]),
        out_shape=[jax.ShapeDtypeStruct((T - tm, D), F32), jax.ShapeDtypeStruct((tm, D), F32)],
        compiler_params=_cparams(("arbitrary",)),
        name="combine",
    )(nchunks, dest, ys, posg, x1)


def _moe(h, posg, cnt, x1, w_gate_up, b_gate_up, w_down, b_down):
    nt, _, tm = posg.shape
    ne = cnt.shape[1]
    max_chunks = (nt * tm * TOP_K) // CHUNK + nt * ne
    nb = -(-max_chunks // CHUNKS_PER_BLOCK) + ne
    nchunks, dest, tail_start, tail_n, nused, blk_e, nxt, edge = _moe_tables(
        cnt, nb, _slots(tm, ne) // CHUNK)
    xs = _dispatch(h, posg, nchunks, dest, tail_start, tail_n, nb=nb)
    ys = _experts(xs, blk_e, nused, nxt, edge, w_gate_up, b_gate_up, w_down, b_down)
    return _combine(ys, posg, x1, nchunks, dest)


def kernel(x_prompt, x_sample, cache_k, cache_v, state_conv, page_table, norm1_w, w_in,
           q_norm_w, k_norm_w, lambda_q1, lambda_k1, lambda_q2, lambda_k2, subln_w,
           conv_w, conv_b, conv_ln_g, conv_ln_b, w_out, norm2_w, router_w, router_b,
           w_gate_up, b_gate_up, w_down, b_down):
    B, S, D = x_prompt.shape
    Bs, Ss, _ = x_sample.shape
    depth = norm1_w.shape[0]
    n_phys, page, heads, _, qk = cache_k.shape[1:]
    vdim = cache_v.shape[-1]
    qc, vc, cc = heads * 2 * qk, heads * vdim, conv_w.shape[2]
    taps = conv_w.shape[1]
    assert Ss == 1 and qk == QK_GROUP and vdim == V7X_LANES and (B * S) % MOE_TILE == 0
    assert Bs <= MOE_TILE and S >= taps - 1
    n_past = page_table.shape[1] * page
    T = B * S
    nt_p = T // MOE_TILE
    pos_p = jnp.arange(S, dtype=F32)
    pos_s = jnp.full((Bs,), n_past, F32)
    xp = x_prompt.reshape(T, D)
    xs = x_sample.reshape(Bs, D)
    pad_tile = lambda a: jnp.pad(a, ((0, MOE_TILE - Bs), (0, 0)))
    outs = [[] for _ in range(6)]
    for l in range(depth):
        lam_init = 0.8 - 0.6 * math.exp(-0.3 * l)
        lamv = jnp.stack([lambda_q1[l], lambda_k1[l], lambda_q2[l], lambda_k2[l]])
        conv_p = (conv_w[l], conv_b[l], conv_ln_g[l], conv_ln_b[l])
        proj_p = (norm1_w[l], w_in[l], q_norm_w[l], k_norm_w[l])
        tail_p = (w_out[l], norm2_w[l], router_w[l], router_b[l])

        q, kt, v, u, kbt, vb = _proj(xp, pos_p, S // PROJ_TILE, PROJ_TILE, *proj_p,
                                     qc=qc, vc=vc, cc=cc, exact_norm=False, attn_layout=True)
        qs, ks_, vs_, us = _proj(xs, pos_s, 1, Bs, *proj_p, qc=qc, vc=vc, cc=cc,
                                 exact_norm=True, attn_layout=False)
        cache_kt = jnp.transpose(cache_k[l], (0, 2, 3, 4, 1)).reshape(n_phys, qc, page)
        cache_vr = cache_v[l].reshape(n_phys, page * heads, vdim)
        attn, attn_s = _attention(q, kbt, vb, qs, ks_, vs_, cache_kt, cache_vr, page_table, lamv,
                                  subln_w[l], lam_init=lam_init)

        bufs = _tail(attn, u, xp, *tail_p, tm=MOE_TILE, n_valid=MOE_TILE,
                     total_tiles=nt_p + 1, conv_params=conv_p, seq=S)
        outs[0].append(jnp.transpose(kt.reshape(B, heads, 2, qk, S), (0, 4, 1, 2, 3)))
        outs[1].append(v.reshape(B, S, heads, vdim))
        outs[2].append(u.reshape(B, S, cc)[:, S - (taps - 1):])

        conv_s = _conv_step(state_conv[l], us, *conv_p)
        x1, h, posg, cnt = _tail(pad_tile(attn_s), pad_tile(conv_s), pad_tile(xs), *tail_p,
                                 tm=MOE_TILE, n_valid=Bs, total_tiles=nt_p + 1,
                                 first_tile=nt_p, into=bufs)
        outs[3].append(ks_.reshape(Bs, Ss, heads, 2, qk))
        outs[4].append(vs_.reshape(Bs, Ss, heads, vdim))
        outs[5].append(jnp.concatenate([state_conv[l][:, Ss:], us[:, None, :]], axis=1))

        xp, y_last = _moe(h, posg, cnt[:, :, 0], x1, w_gate_up[l], b_gate_up[l], w_down[l],
                          b_down[l])
        xs = y_last[:Bs]
    return (xp.reshape(B, S, D), xs.reshape(Bs, Ss, D)) + tuple(jnp.stack(o) for o in outs)
```

```python
import functools
import math

import jax
import jax.numpy as jnp
from jax import lax
from jax.experimental import pallas as pl
from jax.experimental.pallas import tpu as pltpu

F32 = jnp.float32
BF16 = jnp.bfloat16
I32 = jnp.int32
HIGHEST = lax.Precision.HIGHEST

EPS = 1e-6
ROPE_THETA = 10000.0
SWIGLU_LIMIT = 7.0
SWIGLU_ALPHA = 1.702
TOP_K = 4
NEG = -1e30
QK_GROUP = 64

V7X_LANES = 128
V7X_SUBLANES = 8
VMEM_LIMIT = 56 * 1024 * 1024
BF16_ROWS = 16

MOE_TILE = 256
CHUNK = BF16_ROWS
MOE_BLOCK = 512
CHUNKS_PER_BLOCK = MOE_BLOCK // CHUNK
PROJ_TILE = 512


def _cparams(sem, vmem=VMEM_LIMIT):
    return pltpu.CompilerParams(dimension_semantics=sem, vmem_limit_bytes=vmem)


def _dot(a, b, exact=False):
    if exact:
        return jnp.dot(a.astype(F32), b.astype(F32), precision=HIGHEST,
                       preferred_element_type=F32)
    return jnp.dot(a.astype(BF16), b.astype(BF16), preferred_element_type=F32)


def _dot_nt(a, b, exact=False):
    dn = (((1,), (1,)), ((), ()))
    if exact:
        return lax.dot_general(a.astype(F32), b.astype(F32), dn, precision=HIGHEST,
                               preferred_element_type=F32)
    return lax.dot_general(a.astype(BF16), b.astype(BF16), dn, preferred_element_type=F32)


def _bf16_round(x):
    return x.astype(BF16).astype(F32)


def _rope_norm(p, gsum, w, cos, sin, first_half, exact_norm):
    ss = _dot(p * p, gsum, exact_norm)
    n = p * lax.rsqrt(ss * (1.0 / QK_GROUP) + EPS) * w
    outs = []
    for j in range(p.shape[1] // V7X_LANES):
        nj = n[:, j * V7X_LANES:(j + 1) * V7X_LANES]
        rot = jnp.where(first_half, pltpu.roll(nj, V7X_LANES - QK_GROUP // 2, 1),
                        pltpu.roll(nj, QK_GROUP // 2, 1))
        outs.append(nj * cos + rot * sin)
    return jnp.concatenate(outs, axis=1)


def _proj_kernel(x_ref, n1_ref, w_ref, qw_ref, kw_ref, cos_ref, sin_ref, gsum_ref,
                 q_ref, k_ref, v_ref, u_ref, *rest, qc, vc, cc, scale, exact_norm):
    x = x_ref[...]
    h = x * lax.rsqrt(jnp.mean(x * x, axis=-1, keepdims=True) + EPS) * n1_ref[...]
    hm = h.astype(BF16)
    cos = cos_ref[...]
    sin = sin_ref[...]
    lane = lax.broadcasted_iota(I32, cos.shape, 1)
    first_half = (lane % QK_GROUP) < QK_GROUP // 2
    gsum = gsum_ref[...]

    q = _rope_norm(_dot(hm, w_ref[:, 0:qc]), gsum, qw_ref[...], cos, sin, first_half, exact_norm)
    q_ref[...] = (q * scale).astype(q_ref.dtype)
    k = _rope_norm(_dot(hm, w_ref[:, qc:2 * qc]), gsum, kw_ref[...], cos, sin, first_half,
                   exact_norm)
    v = _dot(hm, w_ref[:, 2 * qc:2 * qc + vc])
    o = 2 * qc + vc
    ua = _dot(hm, w_ref[:, o:o + cc])
    ub = _dot(hm, w_ref[:, o + cc:o + 2 * cc])
    u_ref[...] = ua * jax.nn.sigmoid(ub)
    if rest:
        kb_ref, vb_ref = rest
        kt = k.T
        k_ref[0] = kt
        kb_ref[0, :, 0] = kt.astype(BF16).reshape(kb_ref.shape[1], kb_ref.shape[3], kt.shape[1])
        vb_ref[...] = v.astype(BF16)
        nh = vc // V7X_LANES
        for h in range(nh):
            v_ref[pl.ds(h, v.shape[0], stride=nh), :] = v[:, h * V7X_LANES:(h + 1) * V7X_LANES]
    else:
        k_ref[...] = k
        v_ref[...] = v


def _rope_tables(pos):
    half = QK_GROUP // 2
    inv = jnp.power(ROPE_THETA, -jnp.arange(half, dtype=F32) / half)
    ang = pos[:, None] * inv[None, :]
    reps = V7X_LANES // QK_GROUP
    cos = jnp.tile(jnp.cos(ang), (1, 2 * reps))
    s = jnp.sin(ang)
    sin = jnp.tile(jnp.concatenate([-s, s], axis=1), (1, reps))
    return cos, sin


def _proj(x2d, pos_rows, n_pos_blocks, tm, norm1_w, w_in, q_norm_w, k_norm_w, *, qc, vc, cc,
          exact_norm, attn_layout):
    T, D = x2d.shape
    cos, sin = _rope_tables(pos_rows)
    gi = jnp.arange(qc) // QK_GROUP
    gsum = (gi[:, None] == gi[None, :]).astype(F32 if exact_norm else BF16)
    qw = jnp.tile(q_norm_w, qc // QK_GROUP)[None, :]
    kw = jnp.tile(k_norm_w, qc // QK_GROUP)[None, :]
    w = w_in.astype(BF16)
    row = lambda i: (i, 0)
    full = lambda i: (0, 0)
    out_shape = [jax.ShapeDtypeStruct((T, qc), BF16),
                 jax.ShapeDtypeStruct((T, qc), F32),
                 jax.ShapeDtypeStruct((T, vc), F32),
                 jax.ShapeDtypeStruct((T, cc), F32)]
    out_specs = [pl.BlockSpec((tm, qc), row), pl.BlockSpec((tm, qc), row),
                 pl.BlockSpec((tm, vc), row), pl.BlockSpec((tm, cc), row)]
    if attn_layout:
        nseq = T // (n_pos_blocks * tm)
        heads = qc // (2 * QK_GROUP)
        seq_tile = lambda i: (i // n_pos_blocks, 0, i % n_pos_blocks)
        out_shape[1] = jax.ShapeDtypeStruct((nseq, qc, n_pos_blocks * tm), F32)
        out_specs[1] = pl.BlockSpec((1, qc, tm), seq_tile)
        out_shape[2] = jax.ShapeDtypeStruct((T * vc // V7X_LANES, V7X_LANES), F32)
        out_specs[2] = pl.BlockSpec((tm * vc // V7X_LANES, V7X_LANES), row)
        out_shape += [jax.ShapeDtypeStruct((nseq, heads, n_pos_blocks, 2 * QK_GROUP, tm), BF16),
                      jax.ShapeDtypeStruct((T, vc), BF16)]
        out_specs += [pl.BlockSpec((1, heads, 1, 2 * QK_GROUP, tm),
                                   lambda i: (i // n_pos_blocks, 0, i % n_pos_blocks, 0, 0)),
                      pl.BlockSpec((tm, vc), row)]
    return pl.pallas_call(
        functools.partial(_proj_kernel, qc=qc, vc=vc, cc=cc, scale=QK_GROUP ** -0.5,
                          exact_norm=exact_norm),
        grid=(T // tm,),
        in_specs=[pl.BlockSpec((tm, D), row),
                  pl.BlockSpec((1, D), full),
                  pl.BlockSpec(w.shape, full),
                  pl.BlockSpec((1, qc), full),
                  pl.BlockSpec((1, qc), full),
                  pl.BlockSpec((tm, V7X_LANES), lambda i: (i % n_pos_blocks, 0)),
                  pl.BlockSpec((tm, V7X_LANES), lambda i: (i % n_pos_blocks, 0)),
                  pl.BlockSpec((qc, qc), full)],
        out_specs=out_specs,
        out_shape=out_shape,
        compiler_params=_cparams(("arbitrary",)),
        name="proj" if attn_layout else "proj_step",
    )(x2d, norm1_w[None, :], w, qw, kw, cos, sin, gsum)


def _lambda_value(lv, lam_init):
    a = jnp.sum(lv[0:1] * lv[1:2], axis=-1, keepdims=True)
    b = jnp.sum(lv[2:3] * lv[3:4], axis=-1, keepdims=True)
    return jnp.exp(a) - jnp.exp(b) + lam_init


def _subln(o, w, lam_init):
    y = o * lax.rsqrt(jnp.mean(o * o, axis=-1, keepdims=True) + EPS)
    return y * w * (1.0 - lam_init)


def _attn_body(i, lam_ref, sw_ref, q_ref, k_ref, v_ref, o_ref, *, tq, lam_init):
    lam = _lambda_value(lam_ref[...], lam_init)
    q = q_ref[...]
    lane = lax.broadcasted_iota(I32, q.shape, 1)
    zero = jnp.zeros_like(q)
    qs = (jnp.where(lane < QK_GROUP, q, zero), jnp.where(lane >= QK_GROUP, q, zero))

    def chunk(j, carry, masked):
        kc = k_ref[0, 0, j]
        vc = v_ref[pl.ds(pl.multiple_of(j * tq, tq), tq), :]
        out = []
        for c in range(2):
            m, l, acc = carry[c]
            s = jnp.dot(qs[c], kc, preferred_element_type=F32)
            if masked:
                row = lax.broadcasted_iota(I32, s.shape, 0)
                col = lax.broadcasted_iota(I32, s.shape, 1)
                s = jnp.where(col <= row, s, NEG)
            m_new = jnp.maximum(m, jnp.max(s, axis=-1, keepdims=True))
            p = jnp.exp(s - m_new)
            alpha = jnp.exp(m - m_new)
            l = alpha * l + jnp.sum(p, axis=-1, keepdims=True)
            acc = alpha * acc + _dot(p, vc)
            out.append((m_new, l, acc))
        return tuple(out)

    init = tuple((jnp.full((tq, 1), NEG, F32), jnp.zeros((tq, 1), F32),
                  jnp.zeros((tq, V7X_LANES), F32)) for _ in range(2))
    carry = lax.fori_loop(0, i, lambda j, c: chunk(j, c, False), init)
    (_, l0, a0), (_, l1, a1) = chunk(i, carry, True)
    o = a0 / l0 - lam * (a1 / l1)
    o_ref[...] = _subln(o, sw_ref[...], lam_init).astype(o_ref.dtype)


DECODE_PAGES_PER_STEP = 32
SOFTMAX_PAGES = 16
PAGE_RING_DEPTH = 3


def _decode_body(s, half_steps, lam_ref, sw_ref, qm_ref, kn_ref, vn_ref, pages_ref,
                 o_ref, s_ref, m_ref, coef_ref, acc_ref, *, page, heads, lam_init):
    pps = pages_ref.shape[0]
    rows = s_ref.shape[1]
    qm = qm_ref[0]
    row = lax.broadcasted_iota(I32, (rows, V7X_LANES), 0)
    n_pages = half_steps * pps

    @pl.when(s == 0)
    def _():
        m_ref[...] = jnp.full(m_ref.shape, NEG, F32)

    @pl.when(s < half_steps)
    def _():
        m = m_ref[...]
        for j in range(pps):
            sc = jnp.dot(qm, pages_ref[j].astype(BF16), preferred_element_type=F32)
            s_ref[s * pps + j] = sc
            m = jnp.maximum(m, sc)
        m_ref[...] = m

    def head_weights(p):
        a = p * coef_ref[...]
        return _bf16_round(a + pltpu.roll(a, rows - heads, 0))

    @pl.when(s == half_steps)
    def _():
        s_new = jnp.sum(qm.astype(F32) * _bf16_round(kn_ref[0]), axis=-1, keepdims=True)
        m = jnp.maximum(jnp.max(m_ref[...], axis=-1, keepdims=True), s_new)

        def exp_pages(t, l):
            pages = pl.ds(pl.multiple_of(t * SOFTMAX_PAGES, SOFTMAX_PAGES), SOFTMAX_PAGES)
            p = jnp.exp(s_ref[pages] - m[None])
            s_ref[pages] = p
            return l + jnp.sum(p, axis=0)

        lsum = lax.fori_loop(0, n_pages // SOFTMAX_PAGES, exp_pages,
                             jnp.zeros((rows, V7X_LANES), F32))
        p_new = jnp.exp(s_new - m)
        l = jnp.sum(lsum, axis=-1, keepdims=True) + p_new
        lam = _lambda_value(lam_ref[...], lam_init)
        coef = jnp.where(row[:, 0:1] < heads, 1.0, -lam) / l
        coef_ref[...] = jnp.broadcast_to(coef, coef_ref.shape)
        acc_ref[...] = (head_weights(jnp.broadcast_to(p_new, (rows, V7X_LANES)))
                        * _bf16_round(vn_ref[0]))

    @pl.when(s >= half_steps)
    def _():
        lane = lax.broadcasted_iota(I32, (rows, V7X_LANES), 1)
        keep = (lane % heads == row) & (row < heads)
        acc = acc_ref[...]
        for j in range(pps):
            a = head_weights(s_ref[(s - half_steps) * pps + j])
            parts = []
            for c in range(heads):
                idx = (c * page + lane) // heads
                parts.append(jnp.where(keep, jnp.take_along_axis(a, idx, axis=1), 0.0))
            a_exp = jnp.concatenate(parts, axis=1).astype(BF16)
            acc = acc + jnp.dot(a_exp, pages_ref[j].astype(BF16), preferred_element_type=F32)
        acc_ref[...] = acc

    @pl.when(s == 2 * half_steps - 1)
    def _():
        o_ref[0] = _subln(acc_ref[...], sw_ref[...], lam_init)


def _attention_kernel(pt_ref, lam_ref, sw_ref, q_ref, k_ref, v_ref, qm_ref, kn_ref, vn_ref,
                      kt_hbm, vr_hbm, o_ref, od_ref, s_ref, m_ref, coef_ref, acc_ref, pages, sem,
                      *, ratio, nq, steps_per_seq, tq, page, heads, lam_init):
    t = pl.program_id(0)
    depth, pps = pages.shape[:2]
    half_steps = steps_per_seq // 2

    def fetch(step):
        seq = step // steps_per_seq
        s = step % steps_per_seq
        first = (s % half_steps) * pps
        slot = step % depth
        for src_hbm, cond in ((kt_hbm, s < half_steps), (vr_hbm, s >= half_steps)):
            @pl.when(cond)
            def _():
                for j in range(pps):
                    pltpu.make_async_copy(src_hbm.at[pt_ref[seq, first + j]],
                                          pages.at[slot, j], sem.at[slot]).start()

    @pl.when(t == 0)
    def _():
        for ahead in range(depth - 1):
            fetch(t + ahead)

    @pl.when(t + depth - 1 < pl.num_programs(0))
    def _():
        fetch(t + depth - 1)

    slot = t % depth
    pltpu.make_async_copy(kt_hbm.at[0:pps], pages.at[slot], sem.at[slot]).wait()
    _decode_body(t % steps_per_seq, half_steps, lam_ref, sw_ref, qm_ref, kn_ref, vn_ref,
                 pages.at[slot], od_ref, s_ref, m_ref, coef_ref, acc_ref,
                 page=page, heads=heads, lam_init=lam_init)

    @pl.when(t % ratio == 0)
    def _():
        _attn_body((t // ratio) % nq, lam_ref, sw_ref, q_ref, k_ref, v_ref, o_ref,
                   tq=tq, lam_init=lam_init)


def _attention(q, kbt, vb, q_step, k_new, v_new, cache_kt, cache_vr, page_table, lamv, subln_w,
               *, lam_init):
    batch, heads, nq, _, tq = kbt.shape
    seq = nq * tq
    Bs, D = q_step.shape
    n_pages = page_table.shape[1]
    page = cache_kt.shape[2]
    vdim = cache_vr.shape[2]
    pps = DECODE_PAGES_PER_STEP
    half_steps = n_pages // pps
    steps_per_seq = 2 * half_steps
    rows = 2 * heads
    n_dec, n_att = Bs * steps_per_seq, batch * heads * nq
    assert rows == V7X_SUBLANES and page == V7X_LANES and vdim == V7X_LANES
    assert n_pages % pps == 0 and n_pages % SOFTMAX_PAGES == 0 and n_dec % n_att == 0
    assert cache_kt.shape[1:] == cache_vr.shape[1:]
    ratio = n_dec // n_att
    group = jnp.arange(D) // QK_GROUP
    rowmask = ((group % 2) * heads + group // 2)[None, :] == jnp.arange(rows)[:, None]
    qm = jnp.where(rowmask[None], q_step[:, None, :], jnp.zeros((), BF16))
    vn = jnp.pad(v_new.reshape(Bs, heads, vdim), ((0, 0), (0, rows - heads), (0, 0)))

    def att(t):
        a = t // ratio
        return a // (heads * nq), (a // nq) % heads, a % nq

    def q_map(t, pt):
        b, h, i = att(t)
        return b * nq + i, h

    seq3 = lambda t, pt: (t // steps_per_seq, 0, 0)
    const = lambda t, pt: (0, 0)
    out, out_step = pl.pallas_call(
        functools.partial(_attention_kernel, ratio=ratio, nq=nq,
                          steps_per_seq=steps_per_seq, tq=tq, page=page, heads=heads,
                          lam_init=lam_init),
        grid_spec=pltpu.PrefetchScalarGridSpec(
            num_scalar_prefetch=1, grid=(n_dec,),
            in_specs=[pl.BlockSpec(lamv.shape, const), pl.BlockSpec((1, vdim), const),
                      pl.BlockSpec((tq, V7X_LANES), q_map),
                      pl.BlockSpec((1, 1, nq, V7X_LANES, tq),
                                   lambda t, pt: att(t)[:2] + (0, 0, 0)),
                      pl.BlockSpec((seq, V7X_LANES), lambda t, pt: att(t)[:2]),
                      pl.BlockSpec((1, rows, D), seq3), pl.BlockSpec((1, 1, D), seq3),
                      pl.BlockSpec((1, rows, vdim), seq3),
                      pl.BlockSpec(memory_space=pl.ANY), pl.BlockSpec(memory_space=pl.ANY)],
            out_specs=[pl.BlockSpec((tq, V7X_LANES), q_map),
                       pl.BlockSpec((1, rows, vdim), seq3)],
            scratch_shapes=[pltpu.VMEM((n_pages, rows, page), F32),
                            pltpu.VMEM((rows, V7X_LANES), F32),
                            pltpu.VMEM((rows, V7X_LANES), F32),
                            pltpu.VMEM((rows, vdim), F32),
                            pltpu.VMEM((PAGE_RING_DEPTH, pps) + cache_kt.shape[1:], F32),
                            pltpu.SemaphoreType.DMA((PAGE_RING_DEPTH,))]),
        out_shape=[jax.ShapeDtypeStruct((batch * seq, heads * V7X_LANES), BF16),
                   jax.ShapeDtypeStruct((Bs, rows, vdim), F32)],
        compiler_params=_cparams(("arbitrary",)),
        name="attention",
    )(page_table, lamv, subln_w[None, :], q, kbt, vb, qm, k_new[:, None, :], vn,
      cache_kt, cache_vr)
    return out, out_step[:, :heads, :].reshape(Bs, heads * vdim)


CONV_HALO = 32


def _ln_swish(y, b_ref, g_ref, be_ref):
    y = y + b_ref[...]
    mu = jnp.mean(y, axis=-1, keepdims=True)
    yc = y - mu
    z = yc * lax.rsqrt(jnp.mean(yc * yc, axis=-1, keepdims=True) + EPS) * g_ref[...] + be_ref[...]
    return z * jax.nn.sigmoid(z)


def _conv_tile(u_ref, starts_sequence, w_ref, b_ref, g_ref, be_ref, buf_ref, part_ref):
    sub = V7X_SUBLANES
    tc = u_ref.shape[0]
    taps = w_ref.shape[0]
    first = CONV_HALO - (taps - 1)

    @pl.when(starts_sequence)
    def _():
        buf_ref[0:CONV_HALO, :] = jnp.zeros((CONV_HALO, buf_ref.shape[1]), F32)
        buf_ref[CONV_HALO + tc:, :] = jnp.zeros((sub, buf_ref.shape[1]), F32)

    buf_ref[CONV_HALO:CONV_HALO + tc, :] = _bf16_round(u_ref[...])
    acc = None
    for r in range(sub):
        part = None
        for a in range(-(-(first + taps) // sub)):
            k = sub * a + r - first
            if 0 <= k < taps:
                term = buf_ref[sub * a:sub * a + tc + sub, :] * _bf16_round(w_ref[k:k + 1, :])
                part = term if part is None else part + term
        if r == 0:
            acc = part[0:tc]
        else:
            part_ref[...] = part
            acc = acc + part_ref[r:r + tc, :]
    out = _ln_swish(acc, b_ref, g_ref, be_ref)
    buf_ref[0:CONV_HALO, :] = buf_ref[tc:tc + CONV_HALO, :]
    return out


def _conv_step_kernel(st_ref, u_ref, w_ref, b_ref, g_ref, be_ref, o_ref, *, taps):
    acc = u_ref[...] * w_ref[taps - 1:taps, :]
    for k in range(taps - 1):
        acc = acc + st_ref[:, k, :] * w_ref[k:k + 1, :]
    o_ref[...] = _ln_swish(acc, b_ref, g_ref, be_ref)


def _conv_step(state, u, conv_w, conv_b, ln_g, ln_b):
    taps, C = conv_w.shape
    return pl.pallas_call(
        functools.partial(_conv_step_kernel, taps=taps),
        out_shape=jax.ShapeDtypeStruct(u.shape, F32),
        compiler_params=_cparams(None),
        name="conv_step",
    )(state, u, conv_w, conv_b[None, :], ln_g[None, :], ln_b[None, :])


def _tail_kernel(a_ref, c_ref, x_ref, wo_ref, n2_ref, rw_ref, rb_ref, *rest, n_valid,
                 tiles_per_seq):
    tm = x_ref.shape[0]
    half = a_ref.shape[1]
    if tiles_per_seq:
        cw_ref, cb_ref, cg_ref, cbe_ref = rest[:4]
        buf_ref, part_ref = rest[-2:]
        x1_ref, h_ref, posg_ref, cnt_ref = rest[-6:-2]
        conv = _conv_tile(c_ref, pl.program_id(0) % tiles_per_seq == 0, cw_ref, cb_ref, cg_ref,
                          cbe_ref, buf_ref, part_ref)
    else:
        x1_ref, h_ref, posg_ref, cnt_ref = rest[-4:]
        conv = c_ref[...]
    x1 = (x_ref[...] + _dot(a_ref[...], wo_ref[0:half, :]) + _dot(conv, wo_ref[half:, :]))
    x1_ref[...] = x1
    h = (x1 * lax.rsqrt(jnp.mean(x1 * x1, axis=-1, keepdims=True) + EPS)
         * n2_ref[...]).astype(BF16)
    h_ref[...] = h
    logits = _dot_nt(rw_ref[...], h) + rb_ref[...]
    ne = logits.shape[0]
    eidx = lax.broadcasted_iota(I32, logits.shape, 0)
    valid = lax.broadcasted_iota(I32, (1, tm), 1) < n_valid

    sels, vals = [], []
    l = logits
    for _ in range(TOP_K):
        m = jnp.max(l, axis=0, keepdims=True)
        first = jnp.min(jnp.where(l == m, eidx, ne), axis=0, keepdims=True)
        sel = (eidx == first) & valid
        l = jnp.where(eidx == first, -jnp.inf, l)
        sels.append(sel)
        vals.append(m)
    ex = [jnp.exp(v - vals[0]) for v in vals]
    den = ex[0] + ex[1] + ex[2] + ex[3]
    gates = [jnp.where(valid, e / den, 0.0) for e in ex]

    msel = jnp.zeros(logits.shape, F32)
    for sel in sels:
        msel = msel + jnp.where(sel, 1.0, 0.0)
    r0 = lax.broadcasted_iota(I32, (tm, tm), 0)
    r1 = lax.broadcasted_iota(I32, (tm, tm), 1)
    upper = jnp.where(r0 < r1, 1.0, 0.0).astype(BF16)
    rank = jnp.dot(msel.astype(BF16), upper, preferred_element_type=F32)
    cnt = jnp.sum(msel, axis=1, keepdims=True)
    pcnt = jnp.ceil(cnt * (1.0 / CHUNK)) * CHUNK
    e0 = lax.broadcasted_iota(I32, (ne, ne), 0)
    e1 = lax.broadcasted_iota(I32, (ne, ne), 1)
    lower = jnp.where(e1 < e0, 1.0, 0.0)
    off = jnp.dot(lower.astype(BF16), jnp.broadcast_to(pcnt, (ne, V7X_LANES)).astype(BF16),
                  preferred_element_type=F32)[:, 0:1]
    pos = off + rank
    rows = [jnp.where(valid, jnp.sum(jnp.where(sel, pos, 0.0), axis=0, keepdims=True), -1.0)
            for sel in sels]
    posg_ref[0] = jnp.concatenate(rows + gates, axis=0)
    cnt_ref[0] = jnp.broadcast_to(cnt, (ne, V7X_LANES)).astype(I32)


def _tail(attn, conv, x2d, w_out, norm2_w, router_w, router_b, *, tm, n_valid, total_tiles,
          first_tile=0, into=None, conv_params=None, seq=None):
    T, D = x2d.shape
    half = attn.shape[1]
    ne = router_w.shape[1]
    nt = T // tm
    wo = w_out.astype(BF16)
    rw = router_w.T.astype(BF16)
    row = lambda i: (i, 0)
    full = lambda i: (0, 0)
    orow = lambda i: (first_tile + i, 0)
    otile = lambda i: (first_tile + i, 0, 0)
    extra = list(into) if into is not None else []
    conv_in, conv_specs, scratch, tiles_per_seq = [], [], [], 0
    if conv_params is not None:
        conv_w, conv_b, ln_g, ln_b = conv_params
        taps, C = conv_w.shape
        conv_in = [conv_w, conv_b[None, :], ln_g[None, :], ln_b[None, :]]
        conv_specs = [pl.BlockSpec((taps, C), full)] + [pl.BlockSpec((1, C), full)] * 3
        scratch = [pltpu.VMEM((tm + CONV_HALO + V7X_SUBLANES, C), F32),
                   pltpu.VMEM((tm + V7X_SUBLANES, C), F32)]
        tiles_per_seq = seq // tm
    return pl.pallas_call(
        functools.partial(_tail_kernel, n_valid=n_valid, tiles_per_seq=tiles_per_seq),
        grid=(nt,),
        in_specs=[pl.BlockSpec((tm, half), row), pl.BlockSpec((tm, half), row),
                  pl.BlockSpec((tm, D), row), pl.BlockSpec((D, D), full),
                  pl.BlockSpec((1, D), full), pl.BlockSpec((ne, D), full),
                  pl.BlockSpec((ne, 1), full)] + conv_specs
                 + [pl.BlockSpec(memory_space=pl.ANY)] * len(extra),
        out_specs=[pl.BlockSpec((tm, D), orow), pl.BlockSpec((tm, D), orow),
                   pl.BlockSpec((1, 2 * TOP_K, tm), otile),
                   pl.BlockSpec((1, ne, V7X_LANES), otile)],
        out_shape=[jax.ShapeDtypeStruct((total_tiles * tm, D), F32),
                   jax.ShapeDtypeStruct((total_tiles * tm, D), BF16),
                   jax.ShapeDtypeStruct((total_tiles, 2 * TOP_K, tm), F32),
                   jax.ShapeDtypeStruct((total_tiles, ne, V7X_LANES), I32)],
        scratch_shapes=scratch,
        input_output_aliases={7 + len(conv_in) + j: j for j in range(len(extra))},
        compiler_params=_cparams(("arbitrary",)),
        name="tail",
    )(attn, conv, x2d, wo, norm2_w[None, :], rw, router_b[:, None], *conv_in, *extra)


def _slots(tm, ne):
    worst = TOP_K * tm + ne * (CHUNK - 1)
    return -(-worst // V7X_LANES) * V7X_LANES


def _prefix_sum(x, axis, exclusive):
    n = x.shape[axis]
    i = jnp.arange(n)
    tri = (i[:, None] < i[None, :]) if exclusive else (i[:, None] <= i[None, :])
    xm = jnp.moveaxis(x, axis, -1)
    out = jnp.sum(xm[..., :, None] * tri.astype(x.dtype), axis=-2)
    return jnp.moveaxis(out, -1, axis)


def _moe_tables(cnt, nb, slot_chunks):
    nch = (cnt + (CHUNK - 1)) // CHUNK
    tot = jnp.sum(nch, axis=0)
    nblk = (tot + (CHUNKS_PER_BLOCK - 1)) // CHUNKS_PER_BLOCK
    bend = _prefix_sum(nblk, 0, exclusive=False)
    gstart = (bend - nblk) * CHUNKS_PER_BLOCK
    rs = gstart[None, :] + _prefix_sum(nch, 0, exclusive=True)
    tail_start = gstart + tot
    tail_n = nblk * CHUNKS_PER_BLOCK - tot
    nused = bend[-1:]
    blk = jnp.minimum(jnp.arange(nb, dtype=I32), nused - 1)
    blk_e = jnp.sum((bend[None, :] <= blk[:, None]).astype(I32), axis=1)
    cend = _prefix_sum(nch, 1, exclusive=False)
    c = jnp.arange(slot_chunks, dtype=I32)
    run = jnp.minimum(jnp.sum((cend[:, None, :] <= c[None, :, None]).astype(I32), axis=2),
                      nch.shape[1] - 1)
    shift = rs - (cend - nch)
    dest = c[None, :] + jnp.sum(jnp.where(run[:, :, None] == jnp.arange(nch.shape[1]),
                                          shift[:, None, :], 0), axis=2)
    ne = nblk.shape[0]
    eid = jnp.arange(ne, dtype=I32)
    later = jnp.where((eid[None, :] > eid[:, None]) & (nblk[None, :] > 0), eid[None, :], ne)
    next_e = jnp.min(later, axis=1)
    next_e = jnp.where(next_e == ne, -1, next_e)
    rank_e = _prefix_sum((nblk > 0).astype(I32), 0, exclusive=True)
    bidx = jnp.arange(nb, dtype=I32)
    mine = (blk_e[:, None] == eid[None, :]).astype(I32)
    of_block = lambda per_expert: jnp.sum(mine * per_expert[None, :], axis=1)
    first = (bidx == of_block(bend - nblk)) & (bidx < nused)
    last = (bidx == of_block(bend) - 1) & (bidx < nused)
    edge = first.astype(I32) + 2 * last.astype(I32) + 4 * (of_block(rank_e) % 2)
    i32 = lambda a: a.astype(I32)
    return (i32(cend[:, -1]), i32(dest), i32(tail_start), i32(tail_n), i32(nused), i32(blk_e),
            i32(of_block(next_e)), i32(edge))


def _chunk_rows(c):
    return pl.ds(pl.multiple_of(c * CHUNK, CHUNK), CHUNK)


def _for_each_chunk(n, fn):
    def body(c, carry):
        fn(c)
        return carry
    lax.fori_loop(0, n, body, 0)


WAIT_GROUP = 8


def _wait_chunks(n, copy_of_rows):
    _for_each_chunk(n // WAIT_GROUP, lambda c: copy_of_rows(WAIT_GROUP * CHUNK).wait())
    _for_each_chunk(n % WAIT_GROUP, lambda c: copy_of_rows(CHUNK).wait())


def _one_hot_rows(pos, nrows):
    r = lax.broadcasted_iota(I32, (nrows, pos.shape[1]), 0)
    p = jnp.zeros(r.shape, F32)
    for k in range(TOP_K):
        p = jnp.where(r == pos[k:k + 1], 1.0, p)
    return p.astype(BF16)


def _dispatch_kernel(n_ref, dest_ref, ts_ref, tn_ref, h_ref, posg_ref, xs_hbm,
                     buf, zbuf, sem, zsem, *, ne):
    i = pl.program_id(0)
    slot = i % 2
    pos = posg_ref[0][0:TOP_K].astype(I32)
    buf[slot] = jnp.dot(_one_hot_rows(pos, buf.shape[1]), h_ref[...],
                        preferred_element_type=F32).astype(BF16)

    def run_copy(tile, c):
        s = tile % 2
        return pltpu.make_async_copy(buf.at[s, _chunk_rows(c)],
                                     xs_hbm.at[_chunk_rows(dest_ref[tile, c])], sem.at[s])

    def zero_copy(g):
        return pltpu.make_async_copy(zbuf, xs_hbm.at[_chunk_rows(g)], zsem)

    @pl.when(i == 0)
    def _():
        zbuf[...] = jnp.zeros(zbuf.shape, BF16)
        for phase in ("start", "wait"):
            def per_expert(e, carry):
                def body(j, c):
                    cp = zero_copy(ts_ref[e] + j)
                    cp.start() if phase == "start" else cp.wait()
                    return c
                return lax.fori_loop(0, tn_ref[e], body, carry)
            lax.fori_loop(0, ne, per_expert, 0)

    _for_each_chunk(n_ref[i], lambda c: run_copy(i, c).start())

    def wait_tile(tile):
        s = tile % 2
        _wait_chunks(n_ref[tile], lambda r: pltpu.make_async_copy(
            buf.at[s, 0:r], xs_hbm.at[0:r], sem.at[s]))

    @pl.when(i > 0)
    def _():
        wait_tile(i - 1)

    @pl.when(i == pl.num_programs(0) - 1)
    def _():
        wait_tile(i)


def _dispatch(h, posg, nchunks, dest, tail_start, tail_n, *, nb):
    T, D = h.shape
    nt, _, tm = posg.shape
    ne = tail_n.shape[0]
    return pl.pallas_call(
        functools.partial(_dispatch_kernel, ne=ne),
        grid_spec=pltpu.PrefetchScalarGridSpec(
            num_scalar_prefetch=4, grid=(nt,),
            in_specs=[pl.BlockSpec((tm, D), lambda i, *_: (i, 0)),
                      pl.BlockSpec((1, 2 * TOP_K, tm), lambda i, *_: (i, 0, 0))],
            out_specs=pl.BlockSpec(memory_space=pl.ANY),
            scratch_shapes=[pltpu.VMEM((2, _slots(tm, ne), D), BF16),
                            pltpu.VMEM((CHUNK, D), BF16),
                            pltpu.SemaphoreType.DMA((2,)), pltpu.SemaphoreType.DMA(())]),
        out_shape=jax.ShapeDtypeStruct((nb * MOE_BLOCK, D), BF16),
        compiler_params=_cparams(("arbitrary",)),
        name="dispatch",
    )(nchunks, dest, tail_start, tail_n, h, posg)


def _experts_kernel(be_ref, nu_ref, nxt_ref, edge_ref, xs_ref, wgu_hbm, bgu_ref, wd_hbm, bd_ref,
                    ys_ref, stage_gu, stage_d, wgu_s, wd_s, sem):
    b = pl.program_id(0)

    @pl.when(b < nu_ref[0])
    def _():
        edge = edge_ref[b]
        half = edge // 4
        nxt = nxt_ref[b]

        def weight_copies(e):
            return (pltpu.make_async_copy(wgu_hbm.at[e], stage_gu, sem.at[0]),
                    pltpu.make_async_copy(wd_hbm.at[e], stage_d, sem.at[1]))

        def cast_into(h):
            wgu_s[h] = stage_gu[...].astype(BF16)
            wd_s[h] = stage_d[...].astype(BF16)

        @pl.when(b == 0)
        def _():
            for cp in weight_copies(be_ref[b]):
                cp.start()
            for cp in weight_copies(be_ref[b]):
                cp.wait()
            cast_into(half)

        @pl.when((edge % 2 == 1) & (nxt >= 0))
        def _():
            for cp in weight_copies(nxt):
                cp.start()

        ff = wd_s.shape[1]
        gu = jnp.dot(xs_ref[...], wgu_s[half], preferred_element_type=F32) + bgu_ref[0]
        g = jnp.minimum(gu[:, :ff], SWIGLU_LIMIT)
        u = jnp.clip(gu[:, ff:], -SWIGLU_LIMIT, SWIGLU_LIMIT)
        act = (u + 1.0) * g * jax.nn.sigmoid(SWIGLU_ALPHA * g)
        ys = jnp.dot(act.astype(BF16), wd_s[half], preferred_element_type=F32) + bd_ref[0]
        ys_ref[...] = ys.astype(ys_ref.dtype)

        @pl.when(((edge // 2) % 2 == 1) & (nxt >= 0))
        def _():
            for cp in weight_copies(nxt):
                cp.wait()
            cast_into(1 - half)


def _experts(xs, blk_e, nused, nxt, edge, w_gate_up, b_gate_up, w_down, b_down):
    rows, D = xs.shape
    nb = rows // MOE_BLOCK
    ne, _, ff2 = w_gate_up.shape
    ff = w_down.shape[1]
    blk = lambda b, be, nu, *_: (jnp.minimum(b, nu[0] - 1), 0)
    exp3 = lambda b, be, *_: (be[b], 0, 0)
    return pl.pallas_call(
        _experts_kernel,
        grid_spec=pltpu.PrefetchScalarGridSpec(
            num_scalar_prefetch=4, grid=(nb,),
            in_specs=[pl.BlockSpec((MOE_BLOCK, D), blk),
                      pl.BlockSpec(memory_space=pl.ANY), pl.BlockSpec((1, 1, ff2), exp3),
                      pl.BlockSpec(memory_space=pl.ANY), pl.BlockSpec((1, 1, D), exp3)],
            out_specs=pl.BlockSpec((MOE_BLOCK, D), blk),
            scratch_shapes=[pltpu.VMEM((D, ff2), F32), pltpu.VMEM((ff, D), F32),
                            pltpu.VMEM((2, D, ff2), BF16), pltpu.VMEM((2, ff, D), BF16),
                            pltpu.SemaphoreType.DMA((2,))]),
        out_shape=jax.ShapeDtypeStruct((rows, D), BF16),
        compiler_params=_cparams(("arbitrary",)),
        name="experts",
    )(blk_e, nused, nxt, edge, xs, w_gate_up, b_gate_up[:, None, :], w_down,
      b_down[:, None, :])


def _combine_kernel(n_ref, dest_ref, ys_hbm, posg_ref, x1_ref, y_ref, ylast_ref, buf, sem):
    i = pl.program_id(0)
    last = pl.num_programs(0) - 1
    tm = x1_ref.shape[0]
    nslots = buf.shape[1]

    def run_copy(tile, c):
        s = tile % 2
        return pltpu.make_async_copy(ys_hbm.at[_chunk_rows(dest_ref[tile, c])],
                                     buf.at[s, _chunk_rows(c)], sem.at[s])

    def fetch(tile):
        _for_each_chunk(n_ref[tile], lambda c: run_copy(tile, c).start())
        s = tile % 2

        def zero_chunk(c, carry):
            buf[s, _chunk_rows(c), :] = jnp.zeros((CHUNK, buf.shape[2]), BF16)
            return carry
        lax.fori_loop(n_ref[tile], nslots // CHUNK, zero_chunk, 0)

    @pl.when(i == 0)
    def _():
        fetch(i)

    @pl.when(i < last)
    def _():
        fetch(i + 1)

    r0 = lax.broadcasted_iota(I32, (tm, tm), 0)
    r1 = lax.broadcasted_iota(I32, (tm, tm), 1)
    posg_t = _dot_nt(jnp.where(r0 == r1, 1.0, 0.0), posg_ref[0], exact=True)
    slot = lax.broadcasted_iota(I32, (tm, nslots), 1)
    w = jnp.zeros(slot.shape, F32)
    for k in range(TOP_K):
        w = jnp.where(slot == posg_t[:, k:k + 1].astype(I32),
                      posg_t[:, TOP_K + k:TOP_K + k + 1], w)

    _wait_chunks(n_ref[i], lambda r: pltpu.make_async_copy(
        ys_hbm.at[0:r], buf.at[i % 2, 0:r], sem.at[i % 2]))
    y = x1_ref[...] + jnp.dot(w.astype(BF16), buf[i % 2], preferred_element_type=F32)

    @pl.when(i < last)
    def _():
        y_ref[...] = y

    @pl.when(i == last)
    def _():
        ylast_ref[...] = y


def _combine(ys, posg, x1, nchunks, dest):
    T, D = x1.shape
    nt, _, tm = posg.shape
    return pl.pallas_call(
        _combine_kernel,
        grid_spec=pltpu.PrefetchScalarGridSpec(
            num_scalar_prefetch=2, grid=(nt,),
            in_specs=[pl.BlockSpec(memory_space=pl.ANY),
                      pl.BlockSpec((1, 2 * TOP_K, tm), lambda i, *_: (i, 0, 0)),
                      pl.BlockSpec((tm, D), lambda i, *_: (i, 0))],
            out_specs=[pl.BlockSpec((tm, D), lambda i, *_: (jnp.minimum(i, nt - 2), 0)),
                       pl.BlockSpec((tm, D), lambda i, *_: (0, 0))],
            scratch_shapes=[pltpu.VMEM((2, dest.shape[1] * CHUNK, D), BF16),
                            pltpu.SemaphoreType.DMA((2,))]),
        out_shape=[jax.ShapeDtypeStruct((T - tm, D), F32), jax.ShapeDtypeStruct((tm, D), F32)],
        compiler_params=_cparams(("arbitrary",)),
        name="combine",
    )(nchunks, dest, ys, posg, x1)


def _moe(h, posg, cnt, x1, w_gate_up, b_gate_up, w_down, b_down):
    nt, _, tm = posg.shape
    ne = cnt.shape[1]
    max_chunks = (nt * tm * TOP_K) // CHUNK + nt * ne
    nb = -(-max_chunks // CHUNKS_PER_BLOCK) + ne
    nchunks, dest, tail_start, tail_n, nused, blk_e, nxt, edge = _moe_tables(
        cnt, nb, _slots(tm, ne) // CHUNK)
    xs = _dispatch(h, posg, nchunks, dest, tail_start, tail_n, nb=nb)
    ys = _experts(xs, blk_e, nused, nxt, edge, w_gate_up, b_gate_up, w_down, b_down)
    return _combine(ys, posg, x1, nchunks, dest)


def kernel(x_prompt, x_sample, cache_k, cache_v, state_conv, page_table, norm1_w, w_in,
           q_norm_w, k_norm_w, lambda_q1, lambda_k1, lambda_q2, lambda_k2, subln_w,
           conv_w, conv_b, conv_ln_g, conv_ln_b, w_out, norm2_w, router_w, router_b,
           w_gate_up, b_gate_up, w_down, b_down):
    B, S, D = x_prompt.shape
    Bs, Ss, _ = x_sample.shape
    depth = norm1_w.shape[0]
    n_phys, page, heads, _, qk = cache_k.shape[1:]
    vdim = cache_v.shape[-1]
    qc, vc, cc = heads * 2 * qk, heads * vdim, conv_w.shape[2]
    taps = conv_w.shape[1]
    assert Ss == 1 and qk == QK_GROUP and vdim == V7X_LANES and (B * S) % MOE_TILE == 0
    assert Bs <= MOE_TILE and S >= taps - 1
    n_past = page_table.shape[1] * page
    T = B * S
    nt_p = T // MOE_TILE
    pos_p = jnp.arange(S, dtype=F32)
    pos_s = jnp.full((Bs,), n_past, F32)
    xp = x_prompt.reshape(T, D)
    xs = x_sample.reshape(Bs, D)
    pad_tile = lambda a: jnp.pad(a, ((0, MOE_TILE - Bs), (0, 0)))
    outs = [[] for _ in range(6)]
    for l in range(depth):
        lam_init = 0.8 - 0.6 * math.exp(-0.3 * l)
        lamv = jnp.stack([lambda_q1[l], lambda_k1[l], lambda_q2[l], lambda_k2[l]])
        conv_p = (conv_w[l], conv_b[l], conv_ln_g[l], conv_ln_b[l])
        proj_p = (norm1_w[l], w_in[l], q_norm_w[l], k_norm_w[l])
        tail_p = (w_out[l], norm2_w[l], router_w[l], router_b[l])

        q, kt, v, u, kbt, vb = _proj(xp, pos_p, S // PROJ_TILE, PROJ_TILE, *proj_p,
                                     qc=qc, vc=vc, cc=cc, exact_norm=False, attn_layout=True)
        qs, ks_, vs_, us = _proj(xs, pos_s, 1, Bs, *proj_p, qc=qc, vc=vc, cc=cc,
                                 exact_norm=True, attn_layout=False)
        cache_kt = jnp.transpose(cache_k[l], (0, 2, 3, 4, 1)).reshape(n_phys, qc, page)
        cache_vr = cache_v[l].reshape(n_phys, page * heads, vdim)
        attn, attn_s = _attention(q, kbt, vb, qs, ks_, vs_, cache_kt, cache_vr, page_table, lamv,
                                  subln_w[l], lam_init=lam_init)

        bufs = _tail(attn, u, xp, *tail_p, tm=MOE_TILE, n_valid=MOE_TILE,
                     total_tiles=nt_p + 1, conv_params=conv_p, seq=S)
        outs[0].append(jnp.transpose(kt.reshape(B, heads, 2, qk, S), (0, 4, 1, 2, 3)))
        outs[1].append(v.reshape(B, S, heads, vdim))
        outs[2].append(u.reshape(B, S, cc)[:, S - (taps - 1):])

        conv_s = _conv_step(state_conv[l], us, *conv_p)
        x1, h, posg, cnt = _tail(pad_tile(attn_s), pad_tile(conv_s), pad_tile(xs), *tail_p,
                                 tm=MOE_TILE, n_valid=Bs, total_tiles=nt_p + 1,
                                 first_tile=nt_p, into=bufs)
        outs[3].append(ks_.reshape(Bs, Ss, heads, 2, qk))
        outs[4].append(vs_.reshape(Bs, Ss, heads, vdim))
        outs[5].append(jnp.concatenate([state_conv[l][:, Ss:], us[:, None, :]], axis=1))

        xp, y_last = _moe(h, posg, cnt[:, :, 0], x1, w_gate_up[l], b_gate_up[l], w_down[l],
                          b_down[l])
        xs = y_last[:Bs]
    return (xp.reshape(B, S, D), xs.reshape(Bs, Ss, D)) + tuple(jnp.stack(o) for o in outs)
```

```python
import functools
import math

import jax
import jax.numpy as jnp
from jax import lax
from jax.experimental import pallas as pl
from jax.experimental.pallas import tpu as pltpu

F32 = jnp.float32
BF16 = jnp.bfloat16
I32 = jnp.int32
HIGHEST = lax.Precision.HIGHEST

EPS = 1e-6
ROPE_THETA = 10000.0
SWIGLU_LIMIT = 7.0
SWIGLU_ALPHA = 1.702
TOP_K = 4
NEG = -1e30
QK_GROUP = 64

V7X_LANES = 128
V7X_SUBLANES = 8
VMEM_LIMIT = 56 * 1024 * 1024
BF16_ROWS = 16

MOE_TILE = 256
CHUNK = BF16_ROWS
MOE_BLOCK = 512
CHUNKS_PER_BLOCK = MOE_BLOCK // CHUNK
PROJ_TILE = 512


def _cparams(sem, vmem=VMEM_LIMIT):
    return pltpu.CompilerParams(dimension_semantics=sem, vmem_limit_bytes=vmem)


def _dot(a, b, exact=False):
    if exact:
        return jnp.dot(a.astype(F32), b.astype(F32), precision=HIGHEST,
                       preferred_element_type=F32)
    return jnp.dot(a.astype(BF16), b.astype(BF16), preferred_element_type=F32)


def _dot_nt(a, b, exact=False):
    dn = (((1,), (1,)), ((), ()))
    if exact:
        return lax.dot_general(a.astype(F32), b.astype(F32), dn, precision=HIGHEST,
                               preferred_element_type=F32)
    return lax.dot_general(a.astype(BF16), b.astype(BF16), dn, preferred_element_type=F32)


def _bf16_round(x):
    return x.astype(BF16).astype(F32)


def _rope_norm(p, gsum, w, cos, sin, first_half, exact_norm):
    ss = _dot(p * p, gsum, exact_norm)
    n = p * lax.rsqrt(ss * (1.0 / QK_GROUP) + EPS) * w
    outs = []
    for j in range(p.shape[1] // V7X_LANES):
        nj = n[:, j * V7X_LANES:(j + 1) * V7X_LANES]
        rot = jnp.where(first_half, pltpu.roll(nj, V7X_LANES - QK_GROUP // 2, 1),
                        pltpu.roll(nj, QK_GROUP // 2, 1))
        outs.append(nj * cos + rot * sin)
    return jnp.concatenate(outs, axis=1)


def _proj_kernel(x_ref, n1_ref, w_ref, qw_ref, kw_ref, cos_ref, sin_ref, gsum_ref,
                 q_ref, k_ref, v_ref, u_ref, *rest, qc, vc, cc, scale, exact_norm):
    x = x_ref[...]
    h = x * lax.rsqrt(jnp.mean(x * x, axis=-1, keepdims=True) + EPS) * n1_ref[...]
    hm = h.astype(BF16)
    cos = cos_ref[...]
    sin = sin_ref[...]
    lane = lax.broadcasted_iota(I32, cos.shape, 1)
    first_half = (lane % QK_GROUP) < QK_GROUP // 2
    gsum = gsum_ref[...]

    q = _rope_norm(_dot(hm, w_ref[:, 0:qc]), gsum, qw_ref[...], cos, sin, first_half, exact_norm)
    q_ref[...] = (q * scale).astype(q_ref.dtype)
    k = _rope_norm(_dot(hm, w_ref[:, qc:2 * qc]), gsum, kw_ref[...], cos, sin, first_half,
                   exact_norm)
    v = _dot(hm, w_ref[:, 2 * qc:2 * qc + vc])
    o = 2 * qc + vc
    ua = _dot(hm, w_ref[:, o:o + cc])
    ub = _dot(hm, w_ref[:, o + cc:o + 2 * cc])
    u_ref[...] = ua * jax.nn.sigmoid(ub)
    if rest:
        kb_ref, vb_ref = rest
        kt = k.T
        k_ref[0] = kt
        kb_ref[0, :, 0] = kt.astype(BF16).reshape(kb_ref.shape[1], kb_ref.shape[3], kt.shape[1])
        vb_ref[...] = v.astype(BF16)
        nh = vc // V7X_LANES
        for h in range(nh):
            v_ref[pl.ds(h, v.shape[0], stride=nh), :] = v[:, h * V7X_LANES:(h + 1) * V7X_LANES]
    else:
        k_ref[...] = k
        v_ref[...] = v


def _rope_tables(pos):
    half = QK_GROUP // 2
    inv = jnp.power(ROPE_THETA, -jnp.arange(half, dtype=F32) / half)
    ang = pos[:, None] * inv[None, :]
    reps = V7X_LANES // QK_GROUP
    cos = jnp.tile(jnp.cos(ang), (1, 2 * reps))
    s = jnp.sin(ang)
    sin = jnp.tile(jnp.concatenate([-s, s], axis=1), (1, reps))
    return cos, sin


def _proj(x2d, pos_rows, n_pos_blocks, tm, norm1_w, w_in, q_norm_w, k_norm_w, *, qc, vc, cc,
          exact_norm, attn_layout):
    T, D = x2d.shape
    cos, sin = _rope_tables(pos_rows)
    gi = jnp.arange(qc) // QK_GROUP
    gsum = (gi[:, None] == gi[None, :]).astype(F32 if exact_norm else BF16)
    qw = jnp.tile(q_norm_w, qc // QK_GROUP)[None, :]
    kw = jnp.tile(k_norm_w, qc // QK_GROUP)[None, :]
    w = w_in.astype(BF16)
    row = lambda i: (i, 0)
    full = lambda i: (0, 0)
    out_shape = [jax.ShapeDtypeStruct((T, qc), BF16),
                 jax.ShapeDtypeStruct((T, qc), F32),
                 jax.ShapeDtypeStruct((T, vc), F32),
                 jax.ShapeDtypeStruct((T, cc), F32)]
    out_specs = [pl.BlockSpec((tm, qc), row), pl.BlockSpec((tm, qc), row),
                 pl.BlockSpec((tm, vc), row), pl.BlockSpec((tm, cc), row)]
    if attn_layout:
        nseq = T // (n_pos_blocks * tm)
        heads = qc // (2 * QK_GROUP)
        seq_tile = lambda i: (i // n_pos_blocks, 0, i % n_pos_blocks)
        out_shape[1] = jax.ShapeDtypeStruct((nseq, qc, n_pos_blocks * tm), F32)
        out_specs[1] = pl.BlockSpec((1, qc, tm), seq_tile)
        out_shape[2] = jax.ShapeDtypeStruct((T * vc // V7X_LANES, V7X_LANES), F32)
        out_specs[2] = pl.BlockSpec((tm * vc // V7X_LANES, V7X_LANES), row)
        out_shape += [jax.ShapeDtypeStruct((nseq, heads, n_pos_blocks, 2 * QK_GROUP, tm), BF16),
                      jax.ShapeDtypeStruct((T, vc), BF16)]
        out_specs += [pl.BlockSpec((1, heads, 1, 2 * QK_GROUP, tm),
                                   lambda i: (i // n_pos_blocks, 0, i % n_pos_blocks, 0, 0)),
                      pl.BlockSpec((tm, vc), row)]
    return pl.pallas_call(
        functools.partial(_proj_kernel, qc=qc, vc=vc, cc=cc, scale=QK_GROUP ** -0.5,
                          exact_norm=exact_norm),
        grid=(T // tm,),
        in_specs=[pl.BlockSpec((tm, D), row),
                  pl.BlockSpec((1, D), full),
                  pl.BlockSpec(w.shape, full),
                  pl.BlockSpec((1, qc), full),
                  pl.BlockSpec((1, qc), full),
                  pl.BlockSpec((tm, V7X_LANES), lambda i: (i % n_pos_blocks, 0)),
                  pl.BlockSpec((tm, V7X_LANES), lambda i: (i % n_pos_blocks, 0)),
                  pl.BlockSpec((qc, qc), full)],
        out_specs=out_specs,
        out_shape=out_shape,
        compiler_params=_cparams(("arbitrary",)),
        name="proj" if attn_layout else "proj_step",
    )(x2d, norm1_w[None, :], w, qw, kw, cos, sin, gsum)


def _lambda_value(lv, lam_init):
    a = jnp.sum(lv[0:1] * lv[1:2], axis=-1, keepdims=True)
    b = jnp.sum(lv[2:3] * lv[3:4], axis=-1, keepdims=True)
    return jnp.exp(a) - jnp.exp(b) + lam_init


def _subln(o, w, lam_init):
    y = o * lax.rsqrt(jnp.mean(o * o, axis=-1, keepdims=True) + EPS)
    return y * w * (1.0 - lam_init)


def _attn_body(i, lam_ref, sw_ref, q_ref, k_ref, v_ref, o_ref, *, tq, lam_init):
    lam = _lambda_value(lam_ref[...], lam_init)
    q = q_ref[...]
    lane = lax.broadcasted_iota(I32, q.shape, 1)
    zero = jnp.zeros_like(q)
    qs = (jnp.where(lane < QK_GROUP, q, zero), jnp.where(lane >= QK_GROUP, q, zero))

    def chunk(j, carry, masked):
        kc = k_ref[0, 0, j]
        vc = v_ref[pl.ds(pl.multiple_of(j * tq, tq), tq), :]
        out = []
        for c in range(2):
            m, l, acc = carry[c]
            s = jnp.dot(qs[c], kc, preferred_element_type=F32)
            if masked:
                row = lax.broadcasted_iota(I32, s.shape, 0)
                col = lax.broadcasted_iota(I32, s.shape, 1)
                s = jnp.where(col <= row, s, NEG)
            m_new = jnp.maximum(m, jnp.max(s, axis=-1, keepdims=True))
            p = jnp.exp(s - m_new)
            alpha = jnp.exp(m - m_new)
            l = alpha * l + jnp.sum(p, axis=-1, keepdims=True)
            acc = alpha * acc + _dot(p, vc)
            out.append((m_new, l, acc))
        return tuple(out)

    init = tuple((jnp.full((tq, 1), NEG, F32), jnp.zeros((tq, 1), F32),
                  jnp.zeros((tq, V7X_LANES), F32)) for _ in range(2))
    carry = lax.fori_loop(0, i, lambda j, c: chunk(j, c, False), init)
    (_, l0, a0), (_, l1, a1) = chunk(i, carry, True)
    o = a0 / l0 - lam * (a1 / l1)
    o_ref[...] = _subln(o, sw_ref[...], lam_init).astype(o_ref.dtype)


DECODE_PAGES_PER_STEP = 32
SOFTMAX_PAGES = 16
PAGE_RING_DEPTH = 3


def _decode_body(s, half_steps, lam_ref, sw_ref, qm_ref, kn_ref, vn_ref, pages_ref,
                 o_ref, s_ref, m_ref, coef_ref, acc_ref, *, page, heads, lam_init):
    pps = pages_ref.shape[0]
    rows = s_ref.shape[1]
    qm = qm_ref[0]
    row = lax.broadcasted_iota(I32, (rows, V7X_LANES), 0)
    n_pages = half_steps * pps

    @pl.when(s == 0)
    def _():
        m_ref[...] = jnp.full(m_ref.shape, NEG, F32)

    @pl.when(s < half_steps)
    def _():
        m = m_ref[...]
        for j in range(pps):
            sc = jnp.dot(qm, pages_ref[j].astype(BF16), preferred_element_type=F32)
            s_ref[s * pps + j] = sc
            m = jnp.maximum(m, sc)
        m_ref[...] = m

    def head_weights(p):
        a = p * coef_ref[...]
        return _bf16_round(a + pltpu.roll(a, rows - heads, 0))

    @pl.when(s == half_steps)
    def _():
        s_new = jnp.sum(qm.astype(F32) * _bf16_round(kn_ref[0]), axis=-1, keepdims=True)
        m = jnp.maximum(jnp.max(m_ref[...], axis=-1, keepdims=True), s_new)

        def exp_pages(t, l):
            pages = pl.ds(pl.multiple_of(t * SOFTMAX_PAGES, SOFTMAX_PAGES), SOFTMAX_PAGES)
            p = jnp.exp(s_ref[pages] - m[None])
            s_ref[pages] = p
            return l + jnp.sum(p, axis=0)

        lsum = lax.fori_loop(0, n_pages // SOFTMAX_PAGES, exp_pages,
                             jnp.zeros((rows, V7X_LANES), F32))
        p_new = jnp.exp(s_new - m)
        l = jnp.sum(lsum, axis=-1, keepdims=True) + p_new
        lam = _lambda_value(lam_ref[...], lam_init)
        coef = jnp.where(row[:, 0:1] < heads, 1.0, -lam) / l
        coef_ref[...] = jnp.broadcast_to(coef, coef_ref.shape)
        acc_ref[...] = (head_weights(jnp.broadcast_to(p_new, (rows, V7X_LANES)))
                        * _bf16_round(vn_ref[0]))

    @pl.when(s >= half_steps)
    def _():
        lane = lax.broadcasted_iota(I32, (rows, V7X_LANES), 1)
        keep = (lane % heads == row) & (row < heads)
        acc = acc_ref[...]
        for j in range(pps):
            a = head_weights(s_ref[(s - half_steps) * pps + j])
            parts = []
            for c in range(heads):
                idx = (c * page + lane) // heads
                parts.append(jnp.where(keep, jnp.take_along_axis(a, idx, axis=1), 0.0))
            a_exp = jnp.concatenate(parts, axis=1).astype(BF16)
            acc = acc + jnp.dot(a_exp, pages_ref[j].astype(BF16), preferred_element_type=F32)
        acc_ref[...] = acc

    @pl.when(s == 2 * half_steps - 1)
    def _():
        o_ref[0] = _subln(acc_ref[...], sw_ref[...], lam_init)


def _attention_kernel(pt_ref, lam_ref, sw_ref, q_ref, k_ref, v_ref, qm_ref, kn_ref, vn_ref,
                      kt_hbm, vr_hbm, o_ref, od_ref, s_ref, m_ref, coef_ref, acc_ref, pages, sem,
                      *, ratio, nq, steps_per_seq, tq, page, heads, lam_init):
    t = pl.program_id(0)
    depth, pps = pages.shape[:2]
    half_steps = steps_per_seq // 2

    def fetch(step):
        seq = step // steps_per_seq
        s = step % steps_per_seq
        first = (s % half_steps) * pps
        slot = step % depth
        for src_hbm, cond in ((kt_hbm, s < half_steps), (vr_hbm, s >= half_steps)):
            @pl.when(cond)
            def _():
                for j in range(pps):
                    pltpu.make_async_copy(src_hbm.at[pt_ref[seq, first + j]],
                                          pages.at[slot, j], sem.at[slot]).start()

    @pl.when(t == 0)
    def _():
        for ahead in range(depth - 1):
            fetch(t + ahead)

    @pl.when(t + depth - 1 < pl.num_programs(0))
    def _():
        fetch(t + depth - 1)

    slot = t % depth
    pltpu.make_async_copy(kt_hbm.at[0:pps], pages.at[slot], sem.at[slot]).wait()
    _decode_body(t % steps_per_seq, half_steps, lam_ref, sw_ref, qm_ref, kn_ref, vn_ref,
                 pages.at[slot], od_ref, s_ref, m_ref, coef_ref, acc_ref,
                 page=page, heads=heads, lam_init=lam_init)

    @pl.when(t % ratio == 0)
    def _():
        _attn_body((t // ratio) % nq, lam_ref, sw_ref, q_ref, k_ref, v_ref, o_ref,
                   tq=tq, lam_init=lam_init)


def _attention(q, kbt, vb, q_step, k_new, v_new, cache_kt, cache_vr, page_table, lamv, subln_w,
               *, lam_init):
    batch, heads, nq, _, tq = kbt.shape
    seq = nq * tq
    Bs, D = q_step.shape
    n_pages = page_table.shape[1]
    page = cache_kt.shape[2]
    vdim = cache_vr.shape[2]
    pps = DECODE_PAGES_PER_STEP
    half_steps = n_pages // pps
    steps_per_seq = 2 * half_steps
    rows = 2 * heads
    n_dec, n_att = Bs * steps_per_seq, batch * heads * nq
    assert rows == V7X_SUBLANES and page == V7X_LANES and vdim == V7X_LANES
    assert n_pages % pps == 0 and n_pages % SOFTMAX_PAGES == 0 and n_dec % n_att == 0
    assert cache_kt.shape[1:] == cache_vr.shape[1:]
    ratio = n_dec // n_att
    group = jnp.arange(D) // QK_GROUP
    rowmask = ((group % 2) * heads + group // 2)[None, :] == jnp.arange(rows)[:, None]
    qm = jnp.where(rowmask[None], q_step[:, None, :], jnp.zeros((), BF16))
    vn = jnp.pad(v_new.reshape(Bs, heads, vdim), ((0, 0), (0, rows - heads), (0, 0)))

    def att(t):
        a = t // ratio
        return a // (heads * nq), (a // nq) % heads, a % nq

    def q_map(t, pt):
        b, h, i = att(t)
        return b * nq + i, h

    seq3 = lambda t, pt: (t // steps_per_seq, 0, 0)
    const = lambda t, pt: (0, 0)
    out, out_step = pl.pallas_call(
        functools.partial(_attention_kernel, ratio=ratio, nq=nq,
                          steps_per_seq=steps_per_seq, tq=tq, page=page, heads=heads,
                          lam_init=lam_init),
        grid_spec=pltpu.PrefetchScalarGridSpec(
            num_scalar_prefetch=1, grid=(n_dec,),
            in_specs=[pl.BlockSpec(lamv.shape, const), pl.BlockSpec((1, vdim), const),
                      pl.BlockSpec((tq, V7X_LANES), q_map),
                      pl.BlockSpec((1, 1, nq, V7X_LANES, tq),
                                   lambda t, pt: att(t)[:2] + (0, 0, 0)),
                      pl.BlockSpec((seq, V7X_LANES), lambda t, pt: att(t)[:2]),
                      pl.BlockSpec((1, rows, D), seq3), pl.BlockSpec((1, 1, D), seq3),
                      pl.BlockSpec((1, rows, vdim), seq3),
                      pl.BlockSpec(memory_space=pl.ANY), pl.BlockSpec(memory_space=pl.ANY)],
            out_specs=[pl.BlockSpec((tq, V7X_LANES), q_map),
                       pl.BlockSpec((1, rows, vdim), seq3)],
            scratch_shapes=[pltpu.VMEM((n_pages, rows, page), F32),
                            pltpu.VMEM((rows, V7X_LANES), F32),
                            pltpu.VMEM((rows, V7X_LANES), F32),
                            pltpu.VMEM((rows, vdim), F32),
                            pltpu.VMEM((PAGE_RING_DEPTH, pps) + cache_kt.shape[1:], F32),
                            pltpu.SemaphoreType.DMA((PAGE_RING_DEPTH,))]),
        out_shape=[jax.ShapeDtypeStruct((batch * seq, heads * V7X_LANES), BF16),
                   jax.ShapeDtypeStruct((Bs, rows, vdim), F32)],
        compiler_params=_cparams(("arbitrary",)),
        name="attention",
    )(page_table, lamv, subln_w[None, :], q, kbt, vb, qm, k_new[:, None, :], vn,
      cache_kt, cache_vr)
    return out, out_step[:, :heads, :].reshape(Bs, heads * vdim)


CONV_HALO = 32


def _ln_swish(y, b_ref, g_ref, be_ref):
    y = y + b_ref[...]
    mu = jnp.mean(y, axis=-1, keepdims=True)
    yc = y - mu
    z = yc * lax.rsqrt(jnp.mean(yc * yc, axis=-1, keepdims=True) + EPS) * g_ref[...] + be_ref[...]
    return z * jax.nn.sigmoid(z)


def _conv_tile(u_ref, starts_sequence, w_ref, b_ref, g_ref, be_ref, buf_ref, part_ref):
    sub = V7X_SUBLANES
    tc = u_ref.shape[0]
    taps = w_ref.shape[0]
    first = CONV_HALO - (taps - 1)

    @pl.when(starts_sequence)
    def _():
        buf_ref[0:CONV_HALO, :] = jnp.zeros((CONV_HALO, buf_ref.shape[1]), F32)
        buf_ref[CONV_HALO + tc:, :] = jnp.zeros((sub, buf_ref.shape[1]), F32)

    buf_ref[CONV_HALO:CONV_HALO + tc, :] = _bf16_round(u_ref[...])
    acc = None
    for r in range(sub):
        part = None
        for a in range(-(-(first + taps) // sub)):
            k = sub * a + r - first
            if 0 <= k < taps:
                term = buf_ref[sub * a:sub * a + tc + sub, :] * _bf16_round(w_ref[k:k + 1, :])
                part = term if part is None else part + term
        if r == 0:
            acc = part[0:tc]
        else:
            part_ref[...] = part
            acc = acc + part_ref[r:r + tc, :]
    out = _ln_swish(acc, b_ref, g_ref, be_ref)
    buf_ref[0:CONV_HALO, :] = buf_ref[tc:tc + CONV_HALO, :]
    return out


def _conv_step_kernel(st_ref, u_ref, w_ref, b_ref, g_ref, be_ref, o_ref, *, taps):
    acc = u_ref[...] * w_ref[taps - 1:taps, :]
    for k in range(taps - 1):
        acc = acc + st_ref[:, k, :] * w_ref[k:k + 1, :]
    o_ref[...] = _ln_swish(acc, b_ref, g_ref, be_ref)


def _conv_step(state, u, conv_w, conv_b, ln_g, ln_b):
    taps, C = conv_w.shape
    return pl.pallas_call(
        functools.partial(_conv_step_kernel, taps=taps),
        out_shape=jax.ShapeDtypeStruct(u.shape, F32),
        compiler_params=_cparams(None),
        name="conv_step",
    )(state, u, conv_w, conv_b[None, :], ln_g[None, :], ln_b[None, :])


def _tail_kernel(a_ref, c_ref, x_ref, wo_ref, n2_ref, rw_ref, rb_ref, *rest, n_valid,
                 tiles_per_seq):
    tm = x_ref.shape[0]
    half = a_ref.shape[1]
    if tiles_per_seq:
        cw_ref, cb_ref, cg_ref, cbe_ref = rest[:4]
        buf_ref, part_ref = rest[-2:]
        x1_ref, h_ref, posg_ref, cnt_ref = rest[-6:-2]
        conv = _conv_tile(c_ref, pl.program_id(0) % tiles_per_seq == 0, cw_ref, cb_ref, cg_ref,
                          cbe_ref, buf_ref, part_ref)
    else:
        x1_ref, h_ref, posg_ref, cnt_ref = rest[-4:]
        conv = c_ref[...]
    x1 = (x_ref[...] + _dot(a_ref[...], wo_ref[0:half, :]) + _dot(conv, wo_ref[half:, :]))
    x1_ref[...] = x1
    h = (x1 * lax.rsqrt(jnp.mean(x1 * x1, axis=-1, keepdims=True) + EPS)
         * n2_ref[...]).astype(BF16)
    h_ref[...] = h
    logits = _dot_nt(rw_ref[...], h) + rb_ref[...]
    ne = logits.shape[0]
    eidx = lax.broadcasted_iota(I32, logits.shape, 0)
    valid = lax.broadcasted_iota(I32, (1, tm), 1) < n_valid

    sels, vals = [], []
    l = logits
    for _ in range(TOP_K):
        m = jnp.max(l, axis=0, keepdims=True)
        first = jnp.min(jnp.where(l == m, eidx, ne), axis=0, keepdims=True)
        sel = (eidx == first) & valid
        l = jnp.where(eidx == first, -jnp.inf, l)
        sels.append(sel)
        vals.append(m)
    ex = [jnp.exp(v - vals[0]) for v in vals]
    den = ex[0] + ex[1] + ex[2] + ex[3]
    gates = [jnp.where(valid, e / den, 0.0) for e in ex]

    msel = jnp.zeros(logits.shape, F32)
    for sel in sels:
        msel = msel + jnp.where(sel, 1.0, 0.0)
    r0 = lax.broadcasted_iota(I32, (tm, tm), 0)
    r1 = lax.broadcasted_iota(I32, (tm, tm), 1)
    upper = jnp.where(r0 < r1, 1.0, 0.0).astype(BF16)
    rank = jnp.dot(msel.astype(BF16), upper, preferred_element_type=F32)
    cnt = jnp.sum(msel, axis=1, keepdims=True)
    pcnt = jnp.ceil(cnt * (1.0 / CHUNK)) * CHUNK
    e0 = lax.broadcasted_iota(I32, (ne, ne), 0)
    e1 = lax.broadcasted_iota(I32, (ne, ne), 1)
    lower = jnp.where(e1 < e0, 1.0, 0.0)
    off = jnp.dot(lower.astype(BF16), jnp.broadcast_to(pcnt, (ne, V7X_LANES)).astype(BF16),
                  preferred_element_type=F32)[:, 0:1]
    pos = off + rank
    rows = [jnp.where(valid, jnp.sum(jnp.where(sel, pos, 0.0), axis=0, keepdims=True), -1.0)
            for sel in sels]
    posg_ref[0] = jnp.concatenate(rows + gates, axis=0)
    cnt_ref[0] = jnp.broadcast_to(cnt, (ne, V7X_LANES)).astype(I32)


def _tail(attn, conv, x2d, w_out, norm2_w, router_w, router_b, *, tm, n_valid, total_tiles,
          first_tile=0, into=None, conv_params=None, seq=None):
    T, D = x2d.shape
    half = attn.shape[1]
    ne = router_w.shape[1]
    nt = T // tm
    wo = w_out.astype(BF16)
    rw = router_w.T.astype(BF16)
    row = lambda i: (i, 0)
    full = lambda i: (0, 0)
    orow = lambda i: (first_tile + i, 0)
    otile = lambda i: (first_tile + i, 0, 0)
    extra = list(into) if into is not None else []
    conv_in, conv_specs, scratch, tiles_per_seq = [], [], [], 0
    if conv_params is not None:
        conv_w, conv_b, ln_g, ln_b = conv_params
        taps, C = conv_w.shape
        conv_in = [conv_w, conv_b[None, :], ln_g[None, :], ln_b[None, :]]
        conv_specs = [pl.BlockSpec((taps, C), full)] + [pl.BlockSpec((1, C), full)] * 3
        scratch = [pltpu.VMEM((tm + CONV_HALO + V7X_SUBLANES, C), F32),
                   pltpu.VMEM((tm + V7X_SUBLANES, C), F32)]
        tiles_per_seq = seq // tm
    return pl.pallas_call(
        functools.partial(_tail_kernel, n_valid=n_valid, tiles_per_seq=tiles_per_seq),
        grid=(nt,),
        in_specs=[pl.BlockSpec((tm, half), row), pl.BlockSpec((tm, half), row),
                  pl.BlockSpec((tm, D), row), pl.BlockSpec((D, D), full),
                  pl.BlockSpec((1, D), full), pl.BlockSpec((ne, D), full),
                  pl.BlockSpec((ne, 1), full)] + conv_specs
                 + [pl.BlockSpec(memory_space=pl.ANY)] * len(extra),
        out_specs=[pl.BlockSpec((tm, D), orow), pl.BlockSpec((tm, D), orow),
                   pl.BlockSpec((1, 2 * TOP_K, tm), otile),
                   pl.BlockSpec((1, ne, V7X_LANES), otile)],
        out_shape=[jax.ShapeDtypeStruct((total_tiles * tm, D), F32),
                   jax.ShapeDtypeStruct((total_tiles * tm, D), BF16),
                   jax.ShapeDtypeStruct((total_tiles, 2 * TOP_K, tm), F32),
                   jax.ShapeDtypeStruct((total_tiles, ne, V7X_LANES), I32)],
        scratch_shapes=scratch,
        input_output_aliases={7 + len(conv_in) + j: j for j in range(len(extra))},
        compiler_params=_cparams(("arbitrary",)),
        name="tail",
    )(attn, conv, x2d, wo, norm2_w[None, :], rw, router_b[:, None], *conv_in, *extra)


def _slots(tm, ne):
    worst = TOP_K * tm + ne * (CHUNK - 1)
    return -(-worst // V7X_LANES) * V7X_LANES


def _prefix_sum(x, axis, exclusive):
    n = x.shape[axis]
    i = jnp.arange(n)
    tri = (i[:, None] < i[None, :]) if exclusive else (i[:, None] <= i[None, :])
    xm = jnp.moveaxis(x, axis, -1)
    out = jnp.sum(xm[..., :, None] * tri.astype(x.dtype), axis=-2)
    return jnp.moveaxis(out, -1, axis)


def _moe_tables(cnt, nb, slot_chunks):
    nch = (cnt + (CHUNK - 1)) // CHUNK
    tot = jnp.sum(nch, axis=0)
    nblk = (tot + (CHUNKS_PER_BLOCK - 1)) // CHUNKS_PER_BLOCK
    bend = _prefix_sum(nblk, 0, exclusive=False)
    gstart = (bend - nblk) * CHUNKS_PER_BLOCK
    rs = gstart[None, :] + _prefix_sum(nch, 0, exclusive=True)
    tail_start = gstart + tot
    tail_n = nblk * CHUNKS_PER_BLOCK - tot
    nused = bend[-1:]
    blk = jnp.minimum(jnp.arange(nb, dtype=I32), nused - 1)
    blk_e = jnp.sum((bend[None, :] <= blk[:, None]).astype(I32), axis=1)
    cend = _prefix_sum(nch, 1, exclusive=False)
    c = jnp.arange(slot_chunks, dtype=I32)
    run = jnp.minimum(jnp.sum((cend[:, None, :] <= c[None, :, None]).astype(I32), axis=2),
                      nch.shape[1] - 1)
    shift = rs - (cend - nch)
    dest = c[None, :] + jnp.sum(jnp.where(run[:, :, None] == jnp.arange(nch.shape[1]),
                                          shift[:, None, :], 0), axis=2)
    ne = nblk.shape[0]
    eid = jnp.arange(ne, dtype=I32)
    later = jnp.where((eid[None, :] > eid[:, None]) & (nblk[None, :] > 0), eid[None, :], ne)
    next_e = jnp.min(later, axis=1)
    next_e = jnp.where(next_e == ne, -1, next_e)
    rank_e = _prefix_sum((nblk > 0).astype(I32), 0, exclusive=True)
    bidx = jnp.arange(nb, dtype=I32)
    mine = (blk_e[:, None] == eid[None, :]).astype(I32)
    of_block = lambda per_expert: jnp.sum(mine * per_expert[None, :], axis=1)
    first = (bidx == of_block(bend - nblk)) & (bidx < nused)
    last = (bidx == of_block(bend) - 1) & (bidx < nused)
    edge = first.astype(I32) + 2 * last.astype(I32) + 4 * (of_block(rank_e) % 2)
    i32 = lambda a: a.astype(I32)
    return (i32(cend[:, -1]), i32(dest), i32(tail_start), i32(tail_n), i32(nused), i32(blk_e),
            i32(of_block(next_e)), i32(edge))


def _chunk_rows(c):
    return pl.ds(pl.multiple_of(c * CHUNK, CHUNK), CHUNK)


def _for_each_chunk(n, fn, unroll=1):
    def group(g, carry):
        for u in range(unroll):
            fn(g * unroll + u)
        return carry
    lax.fori_loop(0, n // unroll, group, 0)

    def single(c, carry):
        fn(c)
        return carry
    lax.fori_loop((n // unroll) * unroll, n, single, 0)


WAIT_GROUP = 8
DMA_ISSUE_UNROLL = 8


def _wait_chunks(n, copy_of_rows):
    _for_each_chunk(n // WAIT_GROUP, lambda c: copy_of_rows(WAIT_GROUP * CHUNK).wait())
    _for_each_chunk(n % WAIT_GROUP, lambda c: copy_of_rows(CHUNK).wait())


def _one_hot_rows(pos, nrows):
    r = lax.broadcasted_iota(I32, (nrows, pos.shape[1]), 0)
    p = jnp.zeros(r.shape, F32)
    for k in range(TOP_K):
        p = jnp.where(r == pos[k:k + 1], 1.0, p)
    return p.astype(BF16)


def _dispatch_kernel(n_ref, dest_ref, ts_ref, tn_ref, h_ref, posg_ref, xs_hbm,
                     buf, zbuf, sem, zsem, *, ne):
    i = pl.program_id(0)
    slot = i % 2
    pos = posg_ref[0][0:TOP_K].astype(I32)
    buf[slot] = jnp.dot(_one_hot_rows(pos, buf.shape[1]), h_ref[...],
                        preferred_element_type=F32).astype(BF16)

    def run_copy(tile, c):
        s = tile % 2
        return pltpu.make_async_copy(buf.at[s, _chunk_rows(c)],
                                     xs_hbm.at[_chunk_rows(dest_ref[tile, c])], sem.at[s])

    def zero_copy(g):
        return pltpu.make_async_copy(zbuf, xs_hbm.at[_chunk_rows(g)], zsem)

    @pl.when(i == 0)
    def _():
        zbuf[...] = jnp.zeros(zbuf.shape, BF16)
        for phase in ("start", "wait"):
            def per_expert(e, carry):
                def body(j, c):
                    cp = zero_copy(ts_ref[e] + j)
                    cp.start() if phase == "start" else cp.wait()
                    return c
                return lax.fori_loop(0, tn_ref[e], body, carry)
            lax.fori_loop(0, ne, per_expert, 0)

    _for_each_chunk(n_ref[i], lambda c: run_copy(i, c).start(), unroll=DMA_ISSUE_UNROLL)

    def wait_tile(tile):
        s = tile % 2
        _wait_chunks(n_ref[tile], lambda r: pltpu.make_async_copy(
            buf.at[s, 0:r], xs_hbm.at[0:r], sem.at[s]))

    @pl.when(i > 0)
    def _():
        wait_tile(i - 1)

    @pl.when(i == pl.num_programs(0) - 1)
    def _():
        wait_tile(i)


def _dispatch(h, posg, nchunks, dest, tail_start, tail_n, *, nb):
    T, D = h.shape
    nt, _, tm = posg.shape
    ne = tail_n.shape[0]
    return pl.pallas_call(
        functools.partial(_dispatch_kernel, ne=ne),
        grid_spec=pltpu.PrefetchScalarGridSpec(
            num_scalar_prefetch=4, grid=(nt,),
            in_specs=[pl.BlockSpec((tm, D), lambda i, *_: (i, 0)),
                      pl.BlockSpec((1, 2 * TOP_K, tm), lambda i, *_: (i, 0, 0))],
            out_specs=pl.BlockSpec(memory_space=pl.ANY),
            scratch_shapes=[pltpu.VMEM((2, _slots(tm, ne), D), BF16),
                            pltpu.VMEM((CHUNK, D), BF16),
                            pltpu.SemaphoreType.DMA((2,)), pltpu.SemaphoreType.DMA(())]),
        out_shape=jax.ShapeDtypeStruct((nb * MOE_BLOCK, D), BF16),
        compiler_params=_cparams(("arbitrary",)),
        name="dispatch",
    )(nchunks, dest, tail_start, tail_n, h, posg)


def _experts_kernel(be_ref, nu_ref, nxt_ref, edge_ref, xs_ref, wgu_hbm, bgu_ref, wd_hbm, bd_ref,
                    ys_ref, stage_gu, stage_d, wgu_s, wd_s, sem):
    b = pl.program_id(0)

    @pl.when(b < nu_ref[0])
    def _():
        edge = edge_ref[b]
        half = edge // 4
        nxt = nxt_ref[b]

        def weight_copies(e):
            return (pltpu.make_async_copy(wgu_hbm.at[e], stage_gu, sem.at[0]),
                    pltpu.make_async_copy(wd_hbm.at[e], stage_d, sem.at[1]))

        def cast_into(h):
            wgu_s[h] = stage_gu[...].astype(BF16)
            wd_s[h] = stage_d[...].astype(BF16)

        @pl.when(b == 0)
        def _():
            for cp in weight_copies(be_ref[b]):
                cp.start()
            for cp in weight_copies(be_ref[b]):
                cp.wait()
            cast_into(half)

        @pl.when((edge % 2 == 1) & (nxt >= 0))
        def _():
            for cp in weight_copies(nxt):
                cp.start()

        ff = wd_s.shape[1]
        gu = jnp.dot(xs_ref[...], wgu_s[half], preferred_element_type=F32) + bgu_ref[0]
        g = jnp.minimum(gu[:, :ff], SWIGLU_LIMIT)
        u = jnp.clip(gu[:, ff:], -SWIGLU_LIMIT, SWIGLU_LIMIT)
        act = (u + 1.0) * g * jax.nn.sigmoid(SWIGLU_ALPHA * g)
        ys = jnp.dot(act.astype(BF16), wd_s[half], preferred_element_type=F32) + bd_ref[0]
        ys_ref[...] = ys.astype(ys_ref.dtype)

        @pl.when(((edge // 2) % 2 == 1) & (nxt >= 0))
        def _():
            for cp in weight_copies(nxt):
                cp.wait()
            cast_into(1 - half)


def _experts(xs, blk_e, nused, nxt, edge, w_gate_up, b_gate_up, w_down, b_down):
    rows, D = xs.shape
    nb = rows // MOE_BLOCK
    ne, _, ff2 = w_gate_up.shape
    ff = w_down.shape[1]
    blk = lambda b, be, nu, *_: (jnp.minimum(b, nu[0] - 1), 0)
    exp3 = lambda b, be, *_: (be[b], 0, 0)
    return pl.pallas_call(
        _experts_kernel,
        grid_spec=pltpu.PrefetchScalarGridSpec(
            num_scalar_prefetch=4, grid=(nb,),
            in_specs=[pl.BlockSpec((MOE_BLOCK, D), blk),
                      pl.BlockSpec(memory_space=pl.ANY), pl.BlockSpec((1, 1, ff2), exp3),
                      pl.BlockSpec(memory_space=pl.ANY), pl.BlockSpec((1, 1, D), exp3)],
            out_specs=pl.BlockSpec((MOE_BLOCK, D), blk),
            scratch_shapes=[pltpu.VMEM((D, ff2), F32), pltpu.VMEM((ff, D), F32),
                            pltpu.VMEM((2, D, ff2), BF16), pltpu.VMEM((2, ff, D), BF16),
                            pltpu.SemaphoreType.DMA((2,))]),
        out_shape=jax.ShapeDtypeStruct((rows, D), BF16),
        compiler_params=_cparams(("arbitrary",)),
        name="experts",
    )(blk_e, nused, nxt, edge, xs, w_gate_up, b_gate_up[:, None, :], w_down,
      b_down[:, None, :])


def _combine_kernel(n_ref, dest_ref, ys_hbm, posg_ref, x1_ref, y_ref, ylast_ref, buf, sem):
    i = pl.program_id(0)
    last = pl.num_programs(0) - 1
    tm = x1_ref.shape[0]
    nslots = buf.shape[1]

    def run_copy(tile, c):
        s = tile % 2
        return pltpu.make_async_copy(ys_hbm.at[_chunk_rows(dest_ref[tile, c])],
                                     buf.at[s, _chunk_rows(c)], sem.at[s])

    def fetch(tile):
        _for_each_chunk(n_ref[tile], lambda c: run_copy(tile, c).start(),
                        unroll=DMA_ISSUE_UNROLL)
        s = tile % 2

        def zero_chunk(c, carry):
            buf[s, _chunk_rows(c), :] = jnp.zeros((CHUNK, buf.shape[2]), BF16)
            return carry
        lax.fori_loop(n_ref[tile], nslots // CHUNK, zero_chunk, 0)

    @pl.when(i == 0)
    def _():
        fetch(i)

    @pl.when(i < last)
    def _():
        fetch(i + 1)

    r0 = lax.broadcasted_iota(I32, (tm, tm), 0)
    r1 = lax.broadcasted_iota(I32, (tm, tm), 1)
    posg_t = _dot_nt(jnp.where(r0 == r1, 1.0, 0.0), posg_ref[0], exact=True)
    slot = lax.broadcasted_iota(I32, (tm, nslots), 1)
    w = jnp.zeros(slot.shape, F32)
    for k in range(TOP_K):
        w = jnp.where(slot == posg_t[:, k:k + 1].astype(I32),
                      posg_t[:, TOP_K + k:TOP_K + k + 1], w)

    _wait_chunks(n_ref[i], lambda r: pltpu.make_async_copy(
        ys_hbm.at[0:r], buf.at[i % 2, 0:r], sem.at[i % 2]))
    y = x1_ref[...] + jnp.dot(w.astype(BF16), buf[i % 2], preferred_element_type=F32)

    @pl.when(i < last)
    def _():
        y_ref[...] = y

    @pl.when(i == last)
    def _():
        ylast_ref[...] = y


def _combine(ys, posg, x1, nchunks, dest):
    T, D = x1.shape
    nt, _, tm = posg.shape
    return pl.pallas_call(
        _combine_kernel,
        grid_spec=pltpu.PrefetchScalarGridSpec(
            num_scalar_prefetch=2, grid=(nt,),
            in_specs=[pl.BlockSpec(memory_space=pl.ANY),
                      pl.BlockSpec((1, 2 * TOP_K, tm), lambda i, *_: (i, 0, 0)),
                      pl.BlockSpec((tm, D), lambda i, *_: (i, 0))],
            out_specs=[pl.BlockSpec((tm, D), lambda i, *_: (jnp.minimum(i, nt - 2), 0)),
                       pl.BlockSpec((tm, D), lambda i, *_: (0, 0))],
            scratch_shapes=[pltpu.VMEM((2, dest.shape[1] * CHUNK, D), BF16),
                            pltpu.SemaphoreType.DMA((2,))]),
        out_shape=[jax.ShapeDtypeStruct((T - tm, D), F32), jax.ShapeDtypeStruct((tm, D), F32)],
        compiler_params=_cparams(("arbitrary",)),
        name="combine",
    )(nchunks, dest, ys, posg, x1)


def _moe(h, posg, cnt, x1, w_gate_up, b_gate_up, w_down, b_down):
    nt, _, tm = posg.shape
    ne = cnt.shape[1]
    max_chunks = (nt * tm * TOP_K) // CHUNK + nt * ne
    nb = -(-max_chunks // CHUNKS_PER_BLOCK) + ne
    nchunks, dest, tail_start, tail_n, nused, blk_e, nxt, edge = _moe_tables(
        cnt, nb, _slots(tm, ne) // CHUNK)
    xs = _dispatch(h, posg, nchunks, dest, tail_start, tail_n, nb=nb)
    ys = _experts(xs, blk_e, nused, nxt, edge, w_gate_up, b_gate_up, w_down, b_down)
    return _combine(ys, posg, x1, nchunks, dest)


def kernel(x_prompt, x_sample, cache_k, cache_v, state_conv, page_table, norm1_w, w_in,
           q_norm_w, k_norm_w, lambda_q1, lambda_k1, lambda_q2, lambda_k2, subln_w,
           conv_w, conv_b, conv_ln_g, conv_ln_b, w_out, norm2_w, router_w, router_b,
           w_gate_up, b_gate_up, w_down, b_down):
    B, S, D = x_prompt.shape
    Bs, Ss, _ = x_sample.shape
    depth = norm1_w.shape[0]
    n_phys, page, heads, _, qk = cache_k.shape[1:]
    vdim = cache_v.shape[-1]
    qc, vc, cc = heads * 2 * qk, heads * vdim, conv_w.shape[2]
    taps = conv_w.shape[1]
    assert Ss == 1 and qk == QK_GROUP and vdim == V7X_LANES and (B * S) % MOE_TILE == 0
    assert Bs <= MOE_TILE and S >= taps - 1
    n_past = page_table.shape[1] * page
    T = B * S
    nt_p = T // MOE_TILE
    pos_p = jnp.arange(S, dtype=F32)
    pos_s = jnp.full((Bs,), n_past, F32)
    xp = x_prompt.reshape(T, D)
    xs = x_sample.reshape(Bs, D)
    pad_tile = lambda a: jnp.pad(a, ((0, MOE_TILE - Bs), (0, 0)))
    outs = [[] for _ in range(6)]
    for l in range(depth):
        lam_init = 0.8 - 0.6 * math.exp(-0.3 * l)
        lamv = jnp.stack([lambda_q1[l], lambda_k1[l], lambda_q2[l], lambda_k2[l]])
        conv_p = (conv_w[l], conv_b[l], conv_ln_g[l], conv_ln_b[l])
        proj_p = (norm1_w[l], w_in[l], q_norm_w[l], k_norm_w[l])
        tail_p = (w_out[l], norm2_w[l], router_w[l], router_b[l])

        q, kt, v, u, kbt, vb = _proj(xp, pos_p, S // PROJ_TILE, PROJ_TILE, *proj_p,
                                     qc=qc, vc=vc, cc=cc, exact_norm=False, attn_layout=True)
        qs, ks_, vs_, us = _proj(xs, pos_s, 1, Bs, *proj_p, qc=qc, vc=vc, cc=cc,
                                 exact_norm=True, attn_layout=False)
        cache_kt = jnp.transpose(cache_k[l], (0, 2, 3, 4, 1)).reshape(n_phys, qc, page)
        cache_vr = cache_v[l].reshape(n_phys, page * heads, vdim)
        attn, attn_s = _attention(q, kbt, vb, qs, ks_, vs_, cache_kt, cache_vr, page_table, lamv,
                                  subln_w[l], lam_init=lam_init)

        bufs = _tail(attn, u, xp, *tail_p, tm=MOE_TILE, n_valid=MOE_TILE,
                     total_tiles=nt_p + 1, conv_params=conv_p, seq=S)
        outs[0].append(jnp.transpose(kt.reshape(B, heads, 2, qk, S), (0, 4, 1, 2, 3)))
        outs[1].append(v.reshape(B, S, heads, vdim))
        outs[2].append(u.reshape(B, S, cc)[:, S - (taps - 1):])

        conv_s = _conv_step(state_conv[l], us, *conv_p)
        x1, h, posg, cnt = _tail(pad_tile(attn_s), pad_tile(conv_s), pad_tile(xs), *tail_p,
                                 tm=MOE_TILE, n_valid=Bs, total_tiles=nt_p + 1,
                                 first_tile=nt_p, into=bufs)
        outs[3].append(ks_.reshape(Bs, Ss, heads, 2, qk))
        outs[4].append(vs_.reshape(Bs, Ss, heads, vdim))
        outs[5].append(jnp.concatenate([state_conv[l][:, Ss:], us[:, None, :]], axis=1))

        xp, y_last = _moe(h, posg, cnt[:, :, 0], x1, w_gate_up[l], b_gate_up[l], w_down[l],
                          b_down[l])
        xs = y_last[:Bs]
    return (xp.reshape(B, S, D), xs.reshape(Bs, Ss, D)) + tuple(jnp.stack(o) for o in outs)
```

```python
import functools
import math

import jax
import jax.numpy as jnp
from jax import lax
from jax.experimental import pallas as pl
from jax.experimental.pallas import tpu as pltpu

F32 = jnp.float32
BF16 = jnp.bfloat16
I32 = jnp.int32
HIGHEST = lax.Precision.HIGHEST

EPS = 1e-6
ROPE_THETA = 10000.0
SWIGLU_LIMIT = 7.0
SWIGLU_ALPHA = 1.702
TOP_K = 4
NEG = -1e30
QK_GROUP = 64

V7X_LANES = 128
V7X_SUBLANES = 8
VMEM_LIMIT = 56 * 1024 * 1024
BF16_ROWS = 16

MOE_TILE = 256
CHUNK = BF16_ROWS
MOE_BLOCK = 512
CHUNKS_PER_BLOCK = MOE_BLOCK // CHUNK
PROJ_TILE = 512


def _cparams(sem, vmem=VMEM_LIMIT):
    return pltpu.CompilerParams(dimension_semantics=sem, vmem_limit_bytes=vmem)


def _dot(a, b, exact=False):
    if exact:
        return jnp.dot(a.astype(F32), b.astype(F32), precision=HIGHEST,
                       preferred_element_type=F32)
    return jnp.dot(a.astype(BF16), b.astype(BF16), preferred_element_type=F32)


def _dot_nt(a, b, exact=False):
    dn = (((1,), (1,)), ((), ()))
    if exact:
        return lax.dot_general(a.astype(F32), b.astype(F32), dn, precision=HIGHEST,
                               preferred_element_type=F32)
    return lax.dot_general(a.astype(BF16), b.astype(BF16), dn, preferred_element_type=F32)


def _bf16_round(x):
    return x.astype(BF16).astype(F32)


def _rope_norm(p, gsum, w, cos, sin, first_half, exact_norm):
    ss = _dot(p * p, gsum, exact_norm)
    n = p * lax.rsqrt(ss * (1.0 / QK_GROUP) + EPS) * w
    outs = []
    for j in range(p.shape[1] // V7X_LANES):
        nj = n[:, j * V7X_LANES:(j + 1) * V7X_LANES]
        rot = jnp.where(first_half, pltpu.roll(nj, V7X_LANES - QK_GROUP // 2, 1),
                        pltpu.roll(nj, QK_GROUP // 2, 1))
        outs.append(nj * cos + rot * sin)
    return jnp.concatenate(outs, axis=1)


def _proj_kernel(x_ref, n1_ref, w_ref, qw_ref, kw_ref, cos_ref, sin_ref, gsum_ref,
                 q_ref, k_ref, v_ref, u_ref, *rest, qc, vc, cc, scale, exact_norm):
    x = x_ref[...]
    h = x * lax.rsqrt(jnp.mean(x * x, axis=-1, keepdims=True) + EPS) * n1_ref[...]
    hm = h.astype(BF16)
    cos = cos_ref[...]
    sin = sin_ref[...]
    lane = lax.broadcasted_iota(I32, cos.shape, 1)
    first_half = (lane % QK_GROUP) < QK_GROUP // 2
    gsum = gsum_ref[...]

    q = _rope_norm(_dot(hm, w_ref[:, 0:qc]), gsum, qw_ref[...], cos, sin, first_half, exact_norm)
    q_ref[...] = (q * scale).astype(q_ref.dtype)
    k = _rope_norm(_dot(hm, w_ref[:, qc:2 * qc]), gsum, kw_ref[...], cos, sin, first_half,
                   exact_norm)
    v = _dot(hm, w_ref[:, 2 * qc:2 * qc + vc])
    o = 2 * qc + vc
    ua = _dot(hm, w_ref[:, o:o + cc])
    ub = _dot(hm, w_ref[:, o + cc:o + 2 * cc])
    u_ref[...] = ua * jax.nn.sigmoid(ub)
    if rest:
        kb_ref, vb_ref = rest
        kt = k.T
        k_ref[0] = kt
        kb_ref[0, :, 0] = kt.astype(BF16).reshape(kb_ref.shape[1], kb_ref.shape[3], kt.shape[1])
        vb_ref[...] = v.astype(BF16)
        nh = vc // V7X_LANES
        for h in range(nh):
            v_ref[pl.ds(h, v.shape[0], stride=nh), :] = v[:, h * V7X_LANES:(h + 1) * V7X_LANES]
    else:
        k_ref[...] = k
        v_ref[...] = v


def _rope_tables(pos):
    half = QK_GROUP // 2
    inv = jnp.power(ROPE_THETA, -jnp.arange(half, dtype=F32) / half)
    ang = pos[:, None] * inv[None, :]
    reps = V7X_LANES // QK_GROUP
    cos = jnp.tile(jnp.cos(ang), (1, 2 * reps))
    s = jnp.sin(ang)
    sin = jnp.tile(jnp.concatenate([-s, s], axis=1), (1, reps))
    return cos, sin


def _proj(x2d, pos_rows, n_pos_blocks, tm, norm1_w, w_in, q_norm_w, k_norm_w, *, qc, vc, cc,
          exact_norm, attn_layout):
    T, D = x2d.shape
    cos, sin = _rope_tables(pos_rows)
    gi = jnp.arange(qc) // QK_GROUP
    gsum = (gi[:, None] == gi[None, :]).astype(F32 if exact_norm else BF16)
    qw = jnp.tile(q_norm_w, qc // QK_GROUP)[None, :]
    kw = jnp.tile(k_norm_w, qc // QK_GROUP)[None, :]
    w = w_in.astype(BF16)
    row = lambda i: (i, 0)
    full = lambda i: (0, 0)
    out_shape = [jax.ShapeDtypeStruct((T, qc), BF16),
                 jax.ShapeDtypeStruct((T, qc), F32),
                 jax.ShapeDtypeStruct((T, vc), F32),
                 jax.ShapeDtypeStruct((T, cc), F32)]
    out_specs = [pl.BlockSpec((tm, qc), row), pl.BlockSpec((tm, qc), row),
                 pl.BlockSpec((tm, vc), row), pl.BlockSpec((tm, cc), row)]
    if attn_layout:
        nseq = T // (n_pos_blocks * tm)
        heads = qc // (2 * QK_GROUP)
        seq_tile = lambda i: (i // n_pos_blocks, 0, i % n_pos_blocks)
        out_shape[1] = jax.ShapeDtypeStruct((nseq, qc, n_pos_blocks * tm), F32)
        out_specs[1] = pl.BlockSpec((1, qc, tm), seq_tile)
        out_shape[2] = jax.ShapeDtypeStruct((T * vc // V7X_LANES, V7X_LANES), F32)
        out_specs[2] = pl.BlockSpec((tm * vc // V7X_LANES, V7X_LANES), row)
        out_shape += [jax.ShapeDtypeStruct((nseq, heads, n_pos_blocks, 2 * QK_GROUP, tm), BF16),
                      jax.ShapeDtypeStruct((T, vc), BF16)]
        out_specs += [pl.BlockSpec((1, heads, 1, 2 * QK_GROUP, tm),
                                   lambda i: (i // n_pos_blocks, 0, i % n_pos_blocks, 0, 0)),
                      pl.BlockSpec((tm, vc), row)]
    return pl.pallas_call(
        functools.partial(_proj_kernel, qc=qc, vc=vc, cc=cc, scale=QK_GROUP ** -0.5,
                          exact_norm=exact_norm),
        grid=(T // tm,),
        in_specs=[pl.BlockSpec((tm, D), row),
                  pl.BlockSpec((1, D), full),
                  pl.BlockSpec(w.shape, full),
                  pl.BlockSpec((1, qc), full),
                  pl.BlockSpec((1, qc), full),
                  pl.BlockSpec((tm, V7X_LANES), lambda i: (i % n_pos_blocks, 0)),
                  pl.BlockSpec((tm, V7X_LANES), lambda i: (i % n_pos_blocks, 0)),
                  pl.BlockSpec((qc, qc), full)],
        out_specs=out_specs,
        out_shape=out_shape,
        compiler_params=_cparams(("arbitrary",)),
        name="proj" if attn_layout else "proj_step",
    )(x2d, norm1_w[None, :], w, qw, kw, cos, sin, gsum)


def _lambda_value(lv, lam_init):
    a = jnp.sum(lv[0:1] * lv[1:2], axis=-1, keepdims=True)
    b = jnp.sum(lv[2:3] * lv[3:4], axis=-1, keepdims=True)
    return jnp.exp(a) - jnp.exp(b) + lam_init


def _subln(o, w, lam_init):
    y = o * lax.rsqrt(jnp.mean(o * o, axis=-1, keepdims=True) + EPS)
    return y * w * (1.0 - lam_init)


def _attn_body(i, lam_ref, sw_ref, q_ref, k_ref, v_ref, o_ref, *, tq, lam_init):
    lam = _lambda_value(lam_ref[...], lam_init)
    q = q_ref[...]
    lane = lax.broadcasted_iota(I32, q.shape, 1)
    zero = jnp.zeros_like(q)
    qs = (jnp.where(lane < QK_GROUP, q, zero), jnp.where(lane >= QK_GROUP, q, zero))

    def chunk(j, carry, masked):
        kc = k_ref[0, 0, j]
        vc = v_ref[pl.ds(pl.multiple_of(j * tq, tq), tq), :]
        out = []
        for c in range(2):
            m, l, acc = carry[c]
            s = jnp.dot(qs[c], kc, preferred_element_type=F32)
            if masked:
                row = lax.broadcasted_iota(I32, s.shape, 0)
                col = lax.broadcasted_iota(I32, s.shape, 1)
                s = jnp.where(col <= row, s, NEG)
            m_new = jnp.maximum(m, jnp.max(s, axis=-1, keepdims=True))
            p = jnp.exp(s - m_new)
            alpha = jnp.exp(m - m_new)
            l = alpha * l + jnp.sum(p, axis=-1, keepdims=True)
            acc = alpha * acc + _dot(p, vc)
            out.append((m_new, l, acc))
        return tuple(out)

    init = tuple((jnp.full((tq, 1), NEG, F32), jnp.zeros((tq, 1), F32),
                  jnp.zeros((tq, V7X_LANES), F32)) for _ in range(2))
    carry = lax.fori_loop(0, i, lambda j, c: chunk(j, c, False), init)
    (_, l0, a0), (_, l1, a1) = chunk(i, carry, True)
    o = a0 / l0 - lam * (a1 / l1)
    o_ref[...] = _subln(o, sw_ref[...], lam_init).astype(o_ref.dtype)


DECODE_PAGES_PER_STEP = 32
SOFTMAX_PAGES = 16
PAGE_RING_DEPTH = 4


def _decode_body(s, half_steps, lam_ref, sw_ref, qm_ref, kn_ref, vn_ref, pages_ref,
                 o_ref, s_ref, m_ref, coef_ref, acc_ref, *, page, heads, lam_init):
    pps = pages_ref.shape[0]
    rows = s_ref.shape[1]
    qm = qm_ref[0]
    row = lax.broadcasted_iota(I32, (rows, V7X_LANES), 0)
    n_pages = half_steps * pps

    @pl.when(s == 0)
    def _():
        m_ref[...] = jnp.full(m_ref.shape, NEG, F32)

    @pl.when(s < half_steps)
    def _():
        m = m_ref[...]
        for j in range(pps):
            sc = jnp.dot(qm, pages_ref[j].astype(BF16), preferred_element_type=F32)
            s_ref[s * pps + j] = sc
            m = jnp.maximum(m, sc)
        m_ref[...] = m

    def head_weights(p):
        a = p * coef_ref[...]
        return _bf16_round(a + pltpu.roll(a, rows - heads, 0))

    @pl.when(s == half_steps)
    def _():
        s_new = jnp.sum(qm.astype(F32) * _bf16_round(kn_ref[0]), axis=-1, keepdims=True)
        m = jnp.maximum(jnp.max(m_ref[...], axis=-1, keepdims=True), s_new)

        def exp_pages(t, l):
            pages = pl.ds(pl.multiple_of(t * SOFTMAX_PAGES, SOFTMAX_PAGES), SOFTMAX_PAGES)
            p = jnp.exp(s_ref[pages] - m[None])
            s_ref[pages] = p
            return l + jnp.sum(p, axis=0)

        lsum = lax.fori_loop(0, n_pages // SOFTMAX_PAGES, exp_pages,
                             jnp.zeros((rows, V7X_LANES), F32))
        p_new = jnp.exp(s_new - m)
        l = jnp.sum(lsum, axis=-1, keepdims=True) + p_new
        lam = _lambda_value(lam_ref[...], lam_init)
        coef = jnp.where(row[:, 0:1] < heads, 1.0, -lam) / l
        coef_ref[...] = jnp.broadcast_to(coef, coef_ref.shape)
        acc_ref[...] = (head_weights(jnp.broadcast_to(p_new, (rows, V7X_LANES)))
                        * _bf16_round(vn_ref[0]))

    @pl.when(s >= half_steps)
    def _():
        lane = lax.broadcasted_iota(I32, (rows, V7X_LANES), 1)
        keep = (lane % heads == row) & (row < heads)
        acc = acc_ref[...]
        for j in range(pps):
            a = head_weights(s_ref[(s - half_steps) * pps + j])
            parts = []
            for c in range(heads):
                idx = (c * page + lane) // heads
                parts.append(jnp.where(keep, jnp.take_along_axis(a, idx, axis=1), 0.0))
            a_exp = jnp.concatenate(parts, axis=1).astype(BF16)
            acc = acc + jnp.dot(a_exp, pages_ref[j].astype(BF16), preferred_element_type=F32)
        acc_ref[...] = acc

    @pl.when(s == 2 * half_steps - 1)
    def _():
        o_ref[0] = _subln(acc_ref[...], sw_ref[...], lam_init)


def _attention_kernel(pt_ref, lam_ref, sw_ref, q_ref, k_ref, v_ref, qm_ref, kn_ref, vn_ref,
                      kt_hbm, vr_hbm, o_ref, od_ref, s_ref, m_ref, coef_ref, acc_ref, pages, sem,
                      *, ratio, nq, steps_per_seq, tq, page, heads, lam_init):
    t = pl.program_id(0)
    depth, pps = pages.shape[:2]
    half_steps = steps_per_seq // 2

    def fetch(step):
        seq = step // steps_per_seq
        s = step % steps_per_seq
        first = (s % half_steps) * pps
        slot = step % depth
        for src_hbm, cond in ((kt_hbm, s < half_steps), (vr_hbm, s >= half_steps)):
            @pl.when(cond)
            def _():
                for j in range(pps):
                    pltpu.make_async_copy(src_hbm.at[pt_ref[seq, first + j]],
                                          pages.at[slot, j], sem.at[slot]).start()

    @pl.when(t == 0)
    def _():
        for ahead in range(depth - 1):
            fetch(t + ahead)

    @pl.when(t + depth - 1 < pl.num_programs(0))
    def _():
        fetch(t + depth - 1)

    slot = t % depth
    pltpu.make_async_copy(kt_hbm.at[0:pps], pages.at[slot], sem.at[slot]).wait()
    _decode_body(t % steps_per_seq, half_steps, lam_ref, sw_ref, qm_ref, kn_ref, vn_ref,
                 pages.at[slot], od_ref, s_ref, m_ref, coef_ref, acc_ref,
                 page=page, heads=heads, lam_init=lam_init)

    @pl.when(t % ratio == 0)
    def _():
        _attn_body((t // ratio) % nq, lam_ref, sw_ref, q_ref, k_ref, v_ref, o_ref,
                   tq=tq, lam_init=lam_init)


def _attention(q, kbt, vb, q_step, k_new, v_new, cache_kt, cache_vr, page_table, lamv, subln_w,
               *, lam_init):
    batch, heads, nq, _, tq = kbt.shape
    seq = nq * tq
    Bs, D = q_step.shape
    n_pages = page_table.shape[1]
    page = cache_kt.shape[2]
    vdim = cache_vr.shape[2]
    pps = DECODE_PAGES_PER_STEP
    half_steps = n_pages // pps
    steps_per_seq = 2 * half_steps
    rows = 2 * heads
    n_dec, n_att = Bs * steps_per_seq, batch * heads * nq
    assert rows == V7X_SUBLANES and page == V7X_LANES and vdim == V7X_LANES
    assert n_pages % pps == 0 and n_pages % SOFTMAX_PAGES == 0 and n_dec % n_att == 0
    assert cache_kt.shape[1:] == cache_vr.shape[1:]
    ratio = n_dec // n_att
    group = jnp.arange(D) // QK_GROUP
    rowmask = ((group % 2) * heads + group // 2)[None, :] == jnp.arange(rows)[:, None]
    qm = jnp.where(rowmask[None], q_step[:, None, :], jnp.zeros((), BF16))
    vn = jnp.pad(v_new.reshape(Bs, heads, vdim), ((0, 0), (0, rows - heads), (0, 0)))

    def att(t):
        a = t // ratio
        return a // (heads * nq), (a // nq) % heads, a % nq

    def q_map(t, pt):
        b, h, i = att(t)
        return b * nq + i, h

    seq3 = lambda t, pt: (t // steps_per_seq, 0, 0)
    const = lambda t, pt: (0, 0)
    out, out_step = pl.pallas_call(
        functools.partial(_attention_kernel, ratio=ratio, nq=nq,
                          steps_per_seq=steps_per_seq, tq=tq, page=page, heads=heads,
                          lam_init=lam_init),
        grid_spec=pltpu.PrefetchScalarGridSpec(
            num_scalar_prefetch=1, grid=(n_dec,),
            in_specs=[pl.BlockSpec(lamv.shape, const), pl.BlockSpec((1, vdim), const),
                      pl.BlockSpec((tq, V7X_LANES), q_map),
                      pl.BlockSpec((1, 1, nq, V7X_LANES, tq),
                                   lambda t, pt: att(t)[:2] + (0, 0, 0)),
                      pl.BlockSpec((seq, V7X_LANES), lambda t, pt: att(t)[:2]),
                      pl.BlockSpec((1, rows, D), seq3), pl.BlockSpec((1, 1, D), seq3),
                      pl.BlockSpec((1, rows, vdim), seq3),
                      pl.BlockSpec(memory_space=pl.ANY), pl.BlockSpec(memory_space=pl.ANY)],
            out_specs=[pl.BlockSpec((tq, V7X_LANES), q_map),
                       pl.BlockSpec((1, rows, vdim), seq3)],
            scratch_shapes=[pltpu.VMEM((n_pages, rows, page), F32),
                            pltpu.VMEM((rows, V7X_LANES), F32),
                            pltpu.VMEM((rows, V7X_LANES), F32),
                            pltpu.VMEM((rows, vdim), F32),
                            pltpu.VMEM((PAGE_RING_DEPTH, pps) + cache_kt.shape[1:], F32),
                            pltpu.SemaphoreType.DMA((PAGE_RING_DEPTH,))]),
        out_shape=[jax.ShapeDtypeStruct((batch * seq, heads * V7X_LANES), BF16),
                   jax.ShapeDtypeStruct((Bs, rows, vdim), F32)],
        compiler_params=_cparams(("arbitrary",)),
        name="attention",
    )(page_table, lamv, subln_w[None, :], q, kbt, vb, qm, k_new[:, None, :], vn,
      cache_kt, cache_vr)
    return out, out_step[:, :heads, :].reshape(Bs, heads * vdim)


CONV_HALO = 32


def _ln_swish(y, b_ref, g_ref, be_ref):
    y = y + b_ref[...]
    mu = jnp.mean(y, axis=-1, keepdims=True)
    yc = y - mu
    z = yc * lax.rsqrt(jnp.mean(yc * yc, axis=-1, keepdims=True) + EPS) * g_ref[...] + be_ref[...]
    return z * jax.nn.sigmoid(z)


def _conv_tile(u_ref, starts_sequence, w_ref, b_ref, g_ref, be_ref, buf_ref, part_ref):
    sub = V7X_SUBLANES
    tc = u_ref.shape[0]
    taps = w_ref.shape[0]
    first = CONV_HALO - (taps - 1)

    @pl.when(starts_sequence)
    def _():
        buf_ref[0:CONV_HALO, :] = jnp.zeros((CONV_HALO, buf_ref.shape[1]), F32)
        buf_ref[CONV_HALO + tc:, :] = jnp.zeros((sub, buf_ref.shape[1]), F32)

    buf_ref[CONV_HALO:CONV_HALO + tc, :] = _bf16_round(u_ref[...])
    acc = None
    for r in range(sub):
        part = None
        for a in range(-(-(first + taps) // sub)):
            k = sub * a + r - first
            if 0 <= k < taps:
                term = buf_ref[sub * a:sub * a + tc + sub, :] * _bf16_round(w_ref[k:k + 1, :])
                part = term if part is None else part + term
        if r == 0:
            acc = part[0:tc]
        else:
            part_ref[...] = part
            acc = acc + part_ref[r:r + tc, :]
    out = _ln_swish(acc, b_ref, g_ref, be_ref)
    buf_ref[0:CONV_HALO, :] = buf_ref[tc:tc + CONV_HALO, :]
    return out


def _conv_step_kernel(st_ref, u_ref, w_ref, b_ref, g_ref, be_ref, o_ref, *, taps):
    acc = u_ref[...] * w_ref[taps - 1:taps, :]
    for k in range(taps - 1):
        acc = acc + st_ref[:, k, :] * w_ref[k:k + 1, :]
    o_ref[...] = _ln_swish(acc, b_ref, g_ref, be_ref)


def _conv_step(state, u, conv_w, conv_b, ln_g, ln_b):
    taps, C = conv_w.shape
    return pl.pallas_call(
        functools.partial(_conv_step_kernel, taps=taps),
        out_shape=jax.ShapeDtypeStruct(u.shape, F32),
        compiler_params=_cparams(None),
        name="conv_step",
    )(state, u, conv_w, conv_b[None, :], ln_g[None, :], ln_b[None, :])


def _tail_kernel(a_ref, c_ref, x_ref, wo_ref, n2_ref, rw_ref, rb_ref, *rest, n_valid,
                 tiles_per_seq):
    tm = x_ref.shape[0]
    half = a_ref.shape[1]
    if tiles_per_seq:
        cw_ref, cb_ref, cg_ref, cbe_ref = rest[:4]
        buf_ref, part_ref = rest[-2:]
        x1_ref, h_ref, posg_ref, cnt_ref = rest[-6:-2]
        conv = _conv_tile(c_ref, pl.program_id(0) % tiles_per_seq == 0, cw_ref, cb_ref, cg_ref,
                          cbe_ref, buf_ref, part_ref)
    else:
        x1_ref, h_ref, posg_ref, cnt_ref = rest[-4:]
        conv = c_ref[...]
    x1 = (x_ref[...] + _dot(a_ref[...], wo_ref[0:half, :]) + _dot(conv, wo_ref[half:, :]))
    x1_ref[...] = x1
    h = (x1 * lax.rsqrt(jnp.mean(x1 * x1, axis=-1, keepdims=True) + EPS)
         * n2_ref[...]).astype(BF16)
    h_ref[...] = h
    logits = _dot_nt(rw_ref[...], h) + rb_ref[...]
    ne = logits.shape[0]
    eidx = lax.broadcasted_iota(I32, logits.shape, 0)
    valid = lax.broadcasted_iota(I32, (1, tm), 1) < n_valid

    sels, vals = [], []
    l = logits
    for _ in range(TOP_K):
        m = jnp.max(l, axis=0, keepdims=True)
        first = jnp.min(jnp.where(l == m, eidx, ne), axis=0, keepdims=True)
        sel = (eidx == first) & valid
        l = jnp.where(eidx == first, -jnp.inf, l)
        sels.append(sel)
        vals.append(m)
    ex = [jnp.exp(v - vals[0]) for v in vals]
    den = ex[0] + ex[1] + ex[2] + ex[3]
    gates = [jnp.where(valid, e / den, 0.0) for e in ex]

    msel = jnp.zeros(logits.shape, F32)
    for sel in sels:
        msel = msel + jnp.where(sel, 1.0, 0.0)
    r0 = lax.broadcasted_iota(I32, (tm, tm), 0)
    r1 = lax.broadcasted_iota(I32, (tm, tm), 1)
    upper = jnp.where(r0 < r1, 1.0, 0.0).astype(BF16)
    rank = jnp.dot(msel.astype(BF16), upper, preferred_element_type=F32)
    cnt = jnp.sum(msel, axis=1, keepdims=True)
    pcnt = jnp.ceil(cnt * (1.0 / CHUNK)) * CHUNK
    e0 = lax.broadcasted_iota(I32, (ne, ne), 0)
    e1 = lax.broadcasted_iota(I32, (ne, ne), 1)
    lower = jnp.where(e1 < e0, 1.0, 0.0)
    off = jnp.dot(lower.astype(BF16), jnp.broadcast_to(pcnt, (ne, V7X_LANES)).astype(BF16),
                  preferred_element_type=F32)[:, 0:1]
    pos = off + rank
    rows = [jnp.where(valid, jnp.sum(jnp.where(sel, pos, 0.0), axis=0, keepdims=True), -1.0)
            for sel in sels]
    posg_ref[0] = jnp.concatenate(rows + gates, axis=0)
    cnt_ref[0] = jnp.broadcast_to(cnt, (ne, V7X_LANES)).astype(I32)


def _tail(attn, conv, x2d, w_out, norm2_w, router_w, router_b, *, tm, n_valid, total_tiles,
          first_tile=0, into=None, conv_params=None, seq=None):
    T, D = x2d.shape
    half = attn.shape[1]
    ne = router_w.shape[1]
    nt = T // tm
    wo = w_out.astype(BF16)
    rw = router_w.T.astype(BF16)
    row = lambda i: (i, 0)
    full = lambda i: (0, 0)
    orow = lambda i: (first_tile + i, 0)
    otile = lambda i: (first_tile + i, 0, 0)
    extra = list(into) if into is not None else []
    conv_in, conv_specs, scratch, tiles_per_seq = [], [], [], 0
    if conv_params is not None:
        conv_w, conv_b, ln_g, ln_b = conv_params
        taps, C = conv_w.shape
        conv_in = [conv_w, conv_b[None, :], ln_g[None, :], ln_b[None, :]]
        conv_specs = [pl.BlockSpec((taps, C), full)] + [pl.BlockSpec((1, C), full)] * 3
        scratch = [pltpu.VMEM((tm + CONV_HALO + V7X_SUBLANES, C), F32),
                   pltpu.VMEM((tm + V7X_SUBLANES, C), F32)]
        tiles_per_seq = seq // tm
    return pl.pallas_call(
        functools.partial(_tail_kernel, n_valid=n_valid, tiles_per_seq=tiles_per_seq),
        grid=(nt,),
        in_specs=[pl.BlockSpec((tm, half), row), pl.BlockSpec((tm, half), row),
                  pl.BlockSpec((tm, D), row), pl.BlockSpec((D, D), full),
                  pl.BlockSpec((1, D), full), pl.BlockSpec((ne, D), full),
                  pl.BlockSpec((ne, 1), full)] + conv_specs
                 + [pl.BlockSpec(memory_space=pl.ANY)] * len(extra),
        out_specs=[pl.BlockSpec((tm, D), orow), pl.BlockSpec((tm, D), orow),
                   pl.BlockSpec((1, 2 * TOP_K, tm), otile),
                   pl.BlockSpec((1, ne, V7X_LANES), otile)],
        out_shape=[jax.ShapeDtypeStruct((total_tiles * tm, D), F32),
                   jax.ShapeDtypeStruct((total_tiles * tm, D), BF16),
                   jax.ShapeDtypeStruct((total_tiles, 2 * TOP_K, tm), F32),
                   jax.ShapeDtypeStruct((total_tiles, ne, V7X_LANES), I32)],
        scratch_shapes=scratch,
        input_output_aliases={7 + len(conv_in) + j: j for j in range(len(extra))},
        compiler_params=_cparams(("arbitrary",)),
        name="tail",
    )(attn, conv, x2d, wo, norm2_w[None, :], rw, router_b[:, None], *conv_in, *extra)


def _slots(tm, ne):
    worst = TOP_K * tm + ne * (CHUNK - 1)
    return -(-worst // V7X_LANES) * V7X_LANES


def _prefix_sum(x, axis, exclusive):
    n = x.shape[axis]
    i = jnp.arange(n)
    tri = (i[:, None] < i[None, :]) if exclusive else (i[:, None] <= i[None, :])
    xm = jnp.moveaxis(x, axis, -1)
    out = jnp.sum(xm[..., :, None] * tri.astype(x.dtype), axis=-2)
    return jnp.moveaxis(out, -1, axis)


def _moe_tables(cnt, nb, slot_chunks):
    nch = (cnt + (CHUNK - 1)) // CHUNK
    tot = jnp.sum(nch, axis=0)
    nblk = (tot + (CHUNKS_PER_BLOCK - 1)) // CHUNKS_PER_BLOCK
    bend = _prefix_sum(nblk, 0, exclusive=False)
    gstart = (bend - nblk) * CHUNKS_PER_BLOCK
    rs = gstart[None, :] + _prefix_sum(nch, 0, exclusive=True)
    tail_start = gstart + tot
    tail_n = nblk * CHUNKS_PER_BLOCK - tot
    nused = bend[-1:]
    blk = jnp.minimum(jnp.arange(nb, dtype=I32), nused - 1)
    blk_e = jnp.sum((bend[None, :] <= blk[:, None]).astype(I32), axis=1)
    cend = _prefix_sum(nch, 1, exclusive=False)
    c = jnp.arange(slot_chunks, dtype=I32)
    run = jnp.minimum(jnp.sum((cend[:, None, :] <= c[None, :, None]).astype(I32), axis=2),
                      nch.shape[1] - 1)
    shift = rs - (cend - nch)
    dest = c[None, :] + jnp.sum(jnp.where(run[:, :, None] == jnp.arange(nch.shape[1]),
                                          shift[:, None, :], 0), axis=2)
    ne = nblk.shape[0]
    eid = jnp.arange(ne, dtype=I32)
    later = jnp.where((eid[None, :] > eid[:, None]) & (nblk[None, :] > 0), eid[None, :], ne)
    next_e = jnp.min(later, axis=1)
    next_e = jnp.where(next_e == ne, -1, next_e)
    rank_e = _prefix_sum((nblk > 0).astype(I32), 0, exclusive=True)
    bidx = jnp.arange(nb, dtype=I32)
    mine = (blk_e[:, None] == eid[None, :]).astype(I32)
    of_block = lambda per_expert: jnp.sum(mine * per_expert[None, :], axis=1)
    first = (bidx == of_block(bend - nblk)) & (bidx < nused)
    last = (bidx == of_block(bend) - 1) & (bidx < nused)
    edge = first.astype(I32) + 2 * last.astype(I32) + 4 * (of_block(rank_e) % 2)
    i32 = lambda a: a.astype(I32)
    return (i32(cend[:, -1]), i32(dest), i32(tail_start), i32(tail_n), i32(nused), i32(blk_e),
            i32(of_block(next_e)), i32(edge))


def _chunk_rows(c):
    return pl.ds(pl.multiple_of(c * CHUNK, CHUNK), CHUNK)


def _for_each_chunk(n, fn, unroll=1):
    def group(g, carry):
        for u in range(unroll):
            fn(g * unroll + u)
        return carry
    lax.fori_loop(0, n // unroll, group, 0)

    def single(c, carry):
        fn(c)
        return carry
    lax.fori_loop((n // unroll) * unroll, n, single, 0)


WAIT_GROUP = 8
DMA_ISSUE_UNROLL = 8


def _wait_chunks(n, copy_of_rows):
    _for_each_chunk(n // WAIT_GROUP, lambda c: copy_of_rows(WAIT_GROUP * CHUNK).wait())
    _for_each_chunk(n % WAIT_GROUP, lambda c: copy_of_rows(CHUNK).wait())


def _one_hot_rows(pos, nrows):
    r = lax.broadcasted_iota(I32, (nrows, pos.shape[1]), 0)
    p = jnp.zeros(r.shape, F32)
    for k in range(TOP_K):
        p = jnp.where(r == pos[k:k + 1], 1.0, p)
    return p.astype(BF16)


def _dispatch_kernel(n_ref, dest_ref, ts_ref, tn_ref, h_ref, posg_ref, xs_hbm,
                     buf, zbuf, sem, zsem, *, ne):
    i = pl.program_id(0)
    slot = i % 2
    pos = posg_ref[0][0:TOP_K].astype(I32)
    buf[slot] = jnp.dot(_one_hot_rows(pos, buf.shape[1]), h_ref[...],
                        preferred_element_type=F32).astype(BF16)

    def run_copy(tile, c):
        s = tile % 2
        return pltpu.make_async_copy(buf.at[s, _chunk_rows(c)],
                                     xs_hbm.at[_chunk_rows(dest_ref[tile, c])], sem.at[s])

    def zero_copy(g):
        return pltpu.make_async_copy(zbuf, xs_hbm.at[_chunk_rows(g)], zsem)

    @pl.when(i == 0)
    def _():
        zbuf[...] = jnp.zeros(zbuf.shape, BF16)
        for phase in ("start", "wait"):
            def per_expert(e, carry):
                def body(j, c):
                    cp = zero_copy(ts_ref[e] + j)
                    cp.start() if phase == "start" else cp.wait()
                    return c
                return lax.fori_loop(0, tn_ref[e], body, carry)
            lax.fori_loop(0, ne, per_expert, 0)

    _for_each_chunk(n_ref[i], lambda c: run_copy(i, c).start(), unroll=DMA_ISSUE_UNROLL)

    def wait_tile(tile):
        s = tile % 2
        _wait_chunks(n_ref[tile], lambda r: pltpu.make_async_copy(
            buf.at[s, 0:r], xs_hbm.at[0:r], sem.at[s]))

    @pl.when(i > 0)
    def _():
        wait_tile(i - 1)

    @pl.when(i == pl.num_programs(0) - 1)
    def _():
        wait_tile(i)


def _dispatch(h, posg, nchunks, dest, tail_start, tail_n, *, nb):
    T, D = h.shape
    nt, _, tm = posg.shape
    ne = tail_n.shape[0]
    return pl.pallas_call(
        functools.partial(_dispatch_kernel, ne=ne),
        grid_spec=pltpu.PrefetchScalarGridSpec(
            num_scalar_prefetch=4, grid=(nt,),
            in_specs=[pl.BlockSpec((tm, D), lambda i, *_: (i, 0)),
                      pl.BlockSpec((1, 2 * TOP_K, tm), lambda i, *_: (i, 0, 0))],
            out_specs=pl.BlockSpec(memory_space=pl.ANY),
            scratch_shapes=[pltpu.VMEM((2, _slots(tm, ne), D), BF16),
                            pltpu.VMEM((CHUNK, D), BF16),
                            pltpu.SemaphoreType.DMA((2,)), pltpu.SemaphoreType.DMA(())]),
        out_shape=jax.ShapeDtypeStruct((nb * MOE_BLOCK, D), BF16),
        compiler_params=_cparams(("arbitrary",)),
        name="dispatch",
    )(nchunks, dest, tail_start, tail_n, h, posg)


def _experts_kernel(be_ref, nu_ref, nxt_ref, edge_ref, xs_ref, wgu_hbm, bgu_ref, wd_hbm, bd_ref,
                    ys_ref, stage_gu, stage_d, wgu_s, wd_s, sem):
    b = pl.program_id(0)

    @pl.when(b < nu_ref[0])
    def _():
        edge = edge_ref[b]
        half = edge // 4
        nxt = nxt_ref[b]

        def weight_copies(e):
            return (pltpu.make_async_copy(wgu_hbm.at[e], stage_gu, sem.at[0]),
                    pltpu.make_async_copy(wd_hbm.at[e], stage_d, sem.at[1]))

        def cast_into(h):
            wgu_s[h] = stage_gu[...].astype(BF16)
            wd_s[h] = stage_d[...].astype(BF16)

        @pl.when(b == 0)
        def _():
            for cp in weight_copies(be_ref[b]):
                cp.start()
            for cp in weight_copies(be_ref[b]):
                cp.wait()
            cast_into(half)

        @pl.when((edge % 2 == 1) & (nxt >= 0))
        def _():
            for cp in weight_copies(nxt):
                cp.start()

        ff = wd_s.shape[1]
        gu = jnp.dot(xs_ref[...], wgu_s[half], preferred_element_type=F32) + bgu_ref[0]
        g = jnp.minimum(gu[:, :ff], SWIGLU_LIMIT)
        u = jnp.clip(gu[:, ff:], -SWIGLU_LIMIT, SWIGLU_LIMIT)
        act = (u + 1.0) * g * jax.nn.sigmoid(SWIGLU_ALPHA * g)
        ys = jnp.dot(act.astype(BF16), wd_s[half], preferred_element_type=F32) + bd_ref[0]
        ys_ref[...] = ys.astype(ys_ref.dtype)

        @pl.when(((edge // 2) % 2 == 1) & (nxt >= 0))
        def _():
            for cp in weight_copies(nxt):
                cp.wait()
            cast_into(1 - half)


def _experts(xs, blk_e, nused, nxt, edge, w_gate_up, b_gate_up, w_down, b_down):
    rows, D = xs.shape
    nb = rows // MOE_BLOCK
    ne, _, ff2 = w_gate_up.shape
    ff = w_down.shape[1]
    blk = lambda b, be, nu, *_: (jnp.minimum(b, nu[0] - 1), 0)
    exp3 = lambda b, be, *_: (be[b], 0, 0)
    return pl.pallas_call(
        _experts_kernel,
        grid_spec=pltpu.PrefetchScalarGridSpec(
            num_scalar_prefetch=4, grid=(nb,),
            in_specs=[pl.BlockSpec((MOE_BLOCK, D), blk),
                      pl.BlockSpec(memory_space=pl.ANY), pl.BlockSpec((1, 1, ff2), exp3),
                      pl.BlockSpec(memory_space=pl.ANY), pl.BlockSpec((1, 1, D), exp3)],
            out_specs=pl.BlockSpec((MOE_BLOCK, D), blk),
            scratch_shapes=[pltpu.VMEM((D, ff2), F32), pltpu.VMEM((ff, D), F32),
                            pltpu.VMEM((2, D, ff2), BF16), pltpu.VMEM((2, ff, D), BF16),
                            pltpu.SemaphoreType.DMA((2,))]),
        out_shape=jax.ShapeDtypeStruct((rows, D), BF16),
        compiler_params=_cparams(("arbitrary",)),
        name="experts",
    )(blk_e, nused, nxt, edge, xs, w_gate_up, b_gate_up[:, None, :], w_down,
      b_down[:, None, :])


def _combine_kernel(n_ref, dest_ref, ys_hbm, posg_ref, x1_ref, y_ref, ylast_ref, buf, sem):
    i = pl.program_id(0)
    last = pl.num_programs(0) - 1
    tm = x1_ref.shape[0]
    nslots = buf.shape[1]

    def run_copy(tile, c):
        s = tile % 2
        return pltpu.make_async_copy(ys_hbm.at[_chunk_rows(dest_ref[tile, c])],
                                     buf.at[s, _chunk_rows(c)], sem.at[s])

    def fetch(tile):
        _for_each_chunk(n_ref[tile], lambda c: run_copy(tile, c).start(),
                        unroll=DMA_ISSUE_UNROLL)
        s = tile % 2

        def zero_chunk(c, carry):
            buf[s, _chunk_rows(c), :] = jnp.zeros((CHUNK, buf.shape[2]), BF16)
            return carry
        lax.fori_loop(n_ref[tile], nslots // CHUNK, zero_chunk, 0)

    @pl.when(i == 0)
    def _():
        fetch(i)

    @pl.when(i < last)
    def _():
        fetch(i + 1)

    r0 = lax.broadcasted_iota(I32, (tm, tm), 0)
    r1 = lax.broadcasted_iota(I32, (tm, tm), 1)
    posg_t = _dot_nt(jnp.where(r0 == r1, 1.0, 0.0), posg_ref[0], exact=True)
    slot = lax.broadcasted_iota(I32, (tm, nslots), 1)
    w = jnp.zeros(slot.shape, F32)
    for k in range(TOP_K):
        w = jnp.where(slot == posg_t[:, k:k + 1].astype(I32),
                      posg_t[:, TOP_K + k:TOP_K + k + 1], w)

    _wait_chunks(n_ref[i], lambda r: pltpu.make_async_copy(
        ys_hbm.at[0:r], buf.at[i % 2, 0:r], sem.at[i % 2]))
    y = x1_ref[...] + jnp.dot(w.astype(BF16), buf[i % 2], preferred_element_type=F32)

    @pl.when(i < last)
    def _():
        y_ref[...] = y

    @pl.when(i == last)
    def _():
        ylast_ref[...] = y


def _combine(ys, posg, x1, nchunks, dest):
    T, D = x1.shape
    nt, _, tm = posg.shape
    return pl.pallas_call(
        _combine_kernel,
        grid_spec=pltpu.PrefetchScalarGridSpec(
            num_scalar_prefetch=2, grid=(nt,),
            in_specs=[pl.BlockSpec(memory_space=pl.ANY),
                      pl.BlockSpec((1, 2 * TOP_K, tm), lambda i, *_: (i, 0, 0)),
                      pl.BlockSpec((tm, D), lambda i, *_: (i, 0))],
            out_specs=[pl.BlockSpec((tm, D), lambda i, *_: (jnp.minimum(i, nt - 2), 0)),
                       pl.BlockSpec((tm, D), lambda i, *_: (0, 0))],
            scratch_shapes=[pltpu.VMEM((2, dest.shape[1] * CHUNK, D), BF16),
                            pltpu.SemaphoreType.DMA((2,))]),
        out_shape=[jax.ShapeDtypeStruct((T - tm, D), F32), jax.ShapeDtypeStruct((tm, D), F32)],
        compiler_params=_cparams(("arbitrary",)),
        name="combine",
    )(nchunks, dest, ys, posg, x1)


def _moe(h, posg, cnt, x1, w_gate_up, b_gate_up, w_down, b_down):
    nt, _, tm = posg.shape
    ne = cnt.shape[1]
    max_chunks = (nt * tm * TOP_K) // CHUNK + nt * ne
    nb = -(-max_chunks // CHUNKS_PER_BLOCK) + ne
    nchunks, dest, tail_start, tail_n, nused, blk_e, nxt, edge = _moe_tables(
        cnt, nb, _slots(tm, ne) // CHUNK)
    xs = _dispatch(h, posg, nchunks, dest, tail_start, tail_n, nb=nb)
    ys = _experts(xs, blk_e, nused, nxt, edge, w_gate_up, b_gate_up, w_down, b_down)
    return _combine(ys, posg, x1, nchunks, dest)


def kernel(x_prompt, x_sample, cache_k, cache_v, state_conv, page_table, norm1_w, w_in,
           q_norm_w, k_norm_w, lambda_q1, lambda_k1, lambda_q2, lambda_k2, subln_w,
           conv_w, conv_b, conv_ln_g, conv_ln_b, w_out, norm2_w, router_w, router_b,
           w_gate_up, b_gate_up, w_down, b_down):
    B, S, D = x_prompt.shape
    Bs, Ss, _ = x_sample.shape
    depth = norm1_w.shape[0]
    n_phys, page, heads, _, qk = cache_k.shape[1:]
    vdim = cache_v.shape[-1]
    qc, vc, cc = heads * 2 * qk, heads * vdim, conv_w.shape[2]
    taps = conv_w.shape[1]
    assert Ss == 1 and qk == QK_GROUP and vdim == V7X_LANES and (B * S) % MOE_TILE == 0
    assert Bs <= MOE_TILE and S >= taps - 1
    n_past = page_table.shape[1] * page
    T = B * S
    nt_p = T // MOE_TILE
    pos_p = jnp.arange(S, dtype=F32)
    pos_s = jnp.full((Bs,), n_past, F32)
    xp = x_prompt.reshape(T, D)
    xs = x_sample.reshape(Bs, D)
    pad_tile = lambda a: jnp.pad(a, ((0, MOE_TILE - Bs), (0, 0)))
    outs = [[] for _ in range(6)]
    for l in range(depth):
        lam_init = 0.8 - 0.6 * math.exp(-0.3 * l)
        lamv = jnp.stack([lambda_q1[l], lambda_k1[l], lambda_q2[l], lambda_k2[l]])
        conv_p = (conv_w[l], conv_b[l], conv_ln_g[l], conv_ln_b[l])
        proj_p = (norm1_w[l], w_in[l], q_norm_w[l], k_norm_w[l])
        tail_p = (w_out[l], norm2_w[l], router_w[l], router_b[l])

        q, kt, v, u, kbt, vb = _proj(xp, pos_p, S // PROJ_TILE, PROJ_TILE, *proj_p,
                                     qc=qc, vc=vc, cc=cc, exact_norm=False, attn_layout=True)
        qs, ks_, vs_, us = _proj(xs, pos_s, 1, Bs, *proj_p, qc=qc, vc=vc, cc=cc,
                                 exact_norm=True, attn_layout=False)
        cache_kt = jnp.transpose(cache_k[l], (0, 2, 3, 4, 1)).reshape(n_phys, qc, page)
        cache_vr = cache_v[l].reshape(n_phys, page * heads, vdim)
        attn, attn_s = _attention(q, kbt, vb, qs, ks_, vs_, cache_kt, cache_vr, page_table, lamv,
                                  subln_w[l], lam_init=lam_init)

        bufs = _tail(attn, u, xp, *tail_p, tm=MOE_TILE, n_valid=MOE_TILE,
                     total_tiles=nt_p + 1, conv_params=conv_p, seq=S)
        outs[0].append(jnp.transpose(kt.reshape(B, heads, 2, qk, S), (0, 4, 1, 2, 3)))
        outs[1].append(v.reshape(B, S, heads, vdim))
        outs[2].append(u.reshape(B, S, cc)[:, S - (taps - 1):])

        conv_s = _conv_step(state_conv[l], us, *conv_p)
        x1, h, posg, cnt = _tail(pad_tile(attn_s), pad_tile(conv_s), pad_tile(xs), *tail_p,
                                 tm=MOE_TILE, n_valid=Bs, total_tiles=nt_p + 1,
                                 first_tile=nt_p, into=bufs)
        outs[3].append(ks_.reshape(Bs, Ss, heads, 2, qk))
        outs[4].append(vs_.reshape(Bs, Ss, heads, vdim))
        outs[5].append(jnp.concatenate([state_conv[l][:, Ss:], us[:, None, :]], axis=1))

        xp, y_last = _moe(h, posg, cnt[:, :, 0], x1, w_gate_up[l], b_gate_up[l], w_down[l],
                          b_down[l])
        xs = y_last[:Bs]
    return (xp.reshape(B, S, D), xs.reshape(Bs, Ss, D)) + tuple(jnp.stack(o) for o in outs)
```
